```python
import jax, jax.numpy as jnp
from jax import lax
import numpy as np

D_MODEL = 4096
BATCH = 1
SEQ = 8192
DEPTH = 1

HEAD_DIM = 64
A_HEADS = 32
A_KV_HEADS = 8
A_GROUP = A_HEADS // A_KV_HEADS
B_HEADS = 32
A_Q_W = A_HEADS * HEAD_DIM
A_KV_W = A_KV_HEADS * HEAD_DIM
B_W = B_HEADS * HEAD_DIM
IN_W = A_Q_W + 2 * A_KV_W + 3 * B_W + 2 * D_MODEL
ROT_DIM = HEAD_DIM // 4
ROPE_THETA = 500000.0
WINDOW = 128
ATTN_BLOCK = 128
GRID_W = 64
NA_ROWS = 8
NA_COLS = 16
NA_HALO = 2 * NA_COLS
N_EXPERTS = 32
TOP_K = 4
D_FF_EXPERT = D_MODEL // 4
SWIGLU_LIMIT = 7.0
SWIGLU_ALPHA = 1.702
MOE_BLOCK = 128
PLE_DIM = 256
RMS_EPS = 1e-6
NEG_INF = -1e30

kernel_name = "hybrid_gated_window_natten_moe_encoder"


def rmsnorm(x, g):
    xf = x.astype(jnp.float32)
    y = xf * lax.rsqrt(jnp.mean(xf * xf, axis=-1, keepdims=True) + RMS_EPS)
    return (y * g.astype(jnp.float32)).astype(x.dtype)


def partial_rope(x):
    s = x.shape[1]
    half = ROT_DIM // 2
    inv = jnp.float32(ROPE_THETA) ** (-jnp.arange(half, dtype=jnp.float32) * (2.0 / ROT_DIM))
    ang = jnp.arange(s, dtype=jnp.float32)[:, None] * inv[None, :]
    cos = jnp.cos(ang)[None, :, None, :]
    sin = jnp.sin(ang)[None, :, None, :]
    xf = x.astype(jnp.float32)
    x1 = xf[..., :half]
    x2 = xf[..., half:ROT_DIM]
    rot = jnp.concatenate([x1 * cos - x2 * sin, x2 * cos + x1 * sin], axis=-1)
    return jnp.concatenate([rot, xf[..., ROT_DIM:]], axis=-1).astype(x.dtype)


def window_gqa(q, k, v, sink):
    b, s = q.shape[0], q.shape[1]
    nb = s // ATTN_BLOCK
    qb = q.reshape(b, nb, ATTN_BLOCK, A_KV_HEADS, A_GROUP, HEAD_DIM)
    pad = ((0, 0), (ATTN_BLOCK, ATTN_BLOCK), (0, 0), (0, 0))
    kp = jnp.pad(k, pad).reshape(b, nb + 2, ATTN_BLOCK, A_KV_HEADS, HEAD_DIM)
    vp = jnp.pad(v, pad).reshape(b, nb + 2, ATTN_BLOCK, A_KV_HEADS, HEAD_DIM)
    kblk = jnp.concatenate([kp[:, :-2], kp[:, 1:-1], kp[:, 2:]], axis=2)
    vblk = jnp.concatenate([vp[:, :-2], vp[:, 1:-1], vp[:, 2:]], axis=2)
    sc = jnp.einsum('bnqhgd,bnkhd->bnhgqk', qb, kblk).astype(jnp.float32) * (HEAD_DIM ** -0.5)
    n = jnp.arange(nb)
    q_pos = n[:, None] * ATTN_BLOCK + jnp.arange(ATTN_BLOCK)[None, :]
    k_pos = n[:, None] * ATTN_BLOCK - ATTN_BLOCK + jnp.arange(3 * ATTN_BLOCK)[None, :]
    mask = ((jnp.abs(q_pos[:, :, None] - k_pos[:, None, :]) <= WINDOW)
            & (k_pos >= 0)[:, None, :] & (k_pos < s)[:, None, :])
    sc = jnp.where(mask[None, :, None, None], sc, NEG_INF)
    snk = sink.astype(jnp.float32).reshape(A_KV_HEADS, A_GROUP)[None, None, :, :, None]
    m = jnp.maximum(jnp.max(sc, axis=-1), snk)
    pr = jnp.exp(sc - m[..., None])
    denom = jnp.sum(pr, axis=-1) + jnp.exp(snk - m)
    attn = (pr / denom[..., None]).astype(v.dtype)
    out = jnp.einsum('bnhgqk,bnkhd->bnqhgd', attn, vblk)
    return out.reshape(b, s, A_HEADS * HEAD_DIM)


def neighbourhood_attn(q, k, v, rpb):
    b, s = q.shape[0], q.shape[1]
    rows = s // GRID_W
    kr = min(NA_ROWS, rows)
    ncb = GRID_W // NA_COLS
    qg = q.reshape(b, rows, GRID_W, B_HEADS, HEAD_DIM)
    kg = k.reshape(b, rows, GRID_W, B_HEADS, HEAD_DIM)
    vg = v.reshape(b, rows, GRID_W, B_HEADS, HEAD_DIM)
    halo_start = np.clip(np.arange(ncb) * NA_COLS - NA_COLS // 2, 0, GRID_W - NA_HALO)
    key_cols = halo_start[:, None] + np.arange(NA_HALO)[None, :]
    q_cols = np.arange(GRID_W).reshape(ncb, NA_COLS)
    q_cstart = np.clip(q_cols - NA_COLS // 2, 0, GRID_W - NA_COLS)
    col_mask = ((key_cols[:, None, :] >= q_cstart[:, :, None])
                & (key_cols[:, None, :] < q_cstart[:, :, None] + NA_COLS))
    dc_idx = np.clip(key_cols[:, None, :] - q_cols[:, :, None] + NA_COLS - 1, 0, 2 * NA_COLS - 2)
    col_mask_j = jnp.asarray(col_mask)[None, None, :, :, None, :]
    scale = HEAD_DIM ** -0.5

    def row_step(args):
        r, q_row = args
        rs = jnp.clip(r - kr // 2, 0, rows - kr)
        k_rows = lax.dynamic_slice_in_dim(kg, rs, kr, axis=1)
        v_rows = lax.dynamic_slice_in_dim(vg, rs, kr, axis=1)
        k_nb = k_rows[:, :, key_cols]
        v_nb = v_rows[:, :, key_cols]
        qb = q_row.reshape(b, ncb, NA_COLS, B_HEADS, HEAD_DIM)
        sc = jnp.einsum('bnqhd,banphd->bhnqap', qb, k_nb).astype(jnp.float32) * scale
        dr_idx = rs + jnp.arange(kr) - r + NA_ROWS - 1
        bias = rpb[:, dr_idx][:, :, dc_idx]
        bias = jnp.transpose(bias, (0, 2, 3, 1, 4)).astype(jnp.float32)
        sc = jnp.where(col_mask_j, sc + bias[None], NEG_INF)
        shp = sc.shape
        attn = jax.nn.softmax(sc.reshape(shp[:4] + (kr * NA_HALO,)), axis=-1)
        attn = attn.reshape(shp).astype(v.dtype)
        o = jnp.einsum('bhnqap,banphd->bnqhd', attn, v_nb)
        return o.reshape(b, GRID_W, B_HEADS * HEAD_DIM)

    out = lax.map(row_step, (jnp.arange(rows), jnp.moveaxis(qg, 1, 0)))
    return jnp.moveaxis(out, 0, 1).reshape(b, s, B_HEADS * HEAD_DIM)


def moe(xn, w_router, b_router, w_gu, b_gu, w_dn, b_dn):
    b, s, d = xn.shape
    xf = xn.reshape(-1, d)
    n = xf.shape[0]
    logits = (xf @ w_router).astype(jnp.float32) + b_router.astype(jnp.float32)
    top_v, top_e = lax.top_k(logits, TOP_K)
    gates = jax.nn.softmax(top_v, axis=-1)
    e_flat = top_e.reshape(-1)
    tok_flat = jnp.arange(n * TOP_K, dtype=jnp.int32) // TOP_K
    g_flat = gates.reshape(-1)
    order = jnp.argsort(e_flat)
    se, stok, sg = e_flat[order], tok_flat[order], g_flat[order]
    counts = jnp.bincount(e_flat, length=N_EXPERTS)
    starts = jnp.cumsum(counts) - counts
    padded = (counts + MOE_BLOCK - 1) // MOE_BLOCK * MOE_BLOCK
    pends = jnp.cumsum(padded)
    pstarts = pends - padded
    dest = pstarts[se] + jnp.arange(n * TOP_K) - starts[se]
    n_blocks = -(-(n * TOP_K) // MOE_BLOCK) + N_EXPERTS
    n_rows = n_blocks * MOE_BLOCK
    row_tok = jnp.zeros((n_rows,), jnp.int32).at[dest].set(stok)
    row_g = jnp.zeros((n_rows,), jnp.float32).at[dest].set(sg)
    block_e = jnp.clip(jnp.searchsorted(pends, jnp.arange(n_blocks) * MOE_BLOCK, side='right'),
                       0, N_EXPERTS - 1)

    def block_fn(args):
        e, toks, g = args
        xb = xf[toks]
        hgu = xb @ w_gu[e] + b_gu[e]
        gate = jnp.minimum(hgu[:, ::2], SWIGLU_LIMIT)
        up = jnp.clip(hgu[:, 1::2], -SWIGLU_LIMIT, SWIGLU_LIMIT)
        act = gate * jax.nn.sigmoid(gate * SWIGLU_ALPHA) * (up + 1.0)
        out = act @ w_dn[e] + b_dn[e]
        return out * g.astype(out.dtype)[:, None]

    outs = lax.map(block_fn, (block_e, row_tok.reshape(n_blocks, MOE_BLOCK),
                              row_g.reshape(n_blocks, MOE_BLOCK)))
    y = jnp.zeros((n, d), xf.dtype).at[row_tok].add(outs.reshape(n_rows, d))
    return y.reshape(b, s, d)


def setup_inputs(seed: int = 0) -> dict:
    key = jax.random.key(seed)
    ks = jax.random.split(key, 24)
    f32 = jnp.float32
    L, D, E, F = DEPTH, D_MODEL, N_EXPERTS, D_FF_EXPERT

    def nrm(k, shape, scale):
        return jax.random.normal(k, shape, f32) * scale

    return {
        "x": nrm(ks[0], (BATCH, SEQ, D), 1.0),
        "p": nrm(ks[1], (DEPTH, BATCH, SEQ, PLE_DIM), 1.0),
        "g_mix": 1.0 + nrm(ks[2], (L, D), 0.05),
        "w_in": nrm(ks[3], (L, D, IN_W), D ** -0.5),
        "attn_sink": nrm(ks[4], (L, A_HEADS), 1.0),
        "na_rpb": nrm(ks[5], (L, B_HEADS, 2 * NA_ROWS - 1, 2 * NA_COLS - 1), 0.5),
        "w_branch_a": nrm(ks[6], (L, A_Q_W, D), A_Q_W ** -0.5),
        "w_branch_b": nrm(ks[7], (L, B_W, D), B_W ** -0.5),
        "w_out": nrm(ks[8], (L, D, D), D ** -0.5),
        "g_ffn": 1.0 + nrm(ks[9], (L, D), 0.05),
        "w_router": nrm(ks[10], (L, D, E), D ** -0.5),
        "b_router": nrm(ks[11], (L, E), 0.01),
        "w_gate_up": nrm(ks[12], (L, E, D, 2 * F), D ** -0.5),
        "b_gate_up": nrm(ks[13], (L, E, 2 * F), 0.02),
        "w_down": nrm(ks[14], (L, E, F, D), F ** -0.5),
        "b_down": nrm(ks[15], (L, E, D), 0.02),
        "g_ple": 1.0 + nrm(ks[16], (L, D), 0.05),
        "w_ple_gate": nrm(ks[17], (L, D, D), D ** -0.5),
        "w_ple": nrm(ks[18], (L, PLE_DIM, D), PLE_DIM ** -0.5),
        "g_final": 1.0 + nrm(ks[19], (D,), 0.05),
    }


def reference(x, p, g_mix, w_in, attn_sink, na_rpb, w_branch_a, w_branch_b, w_out,
              g_ffn, w_router, b_router, w_gate_up, b_gate_up, w_down, b_down,
              g_ple, w_ple_gate, w_ple, g_final):
    b, s, _ = x.shape
    splits = [int(v) for v in np.cumsum([A_Q_W, A_KV_W, A_KV_W, B_W, B_W, B_W, D_MODEL])]
    h = x
    for i in range(DEPTH):
        xn = rmsnorm(h, g_mix[i])
        proj = xn @ w_in[i]
        qa, ka, va, qb, kb, vb, ga, gb = jnp.split(proj, splits, axis=-1)
        qa = partial_rope(qa.reshape(b, s, A_HEADS, HEAD_DIM))
        ka = partial_rope(ka.reshape(b, s, A_KV_HEADS, HEAD_DIM))
        va = va.reshape(b, s, A_KV_HEADS, HEAD_DIM)
        ya = window_gqa(qa, ka, va, attn_sink[i])
        yb = neighbourhood_attn(qb.reshape(b, s, B_HEADS, HEAD_DIM),
                                kb.reshape(b, s, B_HEADS, HEAD_DIM),
                                vb.reshape(b, s, B_HEADS, HEAD_DIM), na_rpb[i])
        merged = (jax.nn.sigmoid(ga) * (ya @ w_branch_a[i])
                  + jax.nn.sigmoid(gb) * (yb @ w_branch_b[i]))
        h = h + merged @ w_out[i]
        h = h + moe(rmsnorm(h, g_ffn[i]), w_router[i], b_router[i], w_gate_up[i],
                    b_gate_up[i], w_down[i], b_down[i])
        ple_gate = jax.nn.sigmoid(rmsnorm(h, g_ple[i]) @ w_ple_gate[i])
        h = h + ple_gate * (p[i] @ w_ple[i])
    return rmsnorm(h, g_final)
```

```python
import functools

import numpy as np
import jax
import jax.numpy as jnp
from jax import lax
from jax.experimental import pallas as pl
from jax.experimental.pallas import tpu as pltpu

F32 = jnp.float32
BF16 = jnp.bfloat16

HEAD_DIM = 64
A_HEADS = 32
A_KV_HEADS = 8
A_GROUP = A_HEADS // A_KV_HEADS
B_HEADS = 32
A_Q_W = A_HEADS * HEAD_DIM
A_KV_W = A_KV_HEADS * HEAD_DIM
B_W = B_HEADS * HEAD_DIM
ROT_DIM = HEAD_DIM // 4
ROPE_THETA = 500000.0
WINDOW = 128
GRID_W = 64
NA_ROWS = 8
NA_COLS = 16
N_EXPERTS = 32
TOP_K = 4
SWIGLU_LIMIT = 7.0
SWIGLU_ALPHA = 1.702
RMS_EPS = 1e-6
NEG_INF = -1e30

LANES = 128
VMEM_LIMIT = 56 * 1024 * 1024

RMS_ROWS = 256
MM_TM = 512
MM_TN = 512
ATT_BLK = 128
NA_RB = 8
MOE_TM = 512
MOE_TF = 128
CMB_TB = 128


def _params(sem):
    return pltpu.CompilerParams(dimension_semantics=sem, vmem_limit_bytes=VMEM_LIMIT)


def _rms_body(x_ref, g_ref):
    x = x_ref[...].astype(F32)
    ms = jnp.mean(x * x, axis=-1, keepdims=True)
    return x * lax.rsqrt(ms + RMS_EPS) * g_ref[...]


def _rms_kernel(x_ref, g_ref, o_ref):
    o_ref[...] = _rms_body(x_ref, g_ref).astype(o_ref.dtype)


def rmsnorm_call(x, g, out_dtype):
    s, d = x.shape
    return pl.pallas_call(
        _rms_kernel,
        grid=(s // RMS_ROWS,),
        in_specs=[pl.BlockSpec((RMS_ROWS, d), lambda i: (i, 0)),
                  pl.BlockSpec((1, d), lambda i: (0, 0))],
        out_specs=pl.BlockSpec((RMS_ROWS, d), lambda i: (i, 0)),
        out_shape=jax.ShapeDtypeStruct((s, d), out_dtype),
        compiler_params=_params(("parallel",)),
        name="rmsnorm",
    )(x, g.reshape(1, d))


def _rms_router_kernel(x_ref, g_ref, wr_ref, br_ref, xn_ref, e_ref, gt_ref):
    xn = _rms_body(x_ref, g_ref)
    xn_ref[...] = xn
    logits = jnp.dot(xn, wr_ref[...], preferred_element_type=F32,
                     precision=lax.Precision.HIGHEST) + br_ref[...]
    lane = lax.broadcasted_iota(jnp.int32, logits.shape, 1)
    logits = jnp.where(lane < N_EXPERTS, logits, -jnp.inf)
    e_out = jnp.zeros(logits.shape, jnp.int32)
    v_out = jnp.zeros(logits.shape, F32)
    v0 = None
    for k in range(TOP_K):
        m = jnp.max(logits, axis=-1, keepdims=True)
        idx = jnp.min(jnp.where(logits == m, lane, LANES), axis=-1, keepdims=True)
        if k == 0:
            v0 = m
        e_out = jnp.where(lane == k, idx, e_out)
        v_out = jnp.where(lane == k, jnp.exp(m - v0), v_out)
        logits = jnp.where(lane == idx, -jnp.inf, logits)
    e_ref[...] = e_out
    gt_ref[...] = v_out / jnp.sum(v_out, axis=-1, keepdims=True)


def rms_router_call(h, g, w_router, b_router):
    s, d = h.shape
    wr = jnp.zeros((d, LANES), F32).at[:, :N_EXPERTS].set(w_router)
    br = jnp.zeros((1, LANES), F32).at[0, :N_EXPERTS].set(b_router)
    return pl.pallas_call(
        _rms_router_kernel,
        grid=(s // RMS_ROWS,),
        in_specs=[pl.BlockSpec((RMS_ROWS, d), lambda i: (i, 0)),
                  pl.BlockSpec((1, d), lambda i: (0, 0)),
                  pl.BlockSpec((d, LANES), lambda i: (0, 0)),
                  pl.BlockSpec((1, LANES), lambda i: (0, 0))],
        out_specs=[pl.BlockSpec((RMS_ROWS, d), lambda i: (i, 0)),
                   pl.BlockSpec((RMS_ROWS, LANES), lambda i: (i, 0)),
                   pl.BlockSpec((RMS_ROWS, LANES), lambda i: (i, 0))],
        out_shape=[jax.ShapeDtypeStruct((s, d), F32),
                   jax.ShapeDtypeStruct((s, LANES), jnp.int32),
                   jax.ShapeDtypeStruct((s, LANES), F32)],
        compiler_params=_params(("parallel",)),
        name="rms_router",
    )(h, g.reshape(1, d), wr, br)


def _rope(acc, c_ref, s1_ref, s2_ref):
    c, s1, s2 = c_ref[...], s1_ref[...], s2_ref[...]
    half = ROT_DIM // 2
    outs = []
    for t in range(acc.shape[1] // LANES):
        x = acc[:, t * LANES:(t + 1) * LANES]
        outs.append(x * c + pltpu.roll(x, LANES - half, 1) * s1 + pltpu.roll(x, half, 1) * s2)
    return jnp.concatenate(outs, axis=1)


def _inproj_kernel(a_ref, w_ref, c_ref, s1_ref, s2_ref, o_ref, wb_ref, *, n_rope, n_plain):
    j = pl.program_id(0)

    @pl.when(pl.program_id(1) == 0)
    def _():
        wb_ref[...] = w_ref[...].astype(BF16)

    acc = jnp.dot(a_ref[...], wb_ref[...], preferred_element_type=F32)

    @pl.when(j < n_rope)
    def _():
        o_ref[...] = _rope(acc, c_ref, s1_ref, s2_ref).astype(o_ref.dtype)

    @pl.when(jnp.logical_and(j >= n_rope, j < n_plain))
    def _():
        o_ref[...] = acc.astype(o_ref.dtype)

    @pl.when(j >= n_plain)
    def _():
        o_ref[...] = jax.nn.sigmoid(acc).astype(o_ref.dtype)


def inproj_call(xn, w_in, rope_c, rope_s1, rope_s2, d_model):
    s, d = xn.shape
    n = w_in.shape[1]
    n_rope = (A_Q_W + A_KV_W) // MM_TN
    n_plain = (n - 2 * d_model) // MM_TN
    tab = pl.BlockSpec((MM_TM, LANES), lambda j, i: (i, 0))
    return pl.pallas_call(
        functools.partial(_inproj_kernel, n_rope=n_rope, n_plain=n_plain),
        grid=(n // MM_TN, s // MM_TM),
        in_specs=[pl.BlockSpec((MM_TM, d), lambda j, i: (i, 0)),
                  pl.BlockSpec((d, MM_TN), lambda j, i: (0, j)),
                  tab, tab, tab],
        out_specs=pl.BlockSpec((MM_TM, MM_TN), lambda j, i: (i, j)),
        out_shape=jax.ShapeDtypeStruct((s, n), BF16),
        scratch_shapes=[pltpu.VMEM((d, MM_TN), BF16)],
        compiler_params=_params(("parallel", "arbitrary")),
        name="inproj",
    )(xn, w_in, rope_c, rope_s1, rope_s2)


def rope_tables(s):
    half = ROT_DIM // 2
    inv = jnp.float32(ROPE_THETA) ** (-jnp.arange(half, dtype=F32) * (2.0 / ROT_DIM))
    ang = jnp.arange(s, dtype=F32)[:, None] * inv[None, :]
    cos, sin = jnp.cos(ang), jnp.sin(ang)
    ones = jnp.ones((s, HEAD_DIM - ROT_DIM), F32)
    zeros = jnp.zeros((s, HEAD_DIM - ROT_DIM), F32)
    zh = jnp.zeros((s, half), F32)
    c = jnp.concatenate([cos, cos, ones], axis=1)
    s1 = jnp.concatenate([-sin, zh, zeros], axis=1)
    s2 = jnp.concatenate([zh, sin, zeros], axis=1)
    rep = LANES // HEAD_DIM
    return jnp.tile(c, (1, rep)), jnp.tile(s1, (1, rep)), jnp.tile(s2, (1, rep))


def _win_kernel(sink_ref, q_ref, kp_ref, kc_ref, kn_ref, vp_ref, vc_ref, vn_ref, o_ref, *, seq):
    n = pl.program_id(0)
    pr = pl.program_id(1)
    kcat = jnp.concatenate([kp_ref[...], kc_ref[...], kn_ref[...]], axis=0)
    vcat = jnp.concatenate([vp_ref[...], vc_ref[...], vn_ref[...]], axis=0)
    q = q_ref[...]
    rows = A_GROUP * ATT_BLK
    qpos = n * ATT_BLK + lax.broadcasted_iota(jnp.int32, (rows, 3 * ATT_BLK), 0) % ATT_BLK
    kpos = (n - 1) * ATT_BLK + lax.broadcasted_iota(jnp.int32, (rows, 3 * ATT_BLK), 1)
    mask = (jnp.abs(qpos - kpos) <= WINDOW) & (kpos >= 0) & (kpos < seq)
    grp = lax.broadcasted_iota(jnp.int32, (rows, 1), 0) // ATT_BLK
    kv_per_blk = LANES // HEAD_DIM
    for kvh in range(kv_per_blk):
        k_h = kcat[:, kvh * HEAD_DIM:(kvh + 1) * HEAD_DIM]
        v_h = vcat[:, kvh * HEAD_DIM:(kvh + 1) * HEAD_DIM]
        h0 = kvh * A_GROUP
        qs = jnp.concatenate(
            [q[:, (h0 + g) * HEAD_DIM:(h0 + g + 1) * HEAD_DIM] for g in range(A_GROUP)], axis=0)
        sc = lax.dot_general(qs, k_h, (((1,), (1,)), ((), ())),
                             preferred_element_type=F32) * (HEAD_DIM ** -0.5)
        sc = jnp.where(mask, sc, NEG_INF)
        snk = jnp.zeros((rows, 1), F32)
        for g in range(A_GROUP):
            sv = sink_ref[(pr * kv_per_blk + kvh) * A_GROUP + g]
            snk = jnp.where(grp == g, sv, snk)
        m = jnp.maximum(jnp.max(sc, axis=-1, keepdims=True), snk)
        p = jnp.exp(sc - m)
        denom = jnp.sum(p, axis=-1, keepdims=True) + jnp.exp(snk - m)
        attn = (p / denom).astype(BF16)
        out = jnp.dot(attn, v_h, preferred_element_type=F32)
        for g in range(A_GROUP):
            o_ref[:, (h0 + g) * HEAD_DIM:(h0 + g + 1) * HEAD_DIM] = (
                out[g * ATT_BLK:(g + 1) * ATT_BLK].astype(o_ref.dtype))


def window_attn_call(proj, sink):
    s = proj.shape[0]
    nb = s // ATT_BLK
    qw = A_GROUP * LANES
    k0 = A_Q_W // LANES
    v0 = (A_Q_W + A_KV_W) // LANES
    blk = (ATT_BLK, LANES)
    prev = lambda n: jnp.maximum(n - 1, 0)
    nxt = lambda n: jnp.minimum(n + 1, nb - 1)
    return pl.pallas_call(
        functools.partial(_win_kernel, seq=s),
        grid=(nb, A_Q_W // qw),
        in_specs=[pl.BlockSpec(memory_space=pltpu.SMEM),
                  pl.BlockSpec((ATT_BLK, qw), lambda n, p: (n, p)),
                  pl.BlockSpec(blk, lambda n, p: (prev(n), k0 + p)),
                  pl.BlockSpec(blk, lambda n, p: (n, k0 + p)),
                  pl.BlockSpec(blk, lambda n, p: (nxt(n), k0 + p)),
                  pl.BlockSpec(blk, lambda n, p: (prev(n), v0 + p)),
                  pl.BlockSpec(blk, lambda n, p: (n, v0 + p)),
                  pl.BlockSpec(blk, lambda n, p: (nxt(n), v0 + p))],
        out_specs=pl.BlockSpec((ATT_BLK, qw), lambda n, p: (n, p)),
        out_shape=jax.ShapeDtypeStruct((s, A_Q_W), BF16),
        compiler_params=_params(("parallel", "parallel")),
        name="window_attn",
    )(sink, proj, proj, proj, proj, proj, proj, proj)


def na_bias_tables(rpb):
    c = np.arange(GRID_W)
    kc = np.arange(GRID_W)
    dc = np.clip(kc[None, :] - c[:, None] + NA_COLS - 1, 0, 2 * NA_COLS - 2)
    qcs = np.clip(c - NA_COLS // 2, 0, GRID_W - NA_COLS)
    cmask = (kc[None, :] >= qcs[:, None]) & (kc[None, :] < qcs[:, None] + NA_COLS)
    t = jnp.where(jnp.asarray(cmask)[None, None], rpb[:, :, dc].astype(F32), NEG_INF)
    idx = np.arange(NA_ROWS)[:, None] + np.arange(NA_ROWS)[None, :]
    bt = t[:, idx]
    bt = jnp.transpose(bt, (1, 0, 3, 2, 4))
    return bt.reshape(NA_ROWS, rpb.shape[0], GRID_W, NA_ROWS * GRID_W)


def _na_kernel(q_ref, kp_ref, kc_ref, kn_ref, vp_ref, vc_ref, vn_ref, b_ref, o_ref,
               kbuf, vbuf, *, grid_rows):
    rb = pl.program_id(1)
    blk = NA_RB * GRID_W
    kbuf[0:blk] = kp_ref[...]
    kbuf[blk:2 * blk] = kc_ref[...]
    kbuf[2 * blk:3 * blk] = kn_ref[...]
    vbuf[0:blk] = vp_ref[...]
    vbuf[blk:2 * blk] = vc_ref[...]
    vbuf[2 * blk:3 * blk] = vn_ref[...]
    base_row = (rb - 1) * NA_RB
    nkeys = NA_ROWS * GRID_W
    for i in range(NA_RB):
        r = rb * NA_RB + i
        rs = jnp.clip(r - NA_ROWS // 2, 0, grid_rows - NA_ROWS)
        typ = rs - (r - NA_ROWS // 2) + (NA_ROWS // 2 - 1)
        start = pl.multiple_of((rs - base_row) * GRID_W, GRID_W)
        ks = kbuf[pl.ds(start, nkeys), :]
        vs = vbuf[pl.ds(start, nkeys), :]
        qi = q_ref[i * GRID_W:(i + 1) * GRID_W, :]
        for hh in range(LANES // HEAD_DIM):
            sl = slice(hh * HEAD_DIM, (hh + 1) * HEAD_DIM)
            sc = lax.dot_general(qi[:, sl], ks[:, sl], (((1,), (1,)), ((), ())),
                                 preferred_element_type=F32) * (HEAD_DIM ** -0.5)
            sc = sc + b_ref[typ, hh]
            m = jnp.max(sc, axis=-1, keepdims=True)
            p = jnp.exp(sc - m)
            attn = (p / jnp.sum(p, axis=-1, keepdims=True)).astype(BF16)
            out = jnp.dot(attn, vs[:, sl], preferred_element_type=F32)
            o_ref[i * GRID_W:(i + 1) * GRID_W, sl] = out.astype(o_ref.dtype)


def na_attn_call(proj, bias_tab):
    s = proj.shape[0]
    grid_rows = s // GRID_W
    nrb = grid_rows // NA_RB
    blk_rows = NA_RB * GRID_W
    q0 = (A_Q_W + 2 * A_KV_W) // LANES
    k0 = q0 + B_W // LANES
    v0 = k0 + B_W // LANES
    blk = (blk_rows, LANES)
    hpb = LANES // HEAD_DIM
    prev = lambda r: jnp.maximum(r - 1, 0)
    cur = lambda r: r
    nxt = lambda r: jnp.minimum(r + 1, nrb - 1)
    return pl.pallas_call(
        functools.partial(_na_kernel, grid_rows=grid_rows),
        grid=(B_HEADS // hpb, nrb),
        in_specs=[pl.BlockSpec(blk, lambda h, r: (r, q0 + h)),
                  pl.BlockSpec(blk, lambda h, r: (prev(r), k0 + h)),
                  pl.BlockSpec(blk, lambda h, r: (cur(r), k0 + h)),
                  pl.BlockSpec(blk, lambda h, r: (nxt(r), k0 + h)),
                  pl.BlockSpec(blk, lambda h, r: (prev(r), v0 + h)),
                  pl.BlockSpec(blk, lambda h, r: (cur(r), v0 + h)),
                  pl.BlockSpec(blk, lambda h, r: (nxt(r), v0 + h)),
                  pl.BlockSpec((NA_ROWS, hpb, GRID_W, NA_ROWS * GRID_W), lambda h, r: (0, h, 0, 0))],
        out_specs=pl.BlockSpec(blk, lambda h, r: (r, h)),
        out_shape=jax.ShapeDtypeStruct((s, B_W), BF16),
        scratch_shapes=[pltpu.VMEM((3 * blk_rows, LANES), BF16),
                        pltpu.VMEM((3 * blk_rows, LANES), BF16)],
        compiler_params=_params(("parallel", "arbitrary")),
        name="na_attn",
    )(proj, proj, proj, proj, proj, proj, proj, bias_tab)


def _merge_kernel(ya_ref, yb_ref, wa_ref, wb_ref, ga_ref, gb_ref, o_ref, wab, wbb):
    @pl.when(pl.program_id(1) == 0)
    def _():
        wab[...] = wa_ref[...].astype(BF16)
        wbb[...] = wb_ref[...].astype(BF16)

    a = jnp.dot(ya_ref[...], wab[...], preferred_element_type=F32)
    b = jnp.dot(yb_ref[...], wbb[...], preferred_element_type=F32)
    o_ref[...] = (ga_ref[...].astype(F32) * a + gb_ref[...].astype(F32) * b).astype(o_ref.dtype)


def merge_call(ya, yb, w_a, w_b, proj, d_model):
    s = ya.shape[0]
    ga0 = (proj.shape[1] - 2 * d_model) // MM_TN
    gb0 = ga0 + d_model // MM_TN
    return pl.pallas_call(
        _merge_kernel,
        grid=(d_model // MM_TN, s // MM_TM),
        in_specs=[pl.BlockSpec((MM_TM, A_Q_W), lambda j, i: (i, 0)),
                  pl.BlockSpec((MM_TM, B_W), lambda j, i: (i, 0)),
                  pl.BlockSpec((A_Q_W, MM_TN), lambda j, i: (0, j)),
                  pl.BlockSpec((B_W, MM_TN), lambda j, i: (0, j)),
                  pl.BlockSpec((MM_TM, MM_TN), lambda j, i: (i, ga0 + j)),
                  pl.BlockSpec((MM_TM, MM_TN), lambda j, i: (i, gb0 + j))],
        out_specs=pl.BlockSpec((MM_TM, MM_TN), lambda j, i: (i, j)),
        out_shape=jax.ShapeDtypeStruct((s, d_model), BF16),
        scratch_shapes=[pltpu.VMEM((A_Q_W, MM_TN), BF16), pltpu.VMEM((B_W, MM_TN), BF16)],
        compiler_params=_params(("parallel", "arbitrary")),
        name="merge",
    )(ya, yb, w_a, w_b, proj, proj)


def _resmm_kernel(a_ref, w_ref, r_ref, o_ref, wb_ref):
    @pl.when(pl.program_id(1) == 0)
    def _():
        wb_ref[...] = w_ref[...].astype(BF16)

    o_ref[...] = r_ref[...] + jnp.dot(a_ref[...], wb_ref[...], preferred_element_type=F32)


def resmm_call(a, w, res):
    s, k = a.shape
    n = w.shape[1]
    return pl.pallas_call(
        _resmm_kernel,
        grid=(n // MM_TN, s // MM_TM),
        in_specs=[pl.BlockSpec((MM_TM, k), lambda j, i: (i, 0)),
                  pl.BlockSpec((k, MM_TN), lambda j, i: (0, j)),
                  pl.BlockSpec((MM_TM, MM_TN), lambda j, i: (i, j))],
        out_specs=pl.BlockSpec((MM_TM, MM_TN), lambda j, i: (i, j)),
        out_shape=jax.ShapeDtypeStruct((s, n), F32),
        scratch_shapes=[pltpu.VMEM((k, MM_TN), BF16)],
        compiler_params=_params(("parallel", "arbitrary")),
        name="resmm",
    )(a, w, res)


def _ple_kernel(a_ref, wg_ref, p_ref, wp_ref, h_ref, o_ref, wgb, wpb):
    @pl.when(pl.program_id(1) == 0)
    def _():
        wgb[...] = wg_ref[...].astype(BF16)
        wpb[...] = wp_ref[...].astype(BF16)

    gate = jax.nn.sigmoid(jnp.dot(a_ref[...], wgb[...], preferred_element_type=F32))
    emb = jnp.dot(p_ref[...].astype(BF16), wpb[...], preferred_element_type=F32)
    o_ref[...] = h_ref[...] + gate * emb


def ple_call(xn, w_gate, p, w_ple, h):
    s, d = xn.shape
    pd = p.shape[1]
    return pl.pallas_call(
        _ple_kernel,
        grid=(d // MM_TN, s // MM_TM),
        in_specs=[pl.BlockSpec((MM_TM, d), lambda j, i: (i, 0)),
                  pl.BlockSpec((d, MM_TN), lambda j, i: (0, j)),
                  pl.BlockSpec((MM_TM, pd), lambda j, i: (i, 0)),
                  pl.BlockSpec((pd, MM_TN), lambda j, i: (0, j)),
                  pl.BlockSpec((MM_TM, MM_TN), lambda j, i: (i, j))],
        out_specs=pl.BlockSpec((MM_TM, MM_TN), lambda j, i: (i, j)),
        out_shape=jax.ShapeDtypeStruct((s, d), F32),
        scratch_shapes=[pltpu.VMEM((d, MM_TN), BF16), pltpu.VMEM((pd, MM_TN), BF16)],
        compiler_params=_params(("parallel", "arbitrary")),
        name="ple",
    )(xn, w_gate, p, w_ple, h)


def _row_copy(src_hbm, row, dst, slot, sem):
    return pltpu.make_async_copy(src_hbm.at[pl.ds(row, 1), :], dst.at[pl.ds(slot, 1), :], sem)


def _dispatch_kernel(nused_ref, tok_ref, x_hbm, o_ref, buf, sem):
    m = pl.program_id(0)

    @pl.when(m < nused_ref[0])
    def _():
        def issue(r, c):
            _row_copy(x_hbm, tok_ref[0, 0, r], buf, r, sem).start()
            return c

        lax.fori_loop(0, MOE_TM, issue, 0)

        def drain(r, c):
            _row_copy(x_hbm, 0, buf, r, sem).wait()
            return c

        lax.fori_loop(0, MOE_TM, drain, 0)
        o_ref[...] = buf[...].astype(o_ref.dtype)

    @pl.when(m >= nused_ref[0])
    def _():
        o_ref[...] = jnp.zeros(o_ref.shape, o_ref.dtype)


def dispatch_call(n_used, row_tok, xn):
    s, d = xn.shape
    n_rows = row_tok.shape[0]
    nblk = n_rows // MOE_TM
    return pl.pallas_call(
        _dispatch_kernel,
        grid_spec=pltpu.PrefetchScalarGridSpec(
            num_scalar_prefetch=1,
            grid=(nblk,),
            in_specs=[pl.BlockSpec((1, 1, MOE_TM), lambda m, nu: (m, 0, 0), memory_space=pltpu.SMEM),
                      pl.BlockSpec(memory_space=pl.ANY)],
            out_specs=pl.BlockSpec((MOE_TM, d), lambda m, nu: (m, 0)),
            scratch_shapes=[pltpu.VMEM((MOE_TM, d), F32), pltpu.SemaphoreType.DMA(())]),
        out_shape=jax.ShapeDtypeStruct((n_rows, d), BF16),
        compiler_params=_params(("arbitrary",)),
        name="dispatch",
    )(n_used, row_tok.reshape(nblk, 1, MOE_TM), xn)


def _expert_kernel(be_ref, nused_ref, x_ref, wgu_ref, bgu_ref, wdn_ref, bdn_ref, g_ref, o_ref):
    m = pl.program_id(0)
    j = pl.program_id(1)
    nj = pl.num_programs(1)

    @pl.when(m < nused_ref[0])
    def _():
        hgu = jnp.dot(x_ref[...], wgu_ref[...].astype(BF16), preferred_element_type=F32)
        hgu = hgu + bgu_ref[...]
        gate = jnp.minimum(hgu, SWIGLU_LIMIT)
        up = jnp.clip(hgu, -SWIGLU_LIMIT, SWIGLU_LIMIT)
        up = pltpu.roll(up, 2 * MOE_TF - 1, 1)
        act = gate * jax.nn.sigmoid(gate * SWIGLU_ALPHA) * (up + 1.0)
        rr = lax.broadcasted_iota(jnp.int32, (2 * MOE_TF, MOE_TF), 0)
        cc = lax.broadcasted_iota(jnp.int32, (2 * MOE_TF, MOE_TF), 1)
        sel = (rr == 2 * cc).astype(BF16)
        act_c = jnp.dot(act.astype(BF16), sel, preferred_element_type=F32).astype(BF16)
        contrib = jnp.dot(act_c, wdn_ref[...].astype(BF16), preferred_element_type=F32)

        @pl.when(j == 0)
        def _():
            o_ref[...] = contrib

        @pl.when(j > 0)
        def _():
            o_ref[...] += contrib

        @pl.when(j == nj - 1)
        def _():
            o_ref[...] = (o_ref[...] + bdn_ref[...]) * g_ref[...]

    @pl.when(jnp.logical_and(m >= nused_ref[0], j == 0))
    def _():
        o_ref[...] = jnp.zeros(o_ref.shape, o_ref.dtype)


def expert_call(block_e, n_used, xs, w_gu, b_gu, w_dn, b_dn, row_g):
    n_rows, d = xs.shape
    e, _, f2 = w_gu.shape
    f = f2 // 2
    nblk = n_rows // MOE_TM
    nj = f // MOE_TF

    def jeff(m, j, nu):
        return jnp.where(m < nu[0], j, nj - 1)

    return pl.pallas_call(
        _expert_kernel,
        grid_spec=pltpu.PrefetchScalarGridSpec(
            num_scalar_prefetch=2,
            grid=(nblk, nj),
            in_specs=[pl.BlockSpec((MOE_TM, d), lambda m, j, be, nu: (m, 0)),
                      pl.BlockSpec((None, d, 2 * MOE_TF), lambda m, j, be, nu: (be[m], 0, jeff(m, j, nu))),
                      pl.BlockSpec((None, 1, 2 * MOE_TF), lambda m, j, be, nu: (be[m], 0, jeff(m, j, nu))),
                      pl.BlockSpec((None, MOE_TF, d), lambda m, j, be, nu: (be[m], jeff(m, j, nu), 0)),
                      pl.BlockSpec((None, 1, d), lambda m, j, be, nu: (be[m], 0, 0)),
                      pl.BlockSpec((MOE_TM, 1), lambda m, j, be, nu: (m, 0))],
            out_specs=pl.BlockSpec((MOE_TM, d), lambda m, j, be, nu: (m, 0))),
        out_shape=jax.ShapeDtypeStruct((n_rows, d), F32),
        compiler_params=_params(("arbitrary", "arbitrary")),
        name="experts",
    )(block_e, n_used, xs, w_gu, b_gu.reshape(e, 1, f2), w_dn, b_dn.reshape(e, 1, d),
      row_g.reshape(n_rows, 1))


def _combine_kernel(pos_ref, ys_hbm, h_ref, g_ref, h_out, xn_out, buf, sem):
    def issue(t, c):
        for k in range(TOP_K):
            pltpu.make_async_copy(ys_hbm.at[pl.ds(pos_ref[0, 0, t * TOP_K + k], 1), :],
                                  buf.at[k, pl.ds(t, 1), :], sem).start()
        return c

    lax.fori_loop(0, CMB_TB, issue, 0)

    def drain(t, c):
        for k in range(TOP_K):
            pltpu.make_async_copy(ys_hbm.at[pl.ds(0, 1), :], buf.at[k, pl.ds(t, 1), :], sem).wait()
        return c

    lax.fori_loop(0, CMB_TB, drain, 0)
    h = h_ref[...]
    for k in range(TOP_K):
        h = h + buf[k]
    h_out[...] = h
    ms = jnp.mean(h * h, axis=-1, keepdims=True)
    xn_out[...] = (h * lax.rsqrt(ms + RMS_EPS) * g_ref[...]).astype(xn_out.dtype)


def combine_call(pos, ys, h, g):
    s, d = h.shape
    nb = s // CMB_TB
    return pl.pallas_call(
        _combine_kernel,
        grid=(nb,),
        in_specs=[pl.BlockSpec((1, 1, CMB_TB * TOP_K), lambda i: (i, 0, 0), memory_space=pltpu.SMEM),
                  pl.BlockSpec(memory_space=pl.ANY),
                  pl.BlockSpec((CMB_TB, d), lambda i: (i, 0)),
                  pl.BlockSpec((1, d), lambda i: (0, 0))],
        out_specs=[pl.BlockSpec((CMB_TB, d), lambda i: (i, 0)),
                   pl.BlockSpec((CMB_TB, d), lambda i: (i, 0))],
        out_shape=[jax.ShapeDtypeStruct((s, d), F32), jax.ShapeDtypeStruct((s, d), BF16)],
        scratch_shapes=[pltpu.VMEM((TOP_K, CMB_TB, d), F32), pltpu.SemaphoreType.DMA(())],
        compiler_params=_params(("arbitrary",)),
        name="combine",
    )(pos.reshape(nb, 1, CMB_TB * TOP_K), ys, h, g.reshape(1, d))


def moe_routing(top_e, n_tokens):
    e_flat = top_e.reshape(-1)
    onehot = (e_flat[:, None] == jnp.arange(N_EXPERTS, dtype=jnp.int32)[None, :]).astype(jnp.int32)
    csum = jnp.cumsum(onehot, axis=0)
    counts = csum[-1]
    rank = jnp.sum(csum * onehot, axis=1) - 1
    padded = (counts + MOE_TM - 1) // MOE_TM * MOE_TM
    pends = jnp.cumsum(padded)
    pstarts = pends - padded
    pos = pstarts[e_flat] + rank
    n_blocks = (n_tokens * TOP_K) // MOE_TM + N_EXPERTS
    n_rows = n_blocks * MOE_TM
    tok_flat = jnp.arange(n_tokens * TOP_K, dtype=jnp.int32) // TOP_K
    row_tok = jnp.zeros((n_rows,), jnp.int32).at[pos].set(tok_flat)
    n_used = (pends[-1] // MOE_TM).astype(jnp.int32)
    blk_start = jnp.arange(n_blocks, dtype=jnp.int32) * MOE_TM
    block_e = jnp.clip(jnp.searchsorted(pends, blk_start, side='right'), 0, N_EXPERTS - 1)
    last_e = block_e[jnp.maximum(n_used - 1, 0)]
    block_e = jnp.where(jnp.arange(n_blocks) < n_used, block_e, last_e).astype(jnp.int32)
    return pos.astype(jnp.int32), row_tok, block_e, n_used.reshape(1)


def kernel(x, p, g_mix, w_in, attn_sink, na_rpb, w_branch_a, w_branch_b, w_out, g_ffn, w_router,
           b_router, w_gate_up, b_gate_up, w_down, b_down, g_ple, w_ple_gate, w_ple, g_final):
    b, s, d = x.shape
    assert b == 1 and w_in.shape[0] == 1
    h = x.reshape(s, d)
    rope_c, rope_s1, rope_s2 = rope_tables(s)

    xn = rmsnorm_call(h, g_mix[0], BF16)
    proj = inproj_call(xn, w_in[0], rope_c, rope_s1, rope_s2, d)
    ya = window_attn_call(proj, attn_sink[0])
    yb = na_attn_call(proj, na_bias_tables(na_rpb[0]))
    merged = merge_call(ya, yb, w_branch_a[0], w_branch_b[0], proj, d)
    h = resmm_call(merged, w_out[0], h)

    xn2, top_e, gates = rms_router_call(h, g_ffn[0], w_router[0], b_router[0])
    pos, row_tok, block_e, n_used = moe_routing(top_e[:, :TOP_K], s)
    n_rows = row_tok.shape[0]
    row_g = jnp.zeros((n_rows,), F32).at[pos].set(gates[:, :TOP_K].reshape(-1))
    xs = dispatch_call(n_used, row_tok, xn2)
    ys = expert_call(block_e, n_used, xs, w_gate_up[0], b_gate_up[0], w_down[0], b_down[0], row_g)
    h, xn3 = combine_call(pos, ys, h, g_ple[0])

    h = ple_call(xn3, w_ple_gate[0], p[0].reshape(s, -1), w_ple[0], h)
    out = rmsnorm_call(h, g_final, F32)
    return out.reshape(b, s, d)
```

```python
import functools

import numpy as np
import jax
import jax.numpy as jnp
from jax import lax
from jax.experimental import pallas as pl
from jax.experimental.pallas import tpu as pltpu

F32 = jnp.float32
BF16 = jnp.bfloat16

HEAD_DIM = 64
A_HEADS = 32
A_KV_HEADS = 8
A_GROUP = A_HEADS // A_KV_HEADS
B_HEADS = 32
A_Q_W = A_HEADS * HEAD_DIM
A_KV_W = A_KV_HEADS * HEAD_DIM
B_W = B_HEADS * HEAD_DIM
ROT_DIM = HEAD_DIM // 4
ROPE_THETA = 500000.0
WINDOW = 128
GRID_W = 64
NA_ROWS = 8
NA_COLS = 16
N_EXPERTS = 32
TOP_K = 4
SWIGLU_LIMIT = 7.0
SWIGLU_ALPHA = 1.702
RMS_EPS = 1e-6
NEG_INF = -1e30

LANES = 128
VMEM_LIMIT = 56 * 1024 * 1024

RMS_ROWS = 256
MM_TM = 512
MM_TN = 512
ATT_BLK = 128
NA_RB = 8
MOE_TM = 512
MOE_TF = 256
MOE_TN = 2048
CMB_TB = 128


def _params(sem):
    return pltpu.CompilerParams(dimension_semantics=sem, vmem_limit_bytes=VMEM_LIMIT)


def _rms_body(x_ref, g_ref):
    x = x_ref[...].astype(F32)
    ms = jnp.mean(x * x, axis=-1, keepdims=True)
    return x * lax.rsqrt(ms + RMS_EPS) * g_ref[...]


def _rms_kernel(x_ref, g_ref, o_ref):
    o_ref[...] = _rms_body(x_ref, g_ref).astype(o_ref.dtype)


def rmsnorm_call(x, g, out_dtype):
    s, d = x.shape
    return pl.pallas_call(
        _rms_kernel,
        grid=(s // RMS_ROWS,),
        in_specs=[pl.BlockSpec((RMS_ROWS, d), lambda i: (i, 0)),
                  pl.BlockSpec((1, d), lambda i: (0, 0))],
        out_specs=pl.BlockSpec((RMS_ROWS, d), lambda i: (i, 0)),
        out_shape=jax.ShapeDtypeStruct((s, d), out_dtype),
        compiler_params=_params(("parallel",)),
        name="rmsnorm",
    )(x, g.reshape(1, d))


def _to_token_rows(x2d):
    return x2d.reshape(x2d.shape[0], x2d.shape[1] // LANES, LANES)


def _rms_router_kernel(x_ref, g_ref, wr_ref, br_ref, xn_ref, e_ref, gt_ref):
    xn = _rms_body(x_ref, g_ref)
    xn_ref[...] = _to_token_rows(xn.astype(xn_ref.dtype))
    logits = jnp.dot(xn, wr_ref[...], preferred_element_type=F32,
                     precision=lax.Precision.HIGHEST) + br_ref[...]
    lane = lax.broadcasted_iota(jnp.int32, logits.shape, 1)
    logits = jnp.where(lane < N_EXPERTS, logits, -jnp.inf)
    e_out = jnp.zeros(logits.shape, jnp.int32)
    v_out = jnp.zeros(logits.shape, F32)
    v0 = None
    for k in range(TOP_K):
        m = jnp.max(logits, axis=-1, keepdims=True)
        idx = jnp.min(jnp.where(logits == m, lane, LANES), axis=-1, keepdims=True)
        if k == 0:
            v0 = m
        e_out = jnp.where(lane == k, idx, e_out)
        v_out = jnp.where(lane == k, jnp.exp(m - v0), v_out)
        logits = jnp.where(lane == idx, -jnp.inf, logits)
    e_ref[...] = e_out
    gt_ref[...] = v_out / jnp.sum(v_out, axis=-1, keepdims=True)


def rms_router_call(h, g, w_router, b_router):
    s, d = h.shape
    wr = jnp.zeros((d, LANES), F32).at[:, :N_EXPERTS].set(w_router)
    br = jnp.zeros((1, LANES), F32).at[0, :N_EXPERTS].set(b_router)
    return pl.pallas_call(
        _rms_router_kernel,
        grid=(s // RMS_ROWS,),
        in_specs=[pl.BlockSpec((RMS_ROWS, d), lambda i: (i, 0)),
                  pl.BlockSpec((1, d), lambda i: (0, 0)),
                  pl.BlockSpec((d, LANES), lambda i: (0, 0)),
                  pl.BlockSpec((1, LANES), lambda i: (0, 0))],
        out_specs=[pl.BlockSpec((RMS_ROWS, d // LANES, LANES), lambda i: (i, 0, 0)),
                   pl.BlockSpec((RMS_ROWS, LANES), lambda i: (i, 0)),
                   pl.BlockSpec((RMS_ROWS, LANES), lambda i: (i, 0))],
        out_shape=[jax.ShapeDtypeStruct((s, d // LANES, LANES), BF16),
                   jax.ShapeDtypeStruct((s, LANES), jnp.int32),
                   jax.ShapeDtypeStruct((s, LANES), F32)],
        compiler_params=_params(("parallel",)),
        name="rms_router",
    )(h, g.reshape(1, d), wr, br)


def _rope(acc, c_ref, s1_ref, s2_ref):
    c, s1, s2 = c_ref[...], s1_ref[...], s2_ref[...]
    half = ROT_DIM // 2
    outs = []
    for t in range(acc.shape[1] // LANES):
        x = acc[:, t * LANES:(t + 1) * LANES]
        outs.append(x * c + pltpu.roll(x, LANES - half, 1) * s1 + pltpu.roll(x, half, 1) * s2)
    return jnp.concatenate(outs, axis=1)


def _inproj_kernel(a_ref, w_ref, c_ref, s1_ref, s2_ref, o_ref, wb_ref, *, n_rope, n_plain):
    j = pl.program_id(0)

    @pl.when(pl.program_id(1) == 0)
    def _():
        wb_ref[...] = w_ref[...].astype(BF16)

    acc = jnp.dot(a_ref[...], wb_ref[...], preferred_element_type=F32)

    @pl.when(j < n_rope)
    def _():
        o_ref[...] = _rope(acc, c_ref, s1_ref, s2_ref).astype(o_ref.dtype)

    @pl.when(jnp.logical_and(j >= n_rope, j < n_plain))
    def _():
        o_ref[...] = acc.astype(o_ref.dtype)

    @pl.when(j >= n_plain)
    def _():
        o_ref[...] = jax.nn.sigmoid(acc).astype(o_ref.dtype)


def inproj_call(xn, w_in, rope_c, rope_s1, rope_s2, d_model):
    s, d = xn.shape
    n = w_in.shape[1]
    n_rope = (A_Q_W + A_KV_W) // MM_TN
    n_plain = (n - 2 * d_model) // MM_TN
    tab = pl.BlockSpec((MM_TM, LANES), lambda j, i: (i, 0))
    return pl.pallas_call(
        functools.partial(_inproj_kernel, n_rope=n_rope, n_plain=n_plain),
        grid=(n // MM_TN, s // MM_TM),
        in_specs=[pl.BlockSpec((MM_TM, d), lambda j, i: (i, 0)),
                  pl.BlockSpec((d, MM_TN), lambda j, i: (0, j)),
                  tab, tab, tab],
        out_specs=pl.BlockSpec((MM_TM, MM_TN), lambda j, i: (i, j)),
        out_shape=jax.ShapeDtypeStruct((s, n), BF16),
        scratch_shapes=[pltpu.VMEM((d, MM_TN), BF16)],
        compiler_params=_params(("parallel", "arbitrary")),
        name="inproj",
    )(xn, w_in, rope_c, rope_s1, rope_s2)


def rope_tables(s):
    half = ROT_DIM // 2
    inv = jnp.float32(ROPE_THETA) ** (-jnp.arange(half, dtype=F32) * (2.0 / ROT_DIM))
    ang = jnp.arange(s, dtype=F32)[:, None] * inv[None, :]
    cos, sin = jnp.cos(ang), jnp.sin(ang)
    ones = jnp.ones((s, HEAD_DIM - ROT_DIM), F32)
    zeros = jnp.zeros((s, HEAD_DIM - ROT_DIM), F32)
    zh = jnp.zeros((s, half), F32)
    c = jnp.concatenate([cos, cos, ones], axis=1)
    s1 = jnp.concatenate([-sin, zh, zeros], axis=1)
    s2 = jnp.concatenate([zh, sin, zeros], axis=1)
    rep = LANES // HEAD_DIM
    return jnp.tile(c, (1, rep)), jnp.tile(s1, (1, rep)), jnp.tile(s2, (1, rep))


def _win_kernel(sink_ref, q_ref, kp_ref, kc_ref, kn_ref, vp_ref, vc_ref, vn_ref, o_ref, *, seq):
    n = pl.program_id(0)
    pr = pl.program_id(1)
    kcat = jnp.concatenate([kp_ref[...], kc_ref[...], kn_ref[...]], axis=0)
    vcat = jnp.concatenate([vp_ref[...], vc_ref[...], vn_ref[...]], axis=0)
    q = q_ref[...]
    rows = A_GROUP * ATT_BLK
    qpos = n * ATT_BLK + lax.broadcasted_iota(jnp.int32, (rows, 3 * ATT_BLK), 0) % ATT_BLK
    kpos = (n - 1) * ATT_BLK + lax.broadcasted_iota(jnp.int32, (rows, 3 * ATT_BLK), 1)
    mask = (jnp.abs(qpos - kpos) <= WINDOW) & (kpos >= 0) & (kpos < seq)
    grp = lax.broadcasted_iota(jnp.int32, (rows, 1), 0) // ATT_BLK
    kv_per_blk = LANES // HEAD_DIM
    for kvh in range(kv_per_blk):
        k_h = kcat[:, kvh * HEAD_DIM:(kvh + 1) * HEAD_DIM]
        v_h = vcat[:, kvh * HEAD_DIM:(kvh + 1) * HEAD_DIM]
        h0 = kvh * A_GROUP
        qs = jnp.concatenate(
            [q[:, (h0 + g) * HEAD_DIM:(h0 + g + 1) * HEAD_DIM] for g in range(A_GROUP)], axis=0)
        sc = lax.dot_general(qs, k_h, (((1,), (1,)), ((), ())),
                             preferred_element_type=F32) * (HEAD_DIM ** -0.5)
        sc = jnp.where(mask, sc, NEG_INF)
        snk = jnp.zeros((rows, 1), F32)
        for g in range(A_GROUP):
            sv = sink_ref[(pr * kv_per_blk + kvh) * A_GROUP + g]
            snk = jnp.where(grp == g, sv, snk)
        m = jnp.maximum(jnp.max(sc, axis=-1, keepdims=True), snk)
        p = jnp.exp(sc - m)
        denom = jnp.sum(p, axis=-1, keepdims=True) + jnp.exp(snk - m)
        attn = (p / denom).astype(BF16)
        out = jnp.dot(attn, v_h, preferred_element_type=F32)
        for g in range(A_GROUP):
            o_ref[:, (h0 + g) * HEAD_DIM:(h0 + g + 1) * HEAD_DIM] = (
                out[g * ATT_BLK:(g + 1) * ATT_BLK].astype(o_ref.dtype))


def window_attn_call(proj, sink):
    s = proj.shape[0]
    nb = s // ATT_BLK
    qw = A_GROUP * LANES
    k0 = A_Q_W // LANES
    v0 = (A_Q_W + A_KV_W) // LANES
    blk = (ATT_BLK, LANES)
    prev = lambda n: jnp.maximum(n - 1, 0)
    nxt = lambda n: jnp.minimum(n + 1, nb - 1)
    return pl.pallas_call(
        functools.partial(_win_kernel, seq=s),
        grid=(nb, A_Q_W // qw),
        in_specs=[pl.BlockSpec(memory_space=pltpu.SMEM),
                  pl.BlockSpec((ATT_BLK, qw), lambda n, p: (n, p)),
                  pl.BlockSpec(blk, lambda n, p: (prev(n), k0 + p)),
                  pl.BlockSpec(blk, lambda n, p: (n, k0 + p)),
                  pl.BlockSpec(blk, lambda n, p: (nxt(n), k0 + p)),
                  pl.BlockSpec(blk, lambda n, p: (prev(n), v0 + p)),
                  pl.BlockSpec(blk, lambda n, p: (n, v0 + p)),
                  pl.BlockSpec(blk, lambda n, p: (nxt(n), v0 + p))],
        out_specs=pl.BlockSpec((ATT_BLK, qw), lambda n, p: (n, p)),
        out_shape=jax.ShapeDtypeStruct((s, A_Q_W), BF16),
        compiler_params=_params(("parallel", "parallel")),
        name="window_attn",
    )(sink, proj, proj, proj, proj, proj, proj, proj)


def na_bias_tables(rpb):
    c = np.arange(GRID_W)
    kc = np.arange(GRID_W)
    dc = np.clip(kc[None, :] - c[:, None] + NA_COLS - 1, 0, 2 * NA_COLS - 2)
    qcs = np.clip(c - NA_COLS // 2, 0, GRID_W - NA_COLS)
    cmask = (kc[None, :] >= qcs[:, None]) & (kc[None, :] < qcs[:, None] + NA_COLS)
    t = jnp.where(jnp.asarray(cmask)[None, None], rpb[:, :, dc].astype(F32), NEG_INF)
    idx = np.arange(NA_ROWS)[:, None] + np.arange(NA_ROWS)[None, :]
    bt = t[:, idx]
    bt = jnp.transpose(bt, (1, 0, 3, 2, 4))
    return bt.reshape(NA_ROWS, rpb.shape[0], GRID_W, NA_ROWS * GRID_W)


def _na_kernel(q_ref, kp_ref, kc_ref, kn_ref, vp_ref, vc_ref, vn_ref, b_ref, o_ref,
               kbuf, vbuf, *, grid_rows):
    rb = pl.program_id(1)
    blk = NA_RB * GRID_W
    kbuf[0:blk] = kp_ref[...]
    kbuf[blk:2 * blk] = kc_ref[...]
    kbuf[2 * blk:3 * blk] = kn_ref[...]
    vbuf[0:blk] = vp_ref[...]
    vbuf[blk:2 * blk] = vc_ref[...]
    vbuf[2 * blk:3 * blk] = vn_ref[...]
    base_row = (rb - 1) * NA_RB
    nkeys = NA_ROWS * GRID_W
    lane = lax.broadcasted_iota(jnp.int32, (GRID_W, LANES), 1)
    first_head = lane < HEAD_DIM
    zero = jnp.zeros((GRID_W, LANES), BF16)
    for i in range(NA_RB):
        r = rb * NA_RB + i
        rs = jnp.clip(r - NA_ROWS // 2, 0, grid_rows - NA_ROWS)
        typ = rs - (r - NA_ROWS // 2) + (NA_ROWS // 2 - 1)
        start = pl.multiple_of((rs - base_row) * GRID_W, GRID_W)
        ks = kbuf[pl.ds(start, nkeys), :]
        vs = vbuf[pl.ds(start, nkeys), :]
        qi = q_ref[i * GRID_W:(i + 1) * GRID_W, :] * jnp.asarray(HEAD_DIM ** -0.5, BF16)
        qs = jnp.concatenate([jnp.where(first_head, qi, zero), jnp.where(first_head, zero, qi)], axis=0)
        sc = lax.dot_general(qs, ks, (((1,), (1,)), ((), ())), preferred_element_type=F32)
        sc = sc + b_ref[typ].reshape(2 * GRID_W, nkeys)
        m = jnp.max(sc, axis=-1, keepdims=True)
        p = jnp.exp(sc - m)
        attn = (p * (1.0 / jnp.sum(p, axis=-1, keepdims=True))).astype(BF16)
        out = jnp.dot(attn, vs, preferred_element_type=F32)
        o_ref[i * GRID_W:(i + 1) * GRID_W, :] = jnp.where(
            first_head, out[:GRID_W], out[GRID_W:]).astype(o_ref.dtype)


def na_attn_call(proj, bias_tab):
    s = proj.shape[0]
    grid_rows = s // GRID_W
    nrb = grid_rows // NA_RB
    blk_rows = NA_RB * GRID_W
    q0 = (A_Q_W + 2 * A_KV_W) // LANES
    k0 = q0 + B_W // LANES
    v0 = k0 + B_W // LANES
    blk = (blk_rows, LANES)
    hpb = LANES // HEAD_DIM
    prev = lambda r: jnp.maximum(r - 1, 0)
    cur = lambda r: r
    nxt = lambda r: jnp.minimum(r + 1, nrb - 1)
    return pl.pallas_call(
        functools.partial(_na_kernel, grid_rows=grid_rows),
        grid=(B_HEADS // hpb, nrb),
        in_specs=[pl.BlockSpec(blk, lambda h, r: (r, q0 + h)),
                  pl.BlockSpec(blk, lambda h, r: (prev(r), k0 + h)),
                  pl.BlockSpec(blk, lambda h, r: (cur(r), k0 + h)),
                  pl.BlockSpec(blk, lambda h, r: (nxt(r), k0 + h)),
                  pl.BlockSpec(blk, lambda h, r: (prev(r), v0 + h)),
                  pl.BlockSpec(blk, lambda h, r: (cur(r), v0 + h)),
                  pl.BlockSpec(blk, lambda h, r: (nxt(r), v0 + h)),
                  pl.BlockSpec((NA_ROWS, hpb, GRID_W, NA_ROWS * GRID_W), lambda h, r: (0, h, 0, 0))],
        out_specs=pl.BlockSpec(blk, lambda h, r: (r, h)),
        out_shape=jax.ShapeDtypeStruct((s, B_W), BF16),
        scratch_shapes=[pltpu.VMEM((3 * blk_rows, LANES), BF16),
                        pltpu.VMEM((3 * blk_rows, LANES), BF16)],
        compiler_params=_params(("parallel", "arbitrary")),
        name="na_attn",
    )(proj, proj, proj, proj, proj, proj, proj, bias_tab)


def _merge_kernel(ya_ref, yb_ref, wa_ref, wb_ref, ga_ref, gb_ref, o_ref, wab, wbb):
    @pl.when(pl.program_id(1) == 0)
    def _():
        wab[...] = wa_ref[...].astype(BF16)
        wbb[...] = wb_ref[...].astype(BF16)

    a = jnp.dot(ya_ref[...], wab[...], preferred_element_type=F32)
    b = jnp.dot(yb_ref[...], wbb[...], preferred_element_type=F32)
    o_ref[...] = (ga_ref[...].astype(F32) * a + gb_ref[...].astype(F32) * b).astype(o_ref.dtype)


def merge_call(ya, yb, w_a, w_b, proj, d_model):
    s = ya.shape[0]
    ga0 = (proj.shape[1] - 2 * d_model) // MM_TN
    gb0 = ga0 + d_model // MM_TN
    return pl.pallas_call(
        _merge_kernel,
        grid=(d_model // MM_TN, s // MM_TM),
        in_specs=[pl.BlockSpec((MM_TM, A_Q_W), lambda j, i: (i, 0)),
                  pl.BlockSpec((MM_TM, B_W), lambda j, i: (i, 0)),
                  pl.BlockSpec((A_Q_W, MM_TN), lambda j, i: (0, j)),
                  pl.BlockSpec((B_W, MM_TN), lambda j, i: (0, j)),
                  pl.BlockSpec((MM_TM, MM_TN), lambda j, i: (i, ga0 + j)),
                  pl.BlockSpec((MM_TM, MM_TN), lambda j, i: (i, gb0 + j))],
        out_specs=pl.BlockSpec((MM_TM, MM_TN), lambda j, i: (i, j)),
        out_shape=jax.ShapeDtypeStruct((s, d_model), BF16),
        scratch_shapes=[pltpu.VMEM((A_Q_W, MM_TN), BF16), pltpu.VMEM((B_W, MM_TN), BF16)],
        compiler_params=_params(("parallel", "arbitrary")),
        name="merge",
    )(ya, yb, w_a, w_b, proj, proj)


def _resmm_kernel(a_ref, w_ref, r_ref, o_ref, wb_ref):
    @pl.when(pl.program_id(1) == 0)
    def _():
        wb_ref[...] = w_ref[...].astype(BF16)

    o_ref[...] = r_ref[...] + jnp.dot(a_ref[...], wb_ref[...], preferred_element_type=F32)


def resmm_call(a, w, res):
    s, k = a.shape
    n = w.shape[1]
    return pl.pallas_call(
        _resmm_kernel,
        grid=(n // MM_TN, s // MM_TM),
        in_specs=[pl.BlockSpec((MM_TM, k), lambda j, i: (i, 0)),
                  pl.BlockSpec((k, MM_TN), lambda j, i: (0, j)),
                  pl.BlockSpec((MM_TM, MM_TN), lambda j, i: (i, j))],
        out_specs=pl.BlockSpec((MM_TM, MM_TN), lambda j, i: (i, j)),
        out_shape=jax.ShapeDtypeStruct((s, n), F32),
        scratch_shapes=[pltpu.VMEM((k, MM_TN), BF16)],
        compiler_params=_params(("parallel", "arbitrary")),
        name="resmm",
    )(a, w, res)


def _ple_kernel(a_ref, wg_ref, p_ref, wp_ref, h_ref, o_ref, wgb, wpb):
    @pl.when(pl.program_id(1) == 0)
    def _():
        wgb[...] = wg_ref[...].astype(BF16)
        wpb[...] = wp_ref[...].astype(BF16)

    gate = jax.nn.sigmoid(jnp.dot(a_ref[...], wgb[...], preferred_element_type=F32))
    emb = jnp.dot(p_ref[...].astype(BF16), wpb[...], preferred_element_type=F32)
    o_ref[...] = h_ref[...] + gate * emb


def ple_call(xn, w_gate, p, w_ple, h):
    s, d = xn.shape
    pd = p.shape[1]
    return pl.pallas_call(
        _ple_kernel,
        grid=(d // MM_TN, s // MM_TM),
        in_specs=[pl.BlockSpec((MM_TM, d), lambda j, i: (i, 0)),
                  pl.BlockSpec((d, MM_TN), lambda j, i: (0, j)),
                  pl.BlockSpec((MM_TM, pd), lambda j, i: (i, 0)),
                  pl.BlockSpec((pd, MM_TN), lambda j, i: (0, j)),
                  pl.BlockSpec((MM_TM, MM_TN), lambda j, i: (i, j))],
        out_specs=pl.BlockSpec((MM_TM, MM_TN), lambda j, i: (i, j)),
        out_shape=jax.ShapeDtypeStruct((s, d), F32),
        scratch_shapes=[pltpu.VMEM((d, MM_TN), BF16), pltpu.VMEM((pd, MM_TN), BF16)],
        compiler_params=_params(("parallel", "arbitrary")),
        name="ple",
    )(xn, w_gate, p, w_ple, h)


def _dispatch_kernel(nused_ref, tok_ref, tokn_ref, x_hbm, o_ref, buf, sem):
    m = pl.program_id(0)
    nused = nused_ref[0]
    slot = m % 2

    def row_copy(tok, sl, r):
        return pltpu.make_async_copy(x_hbm.at[tok], buf.at[sl, r], sem.at[sl])

    def issue(t_ref, sl):
        def body(r, c):
            row_copy(t_ref[0, 0, r], sl, r).start()
            return c

        lax.fori_loop(0, MOE_TM, body, 0)

    @pl.when(m == 0)
    def _():
        issue(tok_ref, 0)

    @pl.when(m + 1 < nused)
    def _():
        issue(tokn_ref, 1 - slot)

    @pl.when(m < nused)
    def _():
        def drain(r, c):
            row_copy(0, slot, r).wait()
            return c

        lax.fori_loop(0, MOE_TM, drain, 0)
        o_ref[...] = buf[slot].reshape(o_ref.shape)

    @pl.when(m >= nused)
    def _():
        o_ref[...] = jnp.zeros(o_ref.shape, o_ref.dtype)


def dispatch_call(n_used, row_tok, xn3d):
    s, dg, _ = xn3d.shape
    d = dg * LANES
    n_rows = row_tok.shape[0]
    nblk = n_rows // MOE_TM
    tok3 = row_tok.reshape(nblk, 1, MOE_TM)
    tok_spec = lambda f: pl.BlockSpec((1, 1, MOE_TM), f, memory_space=pltpu.SMEM)
    return pl.pallas_call(
        _dispatch_kernel,
        grid_spec=pltpu.PrefetchScalarGridSpec(
            num_scalar_prefetch=1,
            grid=(nblk,),
            in_specs=[tok_spec(lambda m, nu: (m, 0, 0)),
                      tok_spec(lambda m, nu: (jnp.minimum(m + 1, nblk - 1), 0, 0)),
                      pl.BlockSpec(memory_space=pl.ANY)],
            out_specs=pl.BlockSpec((MOE_TM, d), lambda m, nu: (m, 0)),
            scratch_shapes=[pltpu.VMEM((2, MOE_TM, dg, LANES), BF16), pltpu.SemaphoreType.DMA((2,))]),
        out_shape=jax.ShapeDtypeStruct((n_rows, d), BF16),
        compiler_params=_params(("arbitrary",)),
        name="dispatch",
    )(n_used, tok3, tok3, xn3d)


def _expert_changed(be_ref, m):
    return jnp.logical_or(m == 0, be_ref[m] != be_ref[jnp.maximum(m - 1, 0)])


def _gateup_kernel(be_ref, nused_ref, x_ref, wgu_ref, bgu_ref, o_ref, wb):
    m = pl.program_id(1)

    @pl.when(m < nused_ref[0])
    def _():
        @pl.when(_expert_changed(be_ref, m))
        def _():
            wb[...] = wgu_ref[...].astype(BF16)

        hgu = jnp.dot(x_ref[...], wb[...], preferred_element_type=F32) + bgu_ref[...]
        gate = jnp.minimum(hgu, SWIGLU_LIMIT)
        up = jnp.clip(hgu, -SWIGLU_LIMIT, SWIGLU_LIMIT)
        up = pltpu.roll(up, 2 * MOE_TF - 1, 1)
        act = gate * jax.nn.sigmoid(gate * SWIGLU_ALPHA) * (up + 1.0)
        rr = lax.broadcasted_iota(jnp.int32, (2 * MOE_TF, MOE_TF), 0)
        cc = lax.broadcasted_iota(jnp.int32, (2 * MOE_TF, MOE_TF), 1)
        sel = (rr == 2 * cc).astype(BF16)
        o_ref[...] = jnp.dot(act.astype(BF16), sel, preferred_element_type=F32).astype(o_ref.dtype)

    @pl.when(m >= nused_ref[0])
    def _():
        o_ref[...] = jnp.zeros(o_ref.shape, o_ref.dtype)


def gateup_call(block_e, n_used, xs, w_gu, b_gu):
    n_rows, dh = xs.shape
    e, d, f2 = w_gu.shape
    f = f2 // 2
    nblk = n_rows // MOE_TM

    def meff(m, nu):
        return jnp.minimum(m, nu[0] - 1)

    return pl.pallas_call(
        _gateup_kernel,
        grid_spec=pltpu.PrefetchScalarGridSpec(
            num_scalar_prefetch=2,
            grid=(f // MOE_TF, nblk),
            in_specs=[pl.BlockSpec((MOE_TM, dh), lambda j, m, be, nu: (meff(m, nu), 0)),
                      pl.BlockSpec((None, d, 2 * MOE_TF), lambda j, m, be, nu: (be[m], 0, j)),
                      pl.BlockSpec((None, 1, 2 * MOE_TF), lambda j, m, be, nu: (be[m], 0, j))],
            out_specs=pl.BlockSpec((MOE_TM, MOE_TF), lambda j, m, be, nu: (m, j)),
            scratch_shapes=[pltpu.VMEM((d, 2 * MOE_TF), BF16)]),
        out_shape=jax.ShapeDtypeStruct((n_rows, f), BF16),
        compiler_params=_params(("arbitrary", "arbitrary")),
        name="expert_gateup",
    )(block_e, n_used, xs, w_gu, b_gu.reshape(e, 1, f2))


def _down_kernel(be_ref, nused_ref, a_ref, wdn_ref, bdn_ref, o_ref, wb):
    m = pl.program_id(1)

    @pl.when(m < nused_ref[0])
    def _():
        @pl.when(_expert_changed(be_ref, m))
        def _():
            wb[...] = wdn_ref[...].astype(BF16)

        out = jnp.dot(a_ref[...], wb[...], preferred_element_type=F32) + bdn_ref[...]
        o_ref[...] = _to_token_rows(out.astype(o_ref.dtype))

    @pl.when(m >= nused_ref[0])
    def _():
        o_ref[...] = jnp.zeros(o_ref.shape, o_ref.dtype)


def down_call(block_e, n_used, act, w_dn, b_dn):
    n_rows, f = act.shape
    e, _, d = w_dn.shape
    nblk = n_rows // MOE_TM

    def meff(m, nu):
        return jnp.minimum(m, nu[0] - 1)

    return pl.pallas_call(
        _down_kernel,
        grid_spec=pltpu.PrefetchScalarGridSpec(
            num_scalar_prefetch=2,
            grid=(d // MOE_TN, nblk),
            in_specs=[pl.BlockSpec((MOE_TM, f), lambda j, m, be, nu: (meff(m, nu), 0)),
                      pl.BlockSpec((None, f, MOE_TN), lambda j, m, be, nu: (be[m], 0, j)),
                      pl.BlockSpec((None, 1, MOE_TN), lambda j, m, be, nu: (be[m], 0, j))],
            out_specs=pl.BlockSpec((MOE_TM, MOE_TN // LANES, LANES), lambda j, m, be, nu: (m, j, 0)),
            scratch_shapes=[pltpu.VMEM((f, MOE_TN), BF16)]),
        out_shape=jax.ShapeDtypeStruct((n_rows, d // LANES, LANES), BF16),
        compiler_params=_params(("arbitrary", "arbitrary")),
        name="expert_down",
    )(block_e, n_used, act, w_dn, b_dn.reshape(e, 1, d))


def _combine_kernel(pos_ref, posn_ref, ys_hbm, h_ref, gt_ref, g_ref, h_out, xn_out, buf, sem):
    i = pl.program_id(0)
    n = pl.num_programs(0)
    slot = i % 2

    def row_copy(row, sl, k, t):
        return pltpu.make_async_copy(ys_hbm.at[row], buf.at[sl, k, t], sem.at[sl])

    def issue(p_ref, sl):
        def body(t, c):
            for k in range(TOP_K):
                row_copy(p_ref[0, 0, t * TOP_K + k], sl, k, t).start()
            return c

        lax.fori_loop(0, CMB_TB, body, 0)

    @pl.when(i == 0)
    def _():
        issue(pos_ref, 0)

    @pl.when(i + 1 < n)
    def _():
        issue(posn_ref, 1 - slot)

    def drain(t, c):
        for k in range(TOP_K):
            row_copy(0, slot, k, t).wait()
        return c

    lax.fori_loop(0, CMB_TB, drain, 0)

    h = h_ref[...]
    for k in range(TOP_K):
        rows = buf[slot, k].reshape(h.shape).astype(F32)
        h = h + gt_ref[:, k:k + 1] * rows
    h_out[...] = h
    ms = jnp.mean(h * h, axis=-1, keepdims=True)
    xn_out[...] = (h * lax.rsqrt(ms + RMS_EPS) * g_ref[...]).astype(xn_out.dtype)


def combine_call(pos, ys, h, gates, g):
    s, d = h.shape
    nb = s // CMB_TB
    pos3 = pos.reshape(nb, 1, CMB_TB * TOP_K)
    pos_spec = lambda f: pl.BlockSpec((1, 1, CMB_TB * TOP_K), f, memory_space=pltpu.SMEM)
    return pl.pallas_call(
        _combine_kernel,
        grid=(nb,),
        in_specs=[pos_spec(lambda i: (i, 0, 0)),
                  pos_spec(lambda i: (jnp.minimum(i + 1, nb - 1), 0, 0)),
                  pl.BlockSpec(memory_space=pl.ANY),
                  pl.BlockSpec((CMB_TB, d), lambda i: (i, 0)),
                  pl.BlockSpec((CMB_TB, LANES), lambda i: (i, 0)),
                  pl.BlockSpec((1, d), lambda i: (0, 0))],
        out_specs=[pl.BlockSpec((CMB_TB, d), lambda i: (i, 0)),
                   pl.BlockSpec((CMB_TB, d), lambda i: (i, 0))],
        out_shape=[jax.ShapeDtypeStruct((s, d), F32), jax.ShapeDtypeStruct((s, d), BF16)],
        scratch_shapes=[pltpu.VMEM((2, TOP_K, CMB_TB, d // LANES, LANES), BF16),
                        pltpu.SemaphoreType.DMA((2,))],
        compiler_params=_params(("arbitrary",)),
        name="combine",
    )(pos3, pos3, ys, h, gates, g.reshape(1, d))


def moe_routing(top_e, n_tokens):
    e_flat = top_e.reshape(-1)
    onehot = (e_flat[:, None] == jnp.arange(N_EXPERTS, dtype=jnp.int32)[None, :]).astype(jnp.int32)
    csum = jnp.cumsum(onehot, axis=0)
    counts = csum[-1]
    rank = jnp.sum(csum * onehot, axis=1) - 1
    padded = (counts + MOE_TM - 1) // MOE_TM * MOE_TM
    pends = jnp.cumsum(padded).astype(jnp.int32)
    pstarts = pends - padded
    pos = pstarts[e_flat] + rank
    n_blocks = (n_tokens * TOP_K) // MOE_TM + N_EXPERTS
    tok_flat = jnp.arange(n_tokens * TOP_K, dtype=jnp.int32) // TOP_K
    row_tok = jnp.zeros((n_blocks * MOE_TM,), jnp.int32).at[pos].set(tok_flat)
    n_used = pends[-1] // MOE_TM
    blk_start = jnp.arange(n_blocks, dtype=jnp.int32) * MOE_TM
    block_e = jnp.sum((pends[None, :] <= blk_start[:, None]).astype(jnp.int32), axis=1)
    block_e = jnp.minimum(block_e, N_EXPERTS - 1)
    last_e = block_e[jnp.maximum(n_used - 1, 0)]
    block_e = jnp.where(jnp.arange(n_blocks) < n_used, block_e, last_e).astype(jnp.int32)
    return pos.astype(jnp.int32), row_tok, block_e, n_used.reshape(1).astype(jnp.int32)


def kernel(x, p, g_mix, w_in, attn_sink, na_rpb, w_branch_a, w_branch_b, w_out, g_ffn, w_router,
           b_router, w_gate_up, b_gate_up, w_down, b_down, g_ple, w_ple_gate, w_ple, g_final):
    b, s, d = x.shape
    assert b == 1 and w_in.shape[0] == 1
    h = x.reshape(s, d)
    rope_c, rope_s1, rope_s2 = rope_tables(s)

    xn = rmsnorm_call(h, g_mix[0], BF16)
    proj = inproj_call(xn, w_in[0], rope_c, rope_s1, rope_s2, d)
    ya = window_attn_call(proj, attn_sink[0])
    yb = na_attn_call(proj, na_bias_tables(na_rpb[0]))
    merged = merge_call(ya, yb, w_branch_a[0], w_branch_b[0], proj, d)
    h = resmm_call(merged, w_out[0], h)

    xn2, top_e, gates = rms_router_call(h, g_ffn[0], w_router[0], b_router[0])
    pos, row_tok, block_e, n_used = moe_routing(top_e[:, :TOP_K], s)
    xs = dispatch_call(n_used, row_tok, xn2)
    act = gateup_call(block_e, n_used, xs, w_gate_up[0], b_gate_up[0])
    ys = down_call(block_e, n_used, act, w_down[0], b_down[0])
    h, xn3 = combine_call(pos, ys, h, gates, g_ple[0])

    h = ple_call(xn3, w_ple_gate[0], p[0].reshape(s, -1), w_ple[0], h)
    out = rmsnorm_call(h, g_final, F32)
    return out.reshape(b, s, d)
```

```python
import functools

import numpy as np
import jax
import jax.numpy as jnp
from jax import lax
from jax.experimental import pallas as pl
from jax.experimental.pallas import tpu as pltpu

F32 = jnp.float32
BF16 = jnp.bfloat16

HEAD_DIM = 64
A_HEADS = 32
A_KV_HEADS = 8
A_GROUP = A_HEADS // A_KV_HEADS
B_HEADS = 32
A_Q_W = A_HEADS * HEAD_DIM
A_KV_W = A_KV_HEADS * HEAD_DIM
B_W = B_HEADS * HEAD_DIM
ROT_DIM = HEAD_DIM // 4
ROPE_THETA = 500000.0
WINDOW = 128
GRID_W = 64
NA_ROWS = 8
NA_COLS = 16
N_EXPERTS = 32
TOP_K = 4
SWIGLU_LIMIT = 7.0
SWIGLU_ALPHA = 1.702
RMS_EPS = 1e-6
NEG_INF = -1e30

LANES = 128
VMEM_LIMIT = 56 * 1024 * 1024

RMS_ROWS = 256
MM_TM = 1024
MM_TN = 512
ATT_BLK = 128
NA_RB = 8
NA_HPS = 4
MOE_TM = 512
MOE_TF = 256
MOE_TN = 2048
CMB_TB = 128


def _params(sem):
    return pltpu.CompilerParams(dimension_semantics=sem, vmem_limit_bytes=VMEM_LIMIT)


def _rms_body(x_ref, g_ref):
    x = x_ref[...].astype(F32)
    ms = jnp.mean(x * x, axis=-1, keepdims=True)
    return x * lax.rsqrt(ms + RMS_EPS) * g_ref[...]


def _rms_kernel(x_ref, g_ref, o_ref):
    o_ref[...] = _rms_body(x_ref, g_ref).astype(o_ref.dtype)


def rmsnorm_call(x, g, out_dtype):
    s, d = x.shape
    return pl.pallas_call(
        _rms_kernel,
        grid=(s // RMS_ROWS,),
        in_specs=[pl.BlockSpec((RMS_ROWS, d), lambda i: (i, 0)),
                  pl.BlockSpec((1, d), lambda i: (0, 0))],
        out_specs=pl.BlockSpec((RMS_ROWS, d), lambda i: (i, 0)),
        out_shape=jax.ShapeDtypeStruct((s, d), out_dtype),
        compiler_params=_params(("parallel",)),
        name="rmsnorm",
    )(x, g.reshape(1, d))


def _to_token_rows(x2d):
    return x2d.reshape(x2d.shape[0], x2d.shape[1] // LANES, LANES)


def _rms_router_kernel(x_ref, g_ref, wr_ref, br_ref, xn_ref, e_ref, gt_ref):
    xn = _rms_body(x_ref, g_ref)
    xn_ref[...] = _to_token_rows(xn.astype(xn_ref.dtype))
    logits = jnp.dot(xn, wr_ref[...], preferred_element_type=F32,
                     precision=lax.Precision.HIGHEST) + br_ref[...]
    lane = lax.broadcasted_iota(jnp.int32, logits.shape, 1)
    logits = jnp.where(lane < N_EXPERTS, logits, -jnp.inf)
    e_out = jnp.zeros(logits.shape, jnp.int32)
    v_out = jnp.zeros(logits.shape, F32)
    v0 = None
    for k in range(TOP_K):
        m = jnp.max(logits, axis=-1, keepdims=True)
        idx = jnp.min(jnp.where(logits == m, lane, LANES), axis=-1, keepdims=True)
        if k == 0:
            v0 = m
        e_out = jnp.where(lane == k, idx, e_out)
        v_out = jnp.where(lane == k, jnp.exp(m - v0), v_out)
        logits = jnp.where(lane == idx, -jnp.inf, logits)
    e_ref[...] = e_out
    gt_ref[...] = v_out / jnp.sum(v_out, axis=-1, keepdims=True)


def rms_router_call(h, g, w_router, b_router):
    s, d = h.shape
    wr = jnp.zeros((d, LANES), F32).at[:, :N_EXPERTS].set(w_router)
    br = jnp.zeros((1, LANES), F32).at[0, :N_EXPERTS].set(b_router)
    return pl.pallas_call(
        _rms_router_kernel,
        grid=(s // RMS_ROWS,),
        in_specs=[pl.BlockSpec((RMS_ROWS, d), lambda i: (i, 0)),
                  pl.BlockSpec((1, d), lambda i: (0, 0)),
                  pl.BlockSpec((d, LANES), lambda i: (0, 0)),
                  pl.BlockSpec((1, LANES), lambda i: (0, 0))],
        out_specs=[pl.BlockSpec((RMS_ROWS, d // LANES, LANES), lambda i: (i, 0, 0)),
                   pl.BlockSpec((RMS_ROWS, LANES), lambda i: (i, 0)),
                   pl.BlockSpec((RMS_ROWS, LANES), lambda i: (i, 0))],
        out_shape=[jax.ShapeDtypeStruct((s, d // LANES, LANES), BF16),
                   jax.ShapeDtypeStruct((s, LANES), jnp.int32),
                   jax.ShapeDtypeStruct((s, LANES), F32)],
        compiler_params=_params(("parallel",)),
        name="rms_router",
    )(h, g.reshape(1, d), wr, br)


def _rope(acc, c_ref, s1_ref, s2_ref):
    c, s1, s2 = c_ref[...], s1_ref[...], s2_ref[...]
    half = ROT_DIM // 2
    outs = []
    for t in range(acc.shape[1] // LANES):
        x = acc[:, t * LANES:(t + 1) * LANES]
        outs.append(x * c + pltpu.roll(x, LANES - half, 1) * s1 + pltpu.roll(x, half, 1) * s2)
    return jnp.concatenate(outs, axis=1)


def _inproj_kernel(a_ref, w_ref, c_ref, s1_ref, s2_ref, o_ref, wb_ref, *, n_rope, n_plain):
    j = pl.program_id(0)

    @pl.when(pl.program_id(1) == 0)
    def _():
        wb_ref[...] = w_ref[...].astype(BF16)

    acc = jnp.dot(a_ref[...], wb_ref[...], preferred_element_type=F32)

    @pl.when(j < n_rope)
    def _():
        o_ref[...] = _rope(acc, c_ref, s1_ref, s2_ref).astype(o_ref.dtype)

    @pl.when(jnp.logical_and(j >= n_rope, j < n_plain))
    def _():
        o_ref[...] = acc.astype(o_ref.dtype)

    @pl.when(j >= n_plain)
    def _():
        o_ref[...] = jax.nn.sigmoid(acc).astype(o_ref.dtype)


def inproj_call(xn, w_in, rope_c, rope_s1, rope_s2, d_model):
    s, d = xn.shape
    n = w_in.shape[1]
    n_rope = (A_Q_W + A_KV_W) // MM_TN
    n_plain = (n - 2 * d_model) // MM_TN
    tab = pl.BlockSpec((MM_TM, LANES), lambda j, i: (i, 0))
    return pl.pallas_call(
        functools.partial(_inproj_kernel, n_rope=n_rope, n_plain=n_plain),
        grid=(n // MM_TN, s // MM_TM),
        in_specs=[pl.BlockSpec((MM_TM, d), lambda j, i: (i, 0)),
                  pl.BlockSpec((d, MM_TN), lambda j, i: (0, j)),
                  tab, tab, tab],
        out_specs=pl.BlockSpec((MM_TM, MM_TN), lambda j, i: (i, j)),
        out_shape=jax.ShapeDtypeStruct((s, n), BF16),
        scratch_shapes=[pltpu.VMEM((d, MM_TN), BF16)],
        compiler_params=_params(("parallel", "arbitrary")),
        name="inproj",
    )(xn, w_in, rope_c, rope_s1, rope_s2)


def rope_tables(s):
    half = ROT_DIM // 2
    inv = jnp.float32(ROPE_THETA) ** (-jnp.arange(half, dtype=F32) * (2.0 / ROT_DIM))
    ang = jnp.arange(s, dtype=F32)[:, None] * inv[None, :]
    cos, sin = jnp.cos(ang), jnp.sin(ang)
    ones = jnp.ones((s, HEAD_DIM - ROT_DIM), F32)
    zeros = jnp.zeros((s, HEAD_DIM - ROT_DIM), F32)
    zh = jnp.zeros((s, half), F32)
    c = jnp.concatenate([cos, cos, ones], axis=1)
    s1 = jnp.concatenate([-sin, zh, zeros], axis=1)
    s2 = jnp.concatenate([zh, sin, zeros], axis=1)
    rep = LANES // HEAD_DIM
    return jnp.tile(c, (1, rep)), jnp.tile(s1, (1, rep)), jnp.tile(s2, (1, rep))


def _win_kernel(sink_ref, q_ref, kp_ref, kc_ref, kn_ref, vp_ref, vc_ref, vn_ref, o_ref, *, seq):
    n = pl.program_id(0)
    pr = pl.program_id(1)
    kcat = jnp.concatenate([kp_ref[...], kc_ref[...], kn_ref[...]], axis=0)
    vcat = jnp.concatenate([vp_ref[...], vc_ref[...], vn_ref[...]], axis=0)
    q = q_ref[...]
    rows = A_GROUP * ATT_BLK
    qpos = n * ATT_BLK + lax.broadcasted_iota(jnp.int32, (rows, 3 * ATT_BLK), 0) % ATT_BLK
    kpos = (n - 1) * ATT_BLK + lax.broadcasted_iota(jnp.int32, (rows, 3 * ATT_BLK), 1)
    mask = (jnp.abs(qpos - kpos) <= WINDOW) & (kpos >= 0) & (kpos < seq)
    grp = lax.broadcasted_iota(jnp.int32, (rows, 1), 0) // ATT_BLK
    kv_per_blk = LANES // HEAD_DIM
    for kvh in range(kv_per_blk):
        k_h = kcat[:, kvh * HEAD_DIM:(kvh + 1) * HEAD_DIM]
        v_h = vcat[:, kvh * HEAD_DIM:(kvh + 1) * HEAD_DIM]
        h0 = kvh * A_GROUP
        qs = jnp.concatenate(
            [q[:, (h0 + g) * HEAD_DIM:(h0 + g + 1) * HEAD_DIM] for g in range(A_GROUP)], axis=0)
        sc = lax.dot_general(qs, k_h, (((1,), (1,)), ((), ())),
                             preferred_element_type=F32) * (HEAD_DIM ** -0.5)
        sc = jnp.where(mask, sc, NEG_INF)
        snk = jnp.zeros((rows, 1), F32)
        for g in range(A_GROUP):
            sv = sink_ref[(pr * kv_per_blk + kvh) * A_GROUP + g]
            snk = jnp.where(grp == g, sv, snk)
        m = jnp.maximum(jnp.max(sc, axis=-1, keepdims=True), snk)
        p = jnp.exp(sc - m)
        denom = jnp.sum(p, axis=-1, keepdims=True) + jnp.exp(snk - m)
        attn = (p / denom).astype(BF16)
        out = jnp.dot(attn, v_h, preferred_element_type=F32)
        for g in range(A_GROUP):
            o_ref[:, (h0 + g) * HEAD_DIM:(h0 + g + 1) * HEAD_DIM] = (
                out[g * ATT_BLK:(g + 1) * ATT_BLK].astype(o_ref.dtype))


def window_attn_call(proj, sink):
    s = proj.shape[0]
    nb = s // ATT_BLK
    qw = A_GROUP * LANES
    k0 = A_Q_W // LANES
    v0 = (A_Q_W + A_KV_W) // LANES
    blk = (ATT_BLK, LANES)
    prev = lambda n: jnp.maximum(n - 1, 0)
    nxt = lambda n: jnp.minimum(n + 1, nb - 1)
    return pl.pallas_call(
        functools.partial(_win_kernel, seq=s),
        grid=(nb, A_Q_W // qw),
        in_specs=[pl.BlockSpec(memory_space=pltpu.SMEM),
                  pl.BlockSpec((ATT_BLK, qw), lambda n, p: (n, p)),
                  pl.BlockSpec(blk, lambda n, p: (prev(n), k0 + p)),
                  pl.BlockSpec(blk, lambda n, p: (n, k0 + p)),
                  pl.BlockSpec(blk, lambda n, p: (nxt(n), k0 + p)),
                  pl.BlockSpec(blk, lambda n, p: (prev(n), v0 + p)),
                  pl.BlockSpec(blk, lambda n, p: (n, v0 + p)),
                  pl.BlockSpec(blk, lambda n, p: (nxt(n), v0 + p))],
        out_specs=pl.BlockSpec((ATT_BLK, qw), lambda n, p: (n, p)),
        out_shape=jax.ShapeDtypeStruct((s, A_Q_W), BF16),
        compiler_params=_params(("parallel", "parallel")),
        name="window_attn",
    )(sink, proj, proj, proj, proj, proj, proj, proj)


def na_bias_tables(rpb):
    c = np.arange(GRID_W)
    kc = np.arange(GRID_W)
    dc = np.clip(kc[None, :] - c[:, None] + NA_COLS - 1, 0, 2 * NA_COLS - 2)
    qcs = np.clip(c - NA_COLS // 2, 0, GRID_W - NA_COLS)
    cmask = (kc[None, :] >= qcs[:, None]) & (kc[None, :] < qcs[:, None] + NA_COLS)
    t = jnp.where(jnp.asarray(cmask)[None, None], rpb[:, :, dc].astype(F32), NEG_INF)
    idx = np.arange(NA_ROWS)[:, None] + np.arange(NA_ROWS)[None, :]
    bt = t[:, idx]
    bt = jnp.transpose(bt, (1, 0, 3, 2, 4))
    return bt.reshape(NA_ROWS, rpb.shape[0], GRID_W, NA_ROWS * GRID_W)


def _na_kernel(q_ref, kp_ref, kc_ref, kn_ref, vp_ref, vc_ref, vn_ref, b_ref, o_ref,
               kbuf, vbuf, s_scr, p_scr, *, grid_rows):
    rb = pl.program_id(1)
    blk = NA_RB * GRID_W
    kbuf[0:blk] = kp_ref[...]
    kbuf[blk:2 * blk] = kc_ref[...]
    kbuf[2 * blk:3 * blk] = kn_ref[...]
    vbuf[0:blk] = vp_ref[...]
    vbuf[blk:2 * blk] = vc_ref[...]
    vbuf[2 * blk:3 * blk] = vn_ref[...]
    nkeys = NA_ROWS * GRID_W
    width = NA_HPS * HEAD_DIM
    grp = NA_HPS * GRID_W
    row_head = lax.broadcasted_iota(jnp.int32, (grp, width), 0) // GRID_W
    lane_head = lax.broadcasted_iota(jnp.int32, (grp, width), 1) // HEAD_DIM
    own = row_head == lane_head
    out_head = lax.broadcasted_iota(jnp.int32, (GRID_W, width), 1) // HEAD_DIM
    nrb = grid_rows // NA_RB
    half = NA_ROWS // 2

    def rows(first_rb):
        starts = []
        for i in range(NA_RB):
            r = first_rb * NA_RB + i
            rs = min(max(r - half, 0), grid_rows - NA_ROWS)
            typ = rs - (r - half) + (half - 1)
            start = (rs - (first_rb - 1) * NA_RB) * GRID_W
            starts.append(start)
            qi = q_ref[i * GRID_W:(i + 1) * GRID_W, :] * jnp.asarray(HEAD_DIM ** -0.5, BF16)
            qs = jnp.where(own, jnp.concatenate([qi] * NA_HPS, axis=0), jnp.zeros((grp, width), BF16))
            sc = lax.dot_general(qs, kbuf[start:start + nkeys, :], (((1,), (1,)), ((), ())),
                                 preferred_element_type=F32)
            s_scr[i * grp:(i + 1) * grp, :] = sc + b_ref[typ].reshape(grp, nkeys)
        for i in range(NA_RB):
            sc = s_scr[i * grp:(i + 1) * grp, :]
            m = jnp.max(sc, axis=-1, keepdims=True)
            p = jnp.exp(sc - m)
            p_scr[i * grp:(i + 1) * grp, :] = (
                p * (1.0 / jnp.sum(p, axis=-1, keepdims=True))).astype(BF16)
        for i in range(NA_RB):
            start = starts[i]
            out = jnp.dot(p_scr[i * grp:(i + 1) * grp, :], vbuf[start:start + nkeys, :],
                          preferred_element_type=F32)
            res = out[:GRID_W]
            for h in range(1, NA_HPS):
                res = jnp.where(out_head == h, out[h * GRID_W:(h + 1) * GRID_W], res)
            o_ref[i * GRID_W:(i + 1) * GRID_W, :] = res.astype(o_ref.dtype)

    @pl.when(rb == 0)
    def _():
        rows(0)

    @pl.when(rb == nrb - 1)
    def _():
        rows(nrb - 1)

    @pl.when(jnp.logical_and(rb > 0, rb < nrb - 1))
    def _():
        rows(1)


def na_attn_call(proj, bias_tab):
    s = proj.shape[0]
    grid_rows = s // GRID_W
    nrb = grid_rows // NA_RB
    blk_rows = NA_RB * GRID_W
    width = NA_HPS * HEAD_DIM
    q0 = (A_Q_W + 2 * A_KV_W) // width
    k0 = q0 + B_W // width
    v0 = k0 + B_W // width
    blk = (blk_rows, width)
    hpb = NA_HPS
    prev = lambda r: jnp.maximum(r - 1, 0)
    cur = lambda r: r
    nxt = lambda r: jnp.minimum(r + 1, nrb - 1)
    return pl.pallas_call(
        functools.partial(_na_kernel, grid_rows=grid_rows),
        grid=(B_HEADS // hpb, nrb),
        in_specs=[pl.BlockSpec(blk, lambda h, r: (r, q0 + h)),
                  pl.BlockSpec(blk, lambda h, r: (prev(r), k0 + h)),
                  pl.BlockSpec(blk, lambda h, r: (cur(r), k0 + h)),
                  pl.BlockSpec(blk, lambda h, r: (nxt(r), k0 + h)),
                  pl.BlockSpec(blk, lambda h, r: (prev(r), v0 + h)),
                  pl.BlockSpec(blk, lambda h, r: (cur(r), v0 + h)),
                  pl.BlockSpec(blk, lambda h, r: (nxt(r), v0 + h)),
                  pl.BlockSpec((NA_ROWS, hpb, GRID_W, NA_ROWS * GRID_W), lambda h, r: (0, h, 0, 0))],
        out_specs=pl.BlockSpec(blk, lambda h, r: (r, h)),
        out_shape=jax.ShapeDtypeStruct((s, B_W), BF16),
        scratch_shapes=[pltpu.VMEM((3 * blk_rows, width), BF16),
                        pltpu.VMEM((3 * blk_rows, width), BF16),
                        pltpu.VMEM((NA_RB * hpb * GRID_W, NA_ROWS * GRID_W), F32),
                        pltpu.VMEM((NA_RB * hpb * GRID_W, NA_ROWS * GRID_W), BF16)],
        compiler_params=_params(("parallel", "arbitrary")),
        name="na_attn",
    )(proj, proj, proj, proj, proj, proj, proj, bias_tab)


def _merge_kernel(ya_ref, yb_ref, wa_ref, wb_ref, ga_ref, gb_ref, o_ref, wab, wbb):
    @pl.when(pl.program_id(1) == 0)
    def _():
        wab[...] = wa_ref[...].astype(BF16)
        wbb[...] = wb_ref[...].astype(BF16)

    a = jnp.dot(ya_ref[...], wab[...], preferred_element_type=F32)
    b = jnp.dot(yb_ref[...], wbb[...], preferred_element_type=F32)
    o_ref[...] = (ga_ref[...].astype(F32) * a + gb_ref[...].astype(F32) * b).astype(o_ref.dtype)


def merge_call(ya, yb, w_a, w_b, proj, d_model):
    s = ya.shape[0]
    ga0 = (proj.shape[1] - 2 * d_model) // MM_TN
    gb0 = ga0 + d_model // MM_TN
    return pl.pallas_call(
        _merge_kernel,
        grid=(d_model // MM_TN, s // MM_TM),
        in_specs=[pl.BlockSpec((MM_TM, A_Q_W), lambda j, i: (i, 0)),
                  pl.BlockSpec((MM_TM, B_W), lambda j, i: (i, 0)),
                  pl.BlockSpec((A_Q_W, MM_TN), lambda j, i: (0, j)),
                  pl.BlockSpec((B_W, MM_TN), lambda j, i: (0, j)),
                  pl.BlockSpec((MM_TM, MM_TN), lambda j, i: (i, ga0 + j)),
                  pl.BlockSpec((MM_TM, MM_TN), lambda j, i: (i, gb0 + j))],
        out_specs=pl.BlockSpec((MM_TM, MM_TN), lambda j, i: (i, j)),
        out_shape=jax.ShapeDtypeStruct((s, d_model), BF16),
        scratch_shapes=[pltpu.VMEM((A_Q_W, MM_TN), BF16), pltpu.VMEM((B_W, MM_TN), BF16)],
        compiler_params=_params(("parallel", "arbitrary")),
        name="merge",
    )(ya, yb, w_a, w_b, proj, proj)


def _resmm_kernel(a_ref, w_ref, r_ref, o_ref, wb_ref):
    @pl.when(pl.program_id(1) == 0)
    def _():
        wb_ref[...] = w_ref[...].astype(BF16)

    o_ref[...] = r_ref[...] + jnp.dot(a_ref[...], wb_ref[...], preferred_element_type=F32)


def resmm_call(a, w, res):
    s, k = a.shape
    n = w.shape[1]
    return pl.pallas_call(
        _resmm_kernel,
        grid=(n // MM_TN, s // MM_TM),
        in_specs=[pl.BlockSpec((MM_TM, k), lambda j, i: (i, 0)),
                  pl.BlockSpec((k, MM_TN), lambda j, i: (0, j)),
                  pl.BlockSpec((MM_TM, MM_TN), lambda j, i: (i, j))],
        out_specs=pl.BlockSpec((MM_TM, MM_TN), lambda j, i: (i, j)),
        out_shape=jax.ShapeDtypeStruct((s, n), F32),
        scratch_shapes=[pltpu.VMEM((k, MM_TN), BF16)],
        compiler_params=_params(("parallel", "arbitrary")),
        name="resmm",
    )(a, w, res)


def _ple_kernel(a_ref, wg_ref, p_ref, wp_ref, h_ref, o_ref, wgb, wpb):
    @pl.when(pl.program_id(1) == 0)
    def _():
        wgb[...] = wg_ref[...].astype(BF16)
        wpb[...] = wp_ref[...].astype(BF16)

    gate = jax.nn.sigmoid(jnp.dot(a_ref[...], wgb[...], preferred_element_type=F32))
    emb = jnp.dot(p_ref[...].astype(BF16), wpb[...], preferred_element_type=F32)
    o_ref[...] = h_ref[...] + gate * emb


def ple_call(xn, w_gate, p, w_ple, h):
    s, d = xn.shape
    pd = p.shape[1]
    return pl.pallas_call(
        _ple_kernel,
        grid=(d // MM_TN, s // MM_TM),
        in_specs=[pl.BlockSpec((MM_TM, d), lambda j, i: (i, 0)),
                  pl.BlockSpec((d, MM_TN), lambda j, i: (0, j)),
                  pl.BlockSpec((MM_TM, pd), lambda j, i: (i, 0)),
                  pl.BlockSpec((pd, MM_TN), lambda j, i: (0, j)),
                  pl.BlockSpec((MM_TM, MM_TN), lambda j, i: (i, j))],
        out_specs=pl.BlockSpec((MM_TM, MM_TN), lambda j, i: (i, j)),
        out_shape=jax.ShapeDtypeStruct((s, d), F32),
        scratch_shapes=[pltpu.VMEM((d, MM_TN), BF16), pltpu.VMEM((pd, MM_TN), BF16)],
        compiler_params=_params(("parallel", "arbitrary")),
        name="ple",
    )(xn, w_gate, p, w_ple, h)


def _dispatch_kernel(nused_ref, tok_ref, tokn_ref, x_hbm, o_ref, buf, sem):
    m = pl.program_id(0)
    nused = nused_ref[0]
    slot = m % 2

    def row_copy(tok, sl, r):
        return pltpu.make_async_copy(x_hbm.at[tok], buf.at[sl, r], sem.at[sl])

    def issue(t_ref, sl):
        def body(r, c):
            row_copy(t_ref[0, 0, r], sl, r).start()
            return c

        lax.fori_loop(0, MOE_TM, body, 0)

    @pl.when(m == 0)
    def _():
        issue(tok_ref, 0)

    @pl.when(m + 1 < nused)
    def _():
        issue(tokn_ref, 1 - slot)

    @pl.when(m < nused)
    def _():
        pltpu.make_async_copy(x_hbm.at[pl.ds(0, MOE_TM)], buf.at[slot], sem.at[slot]).wait()
        o_ref[...] = buf[slot].reshape(o_ref.shape)

    @pl.when(m >= nused)
    def _():
        o_ref[...] = jnp.zeros(o_ref.shape, o_ref.dtype)


def dispatch_call(n_used, row_tok, xn3d):
    s, dg, _ = xn3d.shape
    d = dg * LANES
    n_rows = row_tok.shape[0]
    nblk = n_rows // MOE_TM
    tok3 = row_tok.reshape(nblk, 1, MOE_TM)
    tok_spec = lambda f: pl.BlockSpec((1, 1, MOE_TM), f, memory_space=pltpu.SMEM)
    return pl.pallas_call(
        _dispatch_kernel,
        grid_spec=pltpu.PrefetchScalarGridSpec(
            num_scalar_prefetch=1,
            grid=(nblk,),
            in_specs=[tok_spec(lambda m, nu: (m, 0, 0)),
                      tok_spec(lambda m, nu: (jnp.minimum(m + 1, nblk - 1), 0, 0)),
                      pl.BlockSpec(memory_space=pl.ANY)],
            out_specs=pl.BlockSpec((MOE_TM, d), lambda m, nu: (m, 0)),
            scratch_shapes=[pltpu.VMEM((2, MOE_TM, dg, LANES), BF16), pltpu.SemaphoreType.DMA((2,))]),
        out_shape=jax.ShapeDtypeStruct((n_rows, d), BF16),
        compiler_params=_params(("arbitrary",)),
        name="dispatch",
    )(n_used, tok3, tok3, xn3d)


def _expert_changed(be_ref, m):
    return jnp.logical_or(m == 0, be_ref[m] != be_ref[jnp.maximum(m - 1, 0)])


def _gateup_kernel(be_ref, nused_ref, x_ref, wgu_ref, bgu_ref, o_ref, wb):
    m = pl.program_id(1)

    @pl.when(m < nused_ref[0])
    def _():
        @pl.when(_expert_changed(be_ref, m))
        def _():
            wb[...] = wgu_ref[...].astype(BF16)

        hgu = jnp.dot(x_ref[...], wb[...], preferred_element_type=F32) + bgu_ref[...]
        gate = jnp.minimum(hgu, SWIGLU_LIMIT)
        up = jnp.clip(hgu, -SWIGLU_LIMIT, SWIGLU_LIMIT)
        up = pltpu.roll(up, 2 * MOE_TF - 1, 1)
        act = gate * jax.nn.sigmoid(gate * SWIGLU_ALPHA) * (up + 1.0)
        rr = lax.broadcasted_iota(jnp.int32, (2 * MOE_TF, MOE_TF), 0)
        cc = lax.broadcasted_iota(jnp.int32, (2 * MOE_TF, MOE_TF), 1)
        sel = (rr == 2 * cc).astype(BF16)
        o_ref[...] = jnp.dot(act.astype(BF16), sel, preferred_element_type=F32).astype(o_ref.dtype)

    @pl.when(m >= nused_ref[0])
    def _():
        o_ref[...] = jnp.zeros(o_ref.shape, o_ref.dtype)


def gateup_call(block_e, n_used, xs, w_gu, b_gu):
    n_rows, dh = xs.shape
    e, d, f2 = w_gu.shape
    f = f2 // 2
    nblk = n_rows // MOE_TM

    def meff(m, nu):
        return jnp.minimum(m, nu[0] - 1)

    return pl.pallas_call(
        _gateup_kernel,
        grid_spec=pltpu.PrefetchScalarGridSpec(
            num_scalar_prefetch=2,
            grid=(f // MOE_TF, nblk),
            in_specs=[pl.BlockSpec((MOE_TM, dh), lambda j, m, be, nu: (meff(m, nu), 0)),
                      pl.BlockSpec((None, d, 2 * MOE_TF), lambda j, m, be, nu: (be[m], 0, j)),
                      pl.BlockSpec((None, 1, 2 * MOE_TF), lambda j, m, be, nu: (be[m], 0, j))],
            out_specs=pl.BlockSpec((MOE_TM, MOE_TF), lambda j, m, be, nu: (m, j)),
            scratch_shapes=[pltpu.VMEM((d, 2 * MOE_TF), BF16)]),
        out_shape=jax.ShapeDtypeStruct((n_rows, f), BF16),
        compiler_params=_params(("arbitrary", "arbitrary")),
        name="expert_gateup",
    )(block_e, n_used, xs, w_gu, b_gu.reshape(e, 1, f2))


def _down_kernel(be_ref, nused_ref, a_ref, wdn_ref, bdn_ref, o_ref, wb):
    m = pl.program_id(1)

    @pl.when(m < nused_ref[0])
    def _():
        @pl.when(_expert_changed(be_ref, m))
        def _():
            wb[...] = wdn_ref[...].astype(BF16)

        out = jnp.dot(a_ref[...], wb[...], preferred_element_type=F32) + bdn_ref[...]
        o_ref[...] = _to_token_rows(out.astype(o_ref.dtype))

    @pl.when(m >= nused_ref[0])
    def _():
        o_ref[...] = jnp.zeros(o_ref.shape, o_ref.dtype)


def down_call(block_e, n_used, act, w_dn, b_dn):
    n_rows, f = act.shape
    e, _, d = w_dn.shape
    nblk = n_rows // MOE_TM

    def meff(m, nu):
        return jnp.minimum(m, nu[0] - 1)

    return pl.pallas_call(
        _down_kernel,
        grid_spec=pltpu.PrefetchScalarGridSpec(
            num_scalar_prefetch=2,
            grid=(d // MOE_TN, nblk),
            in_specs=[pl.BlockSpec((MOE_TM, f), lambda j, m, be, nu: (meff(m, nu), 0)),
                      pl.BlockSpec((None, f, MOE_TN), lambda j, m, be, nu: (be[m], 0, j)),
                      pl.BlockSpec((None, 1, MOE_TN), lambda j, m, be, nu: (be[m], 0, j))],
            out_specs=pl.BlockSpec((MOE_TM, MOE_TN // LANES, LANES), lambda j, m, be, nu: (m, j, 0)),
            scratch_shapes=[pltpu.VMEM((f, MOE_TN), BF16)]),
        out_shape=jax.ShapeDtypeStruct((n_rows, d // LANES, LANES), BF16),
        compiler_params=_params(("arbitrary", "arbitrary")),
        name="expert_down",
    )(block_e, n_used, act, w_dn, b_dn.reshape(e, 1, d))


def _combine_kernel(pos_ref, posn_ref, ys_hbm, h_ref, gt_ref, g_ref, h_out, xn_out, buf, sem):
    i = pl.program_id(0)
    n = pl.num_programs(0)
    slot = i % 2

    def row_copy(row, sl, k, t):
        return pltpu.make_async_copy(ys_hbm.at[row], buf.at[sl, k, t], sem.at[sl])

    def issue(p_ref, sl):
        def body(t, c):
            for k in range(TOP_K):
                row_copy(p_ref[0, 0, t * TOP_K + k], sl, k, t).start()
            return c

        lax.fori_loop(0, CMB_TB, body, 0)

    @pl.when(i == 0)
    def _():
        issue(pos_ref, 0)

    @pl.when(i + 1 < n)
    def _():
        issue(posn_ref, 1 - slot)

    for k in range(TOP_K):
        pltpu.make_async_copy(ys_hbm.at[pl.ds(0, CMB_TB)], buf.at[slot, k], sem.at[slot]).wait()

    h = h_ref[...]
    for k in range(TOP_K):
        rows = buf[slot, k].reshape(h.shape).astype(F32)
        h = h + gt_ref[:, k:k + 1] * rows
    h_out[...] = h
    ms = jnp.mean(h * h, axis=-1, keepdims=True)
    xn_out[...] = (h * lax.rsqrt(ms + RMS_EPS) * g_ref[...]).astype(xn_out.dtype)


def combine_call(pos, ys, h, gates, g):
    s, d = h.shape
    nb = s // CMB_TB
    pos3 = pos.reshape(nb, 1, CMB_TB * TOP_K)
    pos_spec = lambda f: pl.BlockSpec((1, 1, CMB_TB * TOP_K), f, memory_space=pltpu.SMEM)
    return pl.pallas_call(
        _combine_kernel,
        grid=(nb,),
        in_specs=[pos_spec(lambda i: (i, 0, 0)),
                  pos_spec(lambda i: (jnp.minimum(i + 1, nb - 1), 0, 0)),
                  pl.BlockSpec(memory_space=pl.ANY),
                  pl.BlockSpec((CMB_TB, d), lambda i: (i, 0)),
                  pl.BlockSpec((CMB_TB, LANES), lambda i: (i, 0)),
                  pl.BlockSpec((1, d), lambda i: (0, 0))],
        out_specs=[pl.BlockSpec((CMB_TB, d), lambda i: (i, 0)),
                   pl.BlockSpec((CMB_TB, d), lambda i: (i, 0))],
        out_shape=[jax.ShapeDtypeStruct((s, d), F32), jax.ShapeDtypeStruct((s, d), BF16)],
        scratch_shapes=[pltpu.VMEM((2, TOP_K, CMB_TB, d // LANES, LANES), BF16),
                        pltpu.SemaphoreType.DMA((2,))],
        compiler_params=_params(("arbitrary",)),
        name="combine",
    )(pos3, pos3, ys, h, gates, g.reshape(1, d))


def moe_routing(top_e, n_tokens):
    e_flat = top_e.reshape(-1)
    onehot = (e_flat[:, None] == jnp.arange(N_EXPERTS, dtype=jnp.int32)[None, :]).astype(jnp.int32)
    csum = jnp.cumsum(onehot, axis=0)
    counts = csum[-1]
    rank = jnp.sum(csum * onehot, axis=1) - 1
    padded = (counts + MOE_TM - 1) // MOE_TM * MOE_TM
    pends = jnp.cumsum(padded).astype(jnp.int32)
    pstarts = pends - padded
    pos = pstarts[e_flat] + rank
    n_blocks = (n_tokens * TOP_K) // MOE_TM + N_EXPERTS
    tok_flat = jnp.arange(n_tokens * TOP_K, dtype=jnp.int32) // TOP_K
    row_tok = jnp.zeros((n_blocks * MOE_TM,), jnp.int32).at[pos].set(tok_flat)
    n_used = pends[-1] // MOE_TM
    blk_start = jnp.arange(n_blocks, dtype=jnp.int32) * MOE_TM
    block_e = jnp.sum((pends[None, :] <= blk_start[:, None]).astype(jnp.int32), axis=1)
    block_e = jnp.minimum(block_e, N_EXPERTS - 1)
    last_e = block_e[jnp.maximum(n_used - 1, 0)]
    block_e = jnp.where(jnp.arange(n_blocks) < n_used, block_e, last_e).astype(jnp.int32)
    return pos.astype(jnp.int32), row_tok, block_e, n_used.reshape(1).astype(jnp.int32)


def kernel(x, p, g_mix, w_in, attn_sink, na_rpb, w_branch_a, w_branch_b, w_out, g_ffn, w_router,
           b_router, w_gate_up, b_gate_up, w_down, b_down, g_ple, w_ple_gate, w_ple, g_final):
    b, s, d = x.shape
    assert b == 1 and w_in.shape[0] == 1
    h = x.reshape(s, d)
    rope_c, rope_s1, rope_s2 = rope_tables(s)

    xn = rmsnorm_call(h, g_mix[0], BF16)
    proj = inproj_call(xn, w_in[0], rope_c, rope_s1, rope_s2, d)
    ya = window_attn_call(proj, attn_sink[0])
    yb = na_attn_call(proj, na_bias_tables(na_rpb[0]))
    merged = merge_call(ya, yb, w_branch_a[0], w_branch_b[0], proj, d)
    h = resmm_call(merged, w_out[0], h)

    xn2, top_e, gates = rms_router_call(h, g_ffn[0], w_router[0], b_router[0])
    pos, row_tok, block_e, n_used = moe_routing(top_e[:, :TOP_K], s)
    xs = dispatch_call(n_used, row_tok, xn2)
    act = gateup_call(block_e, n_used, xs, w_gate_up[0], b_gate_up[0])
    ys = down_call(block_e, n_used, act, w_down[0], b_down[0])
    h, xn3 = combine_call(pos, ys, h, gates, g_ple[0])

    h = ple_call(xn3, w_ple_gate[0], p[0].reshape(s, -1), w_ple[0], h)
    out = rmsnorm_call(h, g_final, F32)
    return out.reshape(b, s, d)
```

```python
import functools

import numpy as np
import jax
import jax.numpy as jnp
from jax import lax
from jax.experimental import pallas as pl
from jax.experimental.pallas import tpu as pltpu

F32 = jnp.float32
BF16 = jnp.bfloat16

HEAD_DIM = 64
A_HEADS = 32
A_KV_HEADS = 8
A_GROUP = A_HEADS // A_KV_HEADS
B_HEADS = 32
A_Q_W = A_HEADS * HEAD_DIM
A_KV_W = A_KV_HEADS * HEAD_DIM
B_W = B_HEADS * HEAD_DIM
ROT_DIM = HEAD_DIM // 4
ROPE_THETA = 500000.0
WINDOW = 128
GRID_W = 64
NA_ROWS = 8
NA_COLS = 16
N_EXPERTS = 32
TOP_K = 4
SWIGLU_LIMIT = 7.0
SWIGLU_ALPHA = 1.702
RMS_EPS = 1e-6
NEG_INF = -1e30

LANES = 128
VMEM_LIMIT = 56 * 1024 * 1024

RMS_ROWS = 256
MM_TM = 1024
MM_TN = 512
ATT_BLK = 128
NA_RB = 8
NA_HPS = 4
MOE_TM = 512
MOE_TF = 256
MOE_TN = 2048
CMB_TB = 128


def _params(sem):
    return pltpu.CompilerParams(dimension_semantics=sem, vmem_limit_bytes=VMEM_LIMIT)


def _rms_body(x_ref, g_ref):
    x = x_ref[...].astype(F32)
    ms = jnp.mean(x * x, axis=-1, keepdims=True)
    return x * lax.rsqrt(ms + RMS_EPS) * g_ref[...]


def _rms_kernel(x_ref, g_ref, o_ref):
    o_ref[...] = _rms_body(x_ref, g_ref).astype(o_ref.dtype)


def rmsnorm_call(x, g, out_dtype):
    s, d = x.shape
    return pl.pallas_call(
        _rms_kernel,
        grid=(s // RMS_ROWS,),
        in_specs=[pl.BlockSpec((RMS_ROWS, d), lambda i: (i, 0)),
                  pl.BlockSpec((1, d), lambda i: (0, 0))],
        out_specs=pl.BlockSpec((RMS_ROWS, d), lambda i: (i, 0)),
        out_shape=jax.ShapeDtypeStruct((s, d), out_dtype),
        compiler_params=_params(("parallel",)),
        name="rmsnorm",
    )(x, g.reshape(1, d))


def _to_token_rows(x2d):
    return x2d.reshape(x2d.shape[0], x2d.shape[1] // LANES, LANES)


def _rms_router_kernel(x_ref, g_ref, wr_ref, br_ref, xn_ref, e_ref, gt_ref):
    xn = _rms_body(x_ref, g_ref)
    xn_ref[...] = _to_token_rows(xn.astype(xn_ref.dtype))
    logits = jnp.dot(xn, wr_ref[...], preferred_element_type=F32,
                     precision=lax.Precision.HIGHEST) + br_ref[...]
    lane = lax.broadcasted_iota(jnp.int32, logits.shape, 1)
    logits = jnp.where(lane < N_EXPERTS, logits, -jnp.inf)
    e_out = jnp.zeros(logits.shape, jnp.int32)
    v_out = jnp.zeros(logits.shape, F32)
    v0 = None
    for k in range(TOP_K):
        m = jnp.max(logits, axis=-1, keepdims=True)
        idx = jnp.min(jnp.where(logits == m, lane, LANES), axis=-1, keepdims=True)
        if k == 0:
            v0 = m
        e_out = jnp.where(lane == k, idx, e_out)
        v_out = jnp.where(lane == k, jnp.exp(m - v0), v_out)
        logits = jnp.where(lane == idx, -jnp.inf, logits)
    e_ref[...] = e_out
    gt_ref[...] = v_out / jnp.sum(v_out, axis=-1, keepdims=True)


def rms_router_call(h, g, w_router, b_router):
    s, d = h.shape
    wr = jnp.zeros((d, LANES), F32).at[:, :N_EXPERTS].set(w_router)
    br = jnp.zeros((1, LANES), F32).at[0, :N_EXPERTS].set(b_router)
    return pl.pallas_call(
        _rms_router_kernel,
        grid=(s // RMS_ROWS,),
        in_specs=[pl.BlockSpec((RMS_ROWS, d), lambda i: (i, 0)),
                  pl.BlockSpec((1, d), lambda i: (0, 0)),
                  pl.BlockSpec((d, LANES), lambda i: (0, 0)),
                  pl.BlockSpec((1, LANES), lambda i: (0, 0))],
        out_specs=[pl.BlockSpec((RMS_ROWS, d // LANES, LANES), lambda i: (i, 0, 0)),
                   pl.BlockSpec((RMS_ROWS, LANES), lambda i: (i, 0)),
                   pl.BlockSpec((RMS_ROWS, LANES), lambda i: (i, 0))],
        out_shape=[jax.ShapeDtypeStruct((s, d // LANES, LANES), BF16),
                   jax.ShapeDtypeStruct((s, LANES), jnp.int32),
                   jax.ShapeDtypeStruct((s, LANES), F32)],
        compiler_params=_params(("parallel",)),
        name="rms_router",
    )(h, g.reshape(1, d), wr, br)


def _rope(acc, c_ref, s1_ref, s2_ref):
    c, s1, s2 = c_ref[...], s1_ref[...], s2_ref[...]
    half = ROT_DIM // 2
    outs = []
    for t in range(acc.shape[1] // LANES):
        x = acc[:, t * LANES:(t + 1) * LANES]
        outs.append(x * c + pltpu.roll(x, LANES - half, 1) * s1 + pltpu.roll(x, half, 1) * s2)
    return jnp.concatenate(outs, axis=1)


def _inproj_kernel(a_ref, w_ref, c_ref, s1_ref, s2_ref, o_ref, wb_ref, *, n_rope, n_plain):
    j = pl.program_id(0)

    @pl.when(pl.program_id(1) == 0)
    def _():
        wb_ref[...] = w_ref[...].astype(BF16)

    acc = jnp.dot(a_ref[...], wb_ref[...], preferred_element_type=F32)

    @pl.when(j < n_rope)
    def _():
        o_ref[...] = _rope(acc, c_ref, s1_ref, s2_ref).astype(o_ref.dtype)

    @pl.when(jnp.logical_and(j >= n_rope, j < n_plain))
    def _():
        o_ref[...] = acc.astype(o_ref.dtype)

    @pl.when(j >= n_plain)
    def _():
        o_ref[...] = jax.nn.sigmoid(acc).astype(o_ref.dtype)


def inproj_call(xn, w_in, rope_c, rope_s1, rope_s2, d_model):
    s, d = xn.shape
    n = w_in.shape[1]
    n_rope = (A_Q_W + A_KV_W) // MM_TN
    n_plain = (n - 2 * d_model) // MM_TN
    tab = pl.BlockSpec((MM_TM, LANES), lambda j, i: (i, 0))
    return pl.pallas_call(
        functools.partial(_inproj_kernel, n_rope=n_rope, n_plain=n_plain),
        grid=(n // MM_TN, s // MM_TM),
        in_specs=[pl.BlockSpec((MM_TM, d), lambda j, i: (i, 0)),
                  pl.BlockSpec((d, MM_TN), lambda j, i: (0, j)),
                  tab, tab, tab],
        out_specs=pl.BlockSpec((MM_TM, MM_TN), lambda j, i: (i, j)),
        out_shape=jax.ShapeDtypeStruct((s, n), BF16),
        scratch_shapes=[pltpu.VMEM((d, MM_TN), BF16)],
        compiler_params=_params(("parallel", "arbitrary")),
        name="inproj",
    )(xn, w_in, rope_c, rope_s1, rope_s2)


def rope_tables(s):
    half = ROT_DIM // 2
    inv = jnp.float32(ROPE_THETA) ** (-jnp.arange(half, dtype=F32) * (2.0 / ROT_DIM))
    ang = jnp.arange(s, dtype=F32)[:, None] * inv[None, :]
    cos, sin = jnp.cos(ang), jnp.sin(ang)
    ones = jnp.ones((s, HEAD_DIM - ROT_DIM), F32)
    zeros = jnp.zeros((s, HEAD_DIM - ROT_DIM), F32)
    zh = jnp.zeros((s, half), F32)
    c = jnp.concatenate([cos, cos, ones], axis=1)
    s1 = jnp.concatenate([-sin, zh, zeros], axis=1)
    s2 = jnp.concatenate([zh, sin, zeros], axis=1)
    rep = LANES // HEAD_DIM
    return jnp.tile(c, (1, rep)), jnp.tile(s1, (1, rep)), jnp.tile(s2, (1, rep))


def _win_kernel(sink_ref, q_ref, kp_ref, kc_ref, kn_ref, vp_ref, vc_ref, vn_ref, o_ref, *, seq):
    n = pl.program_id(0)
    pr = pl.program_id(1)
    kcat = jnp.concatenate([kp_ref[...], kc_ref[...], kn_ref[...]], axis=0)
    vcat = jnp.concatenate([vp_ref[...], vc_ref[...], vn_ref[...]], axis=0)
    q = q_ref[...] * jnp.asarray(HEAD_DIM ** -0.5, BF16)
    rows = A_GROUP * ATT_BLK
    nk = 3 * ATT_BLK
    qpos = n * ATT_BLK + lax.broadcasted_iota(jnp.int32, (ATT_BLK, nk), 0)
    kpos = (n - 1) * ATT_BLK + lax.broadcasted_iota(jnp.int32, (ATT_BLK, nk), 1)
    mask = ((jnp.abs(qpos - kpos) <= WINDOW) & (kpos >= 0) & (kpos < seq))[None]
    grp = lax.broadcasted_iota(jnp.int32, (A_GROUP, 1, 1), 0)
    kv_per_blk = LANES // HEAD_DIM
    for kvh in range(kv_per_blk):
        k_h = kcat[:, kvh * HEAD_DIM:(kvh + 1) * HEAD_DIM]
        v_h = vcat[:, kvh * HEAD_DIM:(kvh + 1) * HEAD_DIM]
        h0 = kvh * A_GROUP
        qs = jnp.concatenate(
            [q[:, (h0 + g) * HEAD_DIM:(h0 + g + 1) * HEAD_DIM] for g in range(A_GROUP)], axis=0)
        sc = lax.dot_general(qs, k_h, (((1,), (1,)), ((), ())), preferred_element_type=F32)
        sc = jnp.where(mask, sc.reshape(A_GROUP, ATT_BLK, nk), NEG_INF)
        snk = jnp.zeros((A_GROUP, 1, 1), F32)
        for g in range(A_GROUP):
            sv = sink_ref[(pr * kv_per_blk + kvh) * A_GROUP + g]
            snk = jnp.where(grp == g, sv, snk)
        m = jnp.maximum(jnp.max(sc, axis=-1, keepdims=True), snk)
        p = jnp.exp(sc - m)
        denom = jnp.sum(p, axis=-1, keepdims=True) + jnp.exp(snk - m)
        attn = (p * (1.0 / denom)).astype(BF16).reshape(rows, nk)
        out = jnp.dot(attn, v_h, preferred_element_type=F32)
        for g in range(A_GROUP):
            o_ref[:, (h0 + g) * HEAD_DIM:(h0 + g + 1) * HEAD_DIM] = (
                out[g * ATT_BLK:(g + 1) * ATT_BLK].astype(o_ref.dtype))


def window_attn_call(proj, sink):
    s = proj.shape[0]
    nb = s // ATT_BLK
    qw = A_GROUP * LANES
    k0 = A_Q_W // LANES
    v0 = (A_Q_W + A_KV_W) // LANES
    blk = (ATT_BLK, LANES)
    prev = lambda n: jnp.maximum(n - 1, 0)
    nxt = lambda n: jnp.minimum(n + 1, nb - 1)
    return pl.pallas_call(
        functools.partial(_win_kernel, seq=s),
        grid=(nb, A_Q_W // qw),
        in_specs=[pl.BlockSpec(memory_space=pltpu.SMEM),
                  pl.BlockSpec((ATT_BLK, qw), lambda n, p: (n, p)),
                  pl.BlockSpec(blk, lambda n, p: (prev(n), k0 + p)),
                  pl.BlockSpec(blk, lambda n, p: (n, k0 + p)),
                  pl.BlockSpec(blk, lambda n, p: (nxt(n), k0 + p)),
                  pl.BlockSpec(blk, lambda n, p: (prev(n), v0 + p)),
                  pl.BlockSpec(blk, lambda n, p: (n, v0 + p)),
                  pl.BlockSpec(blk, lambda n, p: (nxt(n), v0 + p))],
        out_specs=pl.BlockSpec((ATT_BLK, qw), lambda n, p: (n, p)),
        out_shape=jax.ShapeDtypeStruct((s, A_Q_W), BF16),
        compiler_params=_params(("parallel", "parallel")),
        name="window_attn",
    )(sink, proj, proj, proj, proj, proj, proj, proj)


def na_bias_tables(rpb):
    c = np.arange(GRID_W)
    kc = np.arange(GRID_W)
    dc = np.clip(kc[None, :] - c[:, None] + NA_COLS - 1, 0, 2 * NA_COLS - 2)
    qcs = np.clip(c - NA_COLS // 2, 0, GRID_W - NA_COLS)
    cmask = (kc[None, :] >= qcs[:, None]) & (kc[None, :] < qcs[:, None] + NA_COLS)
    t = jnp.where(jnp.asarray(cmask)[None, None], rpb[:, :, dc].astype(F32), NEG_INF)
    t = jnp.transpose(t, (0, 2, 1, 3)).reshape(rpb.shape[0], GRID_W, (2 * NA_ROWS - 1) * GRID_W)
    return jnp.pad(t, ((0, 0), (0, 0), (GRID_W, 0)))


def _na_kernel(q_ref, kp_ref, kc_ref, kn_ref, vp_ref, vc_ref, vn_ref, b_ref, o_ref,
               kbuf, vbuf, s_scr, p_scr, *, grid_rows):
    rb = pl.program_id(1)
    blk = NA_RB * GRID_W
    kbuf[0:blk] = kp_ref[...]
    kbuf[blk:2 * blk] = kc_ref[...]
    kbuf[2 * blk:3 * blk] = kn_ref[...]
    vbuf[0:blk] = vp_ref[...]
    vbuf[blk:2 * blk] = vc_ref[...]
    vbuf[2 * blk:3 * blk] = vn_ref[...]
    nkeys = NA_ROWS * GRID_W
    width = NA_HPS * HEAD_DIM
    grp = NA_HPS * GRID_W
    row_head = lax.broadcasted_iota(jnp.int32, (grp, width), 0) // GRID_W
    lane_head = lax.broadcasted_iota(jnp.int32, (grp, width), 1) // HEAD_DIM
    own = row_head == lane_head
    out_head = lax.broadcasted_iota(jnp.int32, (GRID_W, width), 1) // HEAD_DIM
    nrb = grid_rows // NA_RB
    half = NA_ROWS // 2

    def rows(first_rb):
        starts = []
        for i in range(NA_RB):
            r = first_rb * NA_RB + i
            rs = min(max(r - half, 0), grid_rows - NA_ROWS)
            typ = rs - (r - half) + (half - 1)
            start = (rs - (first_rb - 1) * NA_RB) * GRID_W
            starts.append(start)
            qi = q_ref[i * GRID_W:(i + 1) * GRID_W, :] * jnp.asarray(HEAD_DIM ** -0.5, BF16)
            qs = jnp.where(own, jnp.concatenate([qi] * NA_HPS, axis=0), jnp.zeros((grp, width), BF16))
            sc = lax.dot_general(qs, kbuf[start:start + nkeys, :], (((1,), (1,)), ((), ())),
                                 preferred_element_type=F32)
            boff = (typ + 1) * GRID_W
            s_scr[i * grp:(i + 1) * grp, :] = sc + b_ref[:, :, boff:boff + nkeys].reshape(grp, nkeys)
        for i in range(NA_RB):
            sc = s_scr[i * grp:(i + 1) * grp, :]
            m = jnp.max(sc, axis=-1, keepdims=True)
            p = jnp.exp(sc - m)
            p_scr[i * grp:(i + 1) * grp, :] = (
                p * (1.0 / jnp.sum(p, axis=-1, keepdims=True))).astype(BF16)
        for i in range(NA_RB):
            start = starts[i]
            out = jnp.dot(p_scr[i * grp:(i + 1) * grp, :], vbuf[start:start + nkeys, :],
                          preferred_element_type=F32)
            res = out[:GRID_W]
            for h in range(1, NA_HPS):
                res = jnp.where(out_head == h, out[h * GRID_W:(h + 1) * GRID_W], res)
            o_ref[i * GRID_W:(i + 1) * GRID_W, :] = res.astype(o_ref.dtype)

    @pl.when(rb == 0)
    def _():
        rows(0)

    @pl.when(rb == nrb - 1)
    def _():
        rows(nrb - 1)

    @pl.when(jnp.logical_and(rb > 0, rb < nrb - 1))
    def _():
        rows(1)


def na_attn_call(proj, bias_tab):
    s = proj.shape[0]
    grid_rows = s // GRID_W
    nrb = grid_rows // NA_RB
    blk_rows = NA_RB * GRID_W
    width = NA_HPS * HEAD_DIM
    q0 = (A_Q_W + 2 * A_KV_W) // width
    k0 = q0 + B_W // width
    v0 = k0 + B_W // width
    blk = (blk_rows, width)
    hpb = NA_HPS
    prev = lambda r: jnp.maximum(r - 1, 0)
    cur = lambda r: r
    nxt = lambda r: jnp.minimum(r + 1, nrb - 1)
    return pl.pallas_call(
        functools.partial(_na_kernel, grid_rows=grid_rows),
        grid=(B_HEADS // hpb, nrb),
        in_specs=[pl.BlockSpec(blk, lambda h, r: (r, q0 + h)),
                  pl.BlockSpec(blk, lambda h, r: (prev(r), k0 + h)),
                  pl.BlockSpec(blk, lambda h, r: (cur(r), k0 + h)),
                  pl.BlockSpec(blk, lambda h, r: (nxt(r), k0 + h)),
                  pl.BlockSpec(blk, lambda h, r: (prev(r), v0 + h)),
                  pl.BlockSpec(blk, lambda h, r: (cur(r), v0 + h)),
                  pl.BlockSpec(blk, lambda h, r: (nxt(r), v0 + h)),
                  pl.BlockSpec((hpb, GRID_W, 2 * NA_ROWS * GRID_W), lambda h, r: (h, 0, 0))],
        out_specs=pl.BlockSpec(blk, lambda h, r: (r, h)),
        out_shape=jax.ShapeDtypeStruct((s, B_W), BF16),
        scratch_shapes=[pltpu.VMEM((3 * blk_rows, width), BF16),
                        pltpu.VMEM((3 * blk_rows, width), BF16),
                        pltpu.VMEM((NA_RB * hpb * GRID_W, NA_ROWS * GRID_W), F32),
                        pltpu.VMEM((NA_RB * hpb * GRID_W, NA_ROWS * GRID_W), BF16)],
        compiler_params=_params(("parallel", "arbitrary")),
        name="na_attn",
    )(proj, proj, proj, proj, proj, proj, proj, bias_tab)


def _merge_kernel(ya_ref, yb_ref, wa_ref, wb_ref, ga_ref, gb_ref, o_ref, wab, wbb):
    @pl.when(pl.program_id(1) == 0)
    def _():
        wab[...] = wa_ref[...].astype(BF16)
        wbb[...] = wb_ref[...].astype(BF16)

    a = jnp.dot(ya_ref[...], wab[...], preferred_element_type=F32)
    b = jnp.dot(yb_ref[...], wbb[...], preferred_element_type=F32)
    o_ref[...] = (ga_ref[...].astype(F32) * a + gb_ref[...].astype(F32) * b).astype(o_ref.dtype)


def merge_call(ya, yb, w_a, w_b, proj, d_model):
    s = ya.shape[0]
    ga0 = (proj.shape[1] - 2 * d_model) // MM_TN
    gb0 = ga0 + d_model // MM_TN
    return pl.pallas_call(
        _merge_kernel,
        grid=(d_model // MM_TN, s // MM_TM),
        in_specs=[pl.BlockSpec((MM_TM, A_Q_W), lambda j, i: (i, 0)),
                  pl.BlockSpec((MM_TM, B_W), lambda j, i: (i, 0)),
                  pl.BlockSpec((A_Q_W, MM_TN), lambda j, i: (0, j)),
                  pl.BlockSpec((B_W, MM_TN), lambda j, i: (0, j)),
                  pl.BlockSpec((MM_TM, MM_TN), lambda j, i: (i, ga0 + j)),
                  pl.BlockSpec((MM_TM, MM_TN), lambda j, i: (i, gb0 + j))],
        out_specs=pl.BlockSpec((MM_TM, MM_TN), lambda j, i: (i, j)),
        out_shape=jax.ShapeDtypeStruct((s, d_model), BF16),
        scratch_shapes=[pltpu.VMEM((A_Q_W, MM_TN), BF16), pltpu.VMEM((B_W, MM_TN), BF16)],
        compiler_params=_params(("parallel", "arbitrary")),
        name="merge",
    )(ya, yb, w_a, w_b, proj, proj)


def _resmm_kernel(a_ref, w_ref, r_ref, o_ref, wb_ref):
    @pl.when(pl.program_id(1) == 0)
    def _():
        wb_ref[...] = w_ref[...].astype(BF16)

    o_ref[...] = r_ref[...] + jnp.dot(a_ref[...], wb_ref[...], preferred_element_type=F32)


def resmm_call(a, w, res):
    s, k = a.shape
    n = w.shape[1]
    return pl.pallas_call(
        _resmm_kernel,
        grid=(n // MM_TN, s // MM_TM),
        in_specs=[pl.BlockSpec((MM_TM, k), lambda j, i: (i, 0)),
                  pl.BlockSpec((k, MM_TN), lambda j, i: (0, j)),
                  pl.BlockSpec((MM_TM, MM_TN), lambda j, i: (i, j))],
        out_specs=pl.BlockSpec((MM_TM, MM_TN), lambda j, i: (i, j)),
        out_shape=jax.ShapeDtypeStruct((s, n), F32),
        scratch_shapes=[pltpu.VMEM((k, MM_TN), BF16)],
        compiler_params=_params(("parallel", "arbitrary")),
        name="resmm",
    )(a, w, res)


def _ple_kernel(a_ref, wg_ref, p_ref, wp_ref, h_ref, o_ref, wgb, wpb):
    @pl.when(pl.program_id(1) == 0)
    def _():
        wgb[...] = wg_ref[...].astype(BF16)
        wpb[...] = wp_ref[...].astype(BF16)

    gate = jax.nn.sigmoid(jnp.dot(a_ref[...], wgb[...], preferred_element_type=F32))
    emb = jnp.dot(p_ref[...].astype(BF16), wpb[...], preferred_element_type=F32)
    o_ref[...] = h_ref[...] + gate * emb


def ple_call(xn, w_gate, p, w_ple, h):
    s, d = xn.shape
    pd = p.shape[1]
    return pl.pallas_call(
        _ple_kernel,
        grid=(d // MM_TN, s // MM_TM),
        in_specs=[pl.BlockSpec((MM_TM, d), lambda j, i: (i, 0)),
                  pl.BlockSpec((d, MM_TN), lambda j, i: (0, j)),
                  pl.BlockSpec((MM_TM, pd), lambda j, i: (i, 0)),
                  pl.BlockSpec((pd, MM_TN), lambda j, i: (0, j)),
                  pl.BlockSpec((MM_TM, MM_TN), lambda j, i: (i, j))],
        out_specs=pl.BlockSpec((MM_TM, MM_TN), lambda j, i: (i, j)),
        out_shape=jax.ShapeDtypeStruct((s, d), F32),
        scratch_shapes=[pltpu.VMEM((d, MM_TN), BF16), pltpu.VMEM((pd, MM_TN), BF16)],
        compiler_params=_params(("parallel", "arbitrary")),
        name="ple",
    )(xn, w_gate, p, w_ple, h)


def _dispatch_kernel(nused_ref, tok_ref, tokn_ref, x_hbm, o_ref, buf, sem):
    m = pl.program_id(0)
    nused = nused_ref[0]
    slot = m % 2

    def row_copy(tok, sl, r):
        return pltpu.make_async_copy(x_hbm.at[tok], buf.at[sl, r], sem.at[sl])

    def issue(t_ref, sl):
        def body(r2, c):
            for pri in range(2):
                r = 2 * r2 + pri
                row_copy(t_ref[0, 0, r], sl, r).start(priority=pri)
            return c

        lax.fori_loop(0, MOE_TM // 2, body, 0)

    @pl.when(m == 0)
    def _():
        issue(tok_ref, 0)

    @pl.when(m + 1 < nused)
    def _():
        issue(tokn_ref, 1 - slot)

    @pl.when(m < nused)
    def _():
        pltpu.make_async_copy(x_hbm.at[pl.ds(0, MOE_TM)], buf.at[slot], sem.at[slot]).wait()
        o_ref[...] = buf[slot].reshape(o_ref.shape)

    @pl.when(m >= nused)
    def _():
        o_ref[...] = jnp.zeros(o_ref.shape, o_ref.dtype)


def dispatch_call(n_used, row_tok, xn3d):
    s, dg, _ = xn3d.shape
    d = dg * LANES
    n_rows = row_tok.shape[0]
    nblk = n_rows // MOE_TM
    tok3 = row_tok.reshape(nblk, 1, MOE_TM)
    tok_spec = lambda f: pl.BlockSpec((1, 1, MOE_TM), f, memory_space=pltpu.SMEM)
    return pl.pallas_call(
        _dispatch_kernel,
        grid_spec=pltpu.PrefetchScalarGridSpec(
            num_scalar_prefetch=1,
            grid=(nblk,),
            in_specs=[tok_spec(lambda m, nu: (m, 0, 0)),
                      tok_spec(lambda m, nu: (jnp.minimum(m + 1, nblk - 1), 0, 0)),
                      pl.BlockSpec(memory_space=pl.ANY)],
            out_specs=pl.BlockSpec((MOE_TM, d), lambda m, nu: (m, 0)),
            scratch_shapes=[pltpu.VMEM((2, MOE_TM, dg, LANES), BF16), pltpu.SemaphoreType.DMA((2,))]),
        out_shape=jax.ShapeDtypeStruct((n_rows, d), BF16),
        compiler_params=_params(("arbitrary",)),
        name="dispatch",
    )(n_used, tok3, tok3, xn3d)


def _expert_changed(be_ref, m):
    return jnp.logical_or(m == 0, be_ref[m] != be_ref[jnp.maximum(m - 1, 0)])


def _for_live_halves(bv_ref, m, o_ref, compute_half):
    hrows = MOE_TM // 2
    for hf in range(2):
        rows = slice(hf * hrows, (hf + 1) * hrows)
        live = bv_ref[m] > hf * hrows

        @pl.when(live)
        def _():
            compute_half(rows)

        @pl.when(jnp.logical_not(live))
        def _():
            o_ref[rows] = jnp.zeros((hrows,) + o_ref.shape[1:], o_ref.dtype)


def _gateup_kernel(be_ref, bv_ref, nused_ref, x_ref, wgu_ref, bgu_ref, o_ref, wb):
    m = pl.program_id(1)

    @pl.when(jnp.logical_and(bv_ref[m] > 0, _expert_changed(be_ref, m)))
    def _():
        wb[...] = wgu_ref[...].astype(BF16)

    def compute_half(rows):
        hgu = jnp.dot(x_ref[rows], wb[...], preferred_element_type=F32) + bgu_ref[...]
        gate = jnp.minimum(hgu, SWIGLU_LIMIT)
        up = jnp.clip(hgu, -SWIGLU_LIMIT, SWIGLU_LIMIT)
        up = pltpu.roll(up, 2 * MOE_TF - 1, 1)
        act = gate * jax.nn.sigmoid(gate * SWIGLU_ALPHA) * (up + 1.0)
        rr = lax.broadcasted_iota(jnp.int32, (2 * MOE_TF, MOE_TF), 0)
        cc = lax.broadcasted_iota(jnp.int32, (2 * MOE_TF, MOE_TF), 1)
        sel = (rr == 2 * cc).astype(BF16)
        o_ref[rows] = jnp.dot(act.astype(BF16), sel, preferred_element_type=F32).astype(o_ref.dtype)

    _for_live_halves(bv_ref, m, o_ref, compute_half)


def gateup_call(block_e, block_valid, n_used, xs, w_gu, b_gu):
    n_rows, dh = xs.shape
    e, d, f2 = w_gu.shape
    f = f2 // 2
    nblk = n_rows // MOE_TM

    def meff(m, nu):
        return jnp.minimum(m, nu[0] - 1)

    return pl.pallas_call(
        _gateup_kernel,
        grid_spec=pltpu.PrefetchScalarGridSpec(
            num_scalar_prefetch=3,
            grid=(f // MOE_TF, nblk),
            in_specs=[pl.BlockSpec((MOE_TM, dh), lambda j, m, be, bv, nu: (meff(m, nu), 0)),
                      pl.BlockSpec((None, d, 2 * MOE_TF), lambda j, m, be, bv, nu: (be[m], 0, j)),
                      pl.BlockSpec((None, 1, 2 * MOE_TF), lambda j, m, be, bv, nu: (be[m], 0, j))],
            out_specs=pl.BlockSpec((MOE_TM, MOE_TF), lambda j, m, be, bv, nu: (m, j)),
            scratch_shapes=[pltpu.VMEM((d, 2 * MOE_TF), BF16)]),
        out_shape=jax.ShapeDtypeStruct((n_rows, f), BF16),
        compiler_params=_params(("arbitrary", "arbitrary")),
        name="expert_gateup",
    )(block_e, block_valid, n_used, xs, w_gu, b_gu.reshape(e, 1, f2))


def _down_kernel(be_ref, bv_ref, nused_ref, a_ref, wdn_ref, bdn_ref, o_ref, wb):
    m = pl.program_id(1)

    @pl.when(jnp.logical_and(bv_ref[m] > 0, _expert_changed(be_ref, m)))
    def _():
        wb[...] = wdn_ref[...].astype(BF16)

    def compute_half(rows):
        out = jnp.dot(a_ref[rows], wb[...], preferred_element_type=F32) + bdn_ref[...]
        o_ref[rows] = _to_token_rows(out.astype(o_ref.dtype))

    _for_live_halves(bv_ref, m, o_ref, compute_half)


def down_call(block_e, block_valid, n_used, act, w_dn, b_dn):
    n_rows, f = act.shape
    e, _, d = w_dn.shape
    nblk = n_rows // MOE_TM

    def meff(m, nu):
        return jnp.minimum(m, nu[0] - 1)

    return pl.pallas_call(
        _down_kernel,
        grid_spec=pltpu.PrefetchScalarGridSpec(
            num_scalar_prefetch=3,
            grid=(d // MOE_TN, nblk),
            in_specs=[pl.BlockSpec((MOE_TM, f), lambda j, m, be, bv, nu: (meff(m, nu), 0)),
                      pl.BlockSpec((None, f, MOE_TN), lambda j, m, be, bv, nu: (be[m], 0, j)),
                      pl.BlockSpec((None, 1, MOE_TN), lambda j, m, be, bv, nu: (be[m], 0, j))],
            out_specs=pl.BlockSpec((MOE_TM, MOE_TN // LANES, LANES), lambda j, m, be, bv, nu: (m, j, 0)),
            scratch_shapes=[pltpu.VMEM((f, MOE_TN), BF16)]),
        out_shape=jax.ShapeDtypeStruct((n_rows, d // LANES, LANES), BF16),
        compiler_params=_params(("arbitrary", "arbitrary")),
        name="expert_down",
    )(block_e, block_valid, n_used, act, w_dn, b_dn.reshape(e, 1, d))


def _combine_kernel(pos_ref, posn_ref, ys_hbm, h_ref, gt_ref, g_ref, h_out, xn_out, buf, sem):
    i = pl.program_id(0)
    n = pl.num_programs(0)
    slot = i % 2

    def row_copy(row, sl, k, t):
        return pltpu.make_async_copy(ys_hbm.at[row], buf.at[sl, k, t], sem.at[sl])

    def issue(p_ref, sl):
        def body(t, c):
            for k in range(TOP_K):
                row_copy(p_ref[0, 0, t * TOP_K + k], sl, k, t).start(priority=k % 2)
            return c

        lax.fori_loop(0, CMB_TB, body, 0)

    @pl.when(i == 0)
    def _():
        issue(pos_ref, 0)

    @pl.when(i + 1 < n)
    def _():
        issue(posn_ref, 1 - slot)

    for k in range(TOP_K):
        pltpu.make_async_copy(ys_hbm.at[pl.ds(0, CMB_TB)], buf.at[slot, k], sem.at[slot]).wait()

    h = h_ref[...]
    for k in range(TOP_K):
        rows = buf[slot, k].reshape(h.shape).astype(F32)
        h = h + gt_ref[:, k:k + 1] * rows
    h_out[...] = h
    ms = jnp.mean(h * h, axis=-1, keepdims=True)
    xn_out[...] = (h * lax.rsqrt(ms + RMS_EPS) * g_ref[...]).astype(xn_out.dtype)


def combine_call(pos, ys, h, gates, g):
    s, d = h.shape
    nb = s // CMB_TB
    pos3 = pos.reshape(nb, 1, CMB_TB * TOP_K)
    pos_spec = lambda f: pl.BlockSpec((1, 1, CMB_TB * TOP_K), f, memory_space=pltpu.SMEM)
    return pl.pallas_call(
        _combine_kernel,
        grid=(nb,),
        in_specs=[pos_spec(lambda i: (i, 0, 0)),
                  pos_spec(lambda i: (jnp.minimum(i + 1, nb - 1), 0, 0)),
                  pl.BlockSpec(memory_space=pl.ANY),
                  pl.BlockSpec((CMB_TB, d), lambda i: (i, 0)),
                  pl.BlockSpec((CMB_TB, LANES), lambda i: (i, 0)),
                  pl.BlockSpec((1, d), lambda i: (0, 0))],
        out_specs=[pl.BlockSpec((CMB_TB, d), lambda i: (i, 0)),
                   pl.BlockSpec((CMB_TB, d), lambda i: (i, 0))],
        out_shape=[jax.ShapeDtypeStruct((s, d), F32), jax.ShapeDtypeStruct((s, d), BF16)],
        scratch_shapes=[pltpu.VMEM((2, TOP_K, CMB_TB, d // LANES, LANES), BF16),
                        pltpu.SemaphoreType.DMA((2,))],
        compiler_params=_params(("arbitrary",)),
        name="combine",
    )(pos3, pos3, ys, h, gates, g.reshape(1, d))


def moe_routing(top_e, n_tokens):
    e_flat = top_e.reshape(-1)
    onehot = (e_flat[:, None] == jnp.arange(N_EXPERTS, dtype=jnp.int32)[None, :]).astype(jnp.int32)
    csum = jnp.cumsum(onehot, axis=0)
    counts = csum[-1]
    rank = jnp.sum(csum * onehot, axis=1) - 1
    padded = (counts + MOE_TM - 1) // MOE_TM * MOE_TM
    pends = jnp.cumsum(padded).astype(jnp.int32)
    pstarts = pends - padded
    pos = pstarts[e_flat] + rank
    n_blocks = (n_tokens * TOP_K) // MOE_TM + N_EXPERTS
    tok_flat = jnp.arange(n_tokens * TOP_K, dtype=jnp.int32) // TOP_K
    row_tok = jnp.zeros((n_blocks * MOE_TM,), jnp.int32).at[pos].set(tok_flat)
    n_used = pends[-1] // MOE_TM
    blk_start = jnp.arange(n_blocks, dtype=jnp.int32) * MOE_TM
    block_e = jnp.sum((pends[None, :] <= blk_start[:, None]).astype(jnp.int32), axis=1)
    block_e = jnp.minimum(block_e, N_EXPERTS - 1)
    last_e = block_e[jnp.maximum(n_used - 1, 0)]
    used = jnp.arange(n_blocks) < n_used
    real_end = pstarts + counts
    block_valid = jnp.clip(real_end[block_e] - blk_start, 0, MOE_TM)
    block_valid = jnp.where(used, block_valid, 0).astype(jnp.int32)
    block_e = jnp.where(used, block_e, last_e).astype(jnp.int32)
    return pos.astype(jnp.int32), row_tok, block_e, block_valid, n_used.reshape(1).astype(jnp.int32)


def kernel(x, p, g_mix, w_in, attn_sink, na_rpb, w_branch_a, w_branch_b, w_out, g_ffn, w_router,
           b_router, w_gate_up, b_gate_up, w_down, b_down, g_ple, w_ple_gate, w_ple, g_final):
    b, s, d = x.shape
    assert b == 1 and w_in.shape[0] == 1
    h = x.reshape(s, d)
    rope_c, rope_s1, rope_s2 = rope_tables(s)

    xn = rmsnorm_call(h, g_mix[0], BF16)
    proj = inproj_call(xn, w_in[0], rope_c, rope_s1, rope_s2, d)
    ya = window_attn_call(proj, attn_sink[0])
    yb = na_attn_call(proj, na_bias_tables(na_rpb[0]))
    merged = merge_call(ya, yb, w_branch_a[0], w_branch_b[0], proj, d)
    h = resmm_call(merged, w_out[0], h)

    xn2, top_e, gates = rms_router_call(h, g_ffn[0], w_router[0], b_router[0])
    pos, row_tok, block_e, block_valid, n_used = moe_routing(top_e[:, :TOP_K], s)
    xs = dispatch_call(n_used, row_tok, xn2)
    act = gateup_call(block_e, block_valid, n_used, xs, w_gate_up[0], b_gate_up[0])
    ys = down_call(block_e, block_valid, n_used, act, w_down[0], b_down[0])
    h, xn3 = combine_call(pos, ys, h, gates, g_ple[0])

    h = ple_call(xn3, w_ple_gate[0], p[0].reshape(s, -1), w_ple[0], h)
    out = rmsnorm_call(h, g_final, F32)
    return out.reshape(b, s, d)
```

```python
import functools

import numpy as np
import jax
import jax.numpy as jnp
from jax import lax
from jax.experimental import pallas as pl
from jax.experimental.pallas import tpu as pltpu

F32 = jnp.float32
BF16 = jnp.bfloat16

HEAD_DIM = 64
A_HEADS = 32
A_KV_HEADS = 8
A_GROUP = A_HEADS // A_KV_HEADS
B_HEADS = 32
A_Q_W = A_HEADS * HEAD_DIM
A_KV_W = A_KV_HEADS * HEAD_DIM
B_W = B_HEADS * HEAD_DIM
ROT_DIM = HEAD_DIM // 4
ROPE_THETA = 500000.0
WINDOW = 128
GRID_W = 64
NA_ROWS = 8
NA_COLS = 16
N_EXPERTS = 32
TOP_K = 4
SWIGLU_LIMIT = 7.0
SWIGLU_ALPHA = 1.702
RMS_EPS = 1e-6
NEG_INF = -1e30

LANES = 128
VMEM_LIMIT = 56 * 1024 * 1024

RMS_ROWS = 256
MM_TM = 1024
MM_TN = 512
ATT_BLK = 128
NA_RB = 8
NA_HPS = 4
MOE_TM = 512
MOE_TF = 256
MOE_TN = 2048
CMB_TB = 128
DMA_UNROLL = 8


def _params(sem):
    return pltpu.CompilerParams(dimension_semantics=sem, vmem_limit_bytes=VMEM_LIMIT)


def _rms_body(x_ref, g_ref):
    x = x_ref[...].astype(F32)
    ms = jnp.mean(x * x, axis=-1, keepdims=True)
    return x * lax.rsqrt(ms + RMS_EPS) * g_ref[...]


def _rms_kernel(x_ref, g_ref, o_ref):
    o_ref[...] = _rms_body(x_ref, g_ref).astype(o_ref.dtype)


def rmsnorm_call(x, g, out_dtype):
    s, d = x.shape
    return pl.pallas_call(
        _rms_kernel,
        grid=(s // RMS_ROWS,),
        in_specs=[pl.BlockSpec((RMS_ROWS, d), lambda i: (i, 0)),
                  pl.BlockSpec((1, d), lambda i: (0, 0))],
        out_specs=pl.BlockSpec((RMS_ROWS, d), lambda i: (i, 0)),
        out_shape=jax.ShapeDtypeStruct((s, d), out_dtype),
        compiler_params=_params(("parallel",)),
        name="rmsnorm",
    )(x, g.reshape(1, d))


def _to_token_rows(x2d):
    return x2d.reshape(x2d.shape[0], x2d.shape[1] // LANES, LANES)


def _split_bf16(x):
    hi = x.astype(BF16)
    return hi, (x - hi.astype(F32)).astype(BF16)


def _rms_router_kernel(x_ref, g_ref, wh_ref, wl_ref, br_ref, xn_ref, e_ref, gt_ref):
    xn = _rms_body(x_ref, g_ref)
    xh, xl = _split_bf16(xn)
    xn_ref[...] = _to_token_rows(xh)
    logits = (jnp.dot(xh, wh_ref[...], preferred_element_type=F32)
              + jnp.dot(xh, wl_ref[...], preferred_element_type=F32)
              + jnp.dot(xl, wh_ref[...], preferred_element_type=F32)) + br_ref[...]
    lane = lax.broadcasted_iota(jnp.int32, logits.shape, 1)
    logits = jnp.where(lane < N_EXPERTS, logits, -jnp.inf)
    e_out = jnp.zeros(logits.shape, jnp.int32)
    v_out = jnp.zeros(logits.shape, F32)
    v0 = None
    for k in range(TOP_K):
        m = jnp.max(logits, axis=-1, keepdims=True)
        idx = jnp.min(jnp.where(logits == m, lane, LANES), axis=-1, keepdims=True)
        if k == 0:
            v0 = m
        e_out = jnp.where(lane == k, idx, e_out)
        v_out = jnp.where(lane == k, jnp.exp(m - v0), v_out)
        logits = jnp.where(lane == idx, -jnp.inf, logits)
    e_ref[...] = e_out
    gt_ref[...] = v_out / jnp.sum(v_out, axis=-1, keepdims=True)


def rms_router_call(h, g, w_router, b_router):
    s, d = h.shape
    wr = jnp.zeros((d, LANES), F32).at[:, :N_EXPERTS].set(w_router)
    br = jnp.zeros((1, LANES), F32).at[0, :N_EXPERTS].set(b_router)
    wr_hi, wr_lo = _split_bf16(wr)
    return pl.pallas_call(
        _rms_router_kernel,
        grid=(s // RMS_ROWS,),
        in_specs=[pl.BlockSpec((RMS_ROWS, d), lambda i: (i, 0)),
                  pl.BlockSpec((1, d), lambda i: (0, 0)),
                  pl.BlockSpec((d, LANES), lambda i: (0, 0)),
                  pl.BlockSpec((d, LANES), lambda i: (0, 0)),
                  pl.BlockSpec((1, LANES), lambda i: (0, 0))],
        out_specs=[pl.BlockSpec((RMS_ROWS, d // LANES, LANES), lambda i: (i, 0, 0)),
                   pl.BlockSpec((RMS_ROWS, LANES), lambda i: (i, 0)),
                   pl.BlockSpec((RMS_ROWS, LANES), lambda i: (i, 0))],
        out_shape=[jax.ShapeDtypeStruct((s, d // LANES, LANES), BF16),
                   jax.ShapeDtypeStruct((s, LANES), jnp.int32),
                   jax.ShapeDtypeStruct((s, LANES), F32)],
        compiler_params=_params(("parallel",)),
        name="rms_router",
    )(h, g.reshape(1, d), wr_hi, wr_lo, br)


def _rope(acc, c_ref, s1_ref, s2_ref):
    c, s1, s2 = c_ref[...], s1_ref[...], s2_ref[...]
    half = ROT_DIM // 2
    outs = []
    for t in range(acc.shape[1] // LANES):
        x = acc[:, t * LANES:(t + 1) * LANES]
        outs.append(x * c + pltpu.roll(x, LANES - half, 1) * s1 + pltpu.roll(x, half, 1) * s2)
    return jnp.concatenate(outs, axis=1)


def _inproj_kernel(a_ref, w_ref, c_ref, s1_ref, s2_ref, o_ref, wb_ref, *, n_rope, n_plain):
    j = pl.program_id(0)

    @pl.when(pl.program_id(1) == 0)
    def _():
        wb_ref[...] = w_ref[...].astype(BF16)

    acc = jnp.dot(a_ref[...], wb_ref[...], preferred_element_type=F32)

    @pl.when(j < n_rope)
    def _():
        o_ref[...] = _rope(acc, c_ref, s1_ref, s2_ref).astype(o_ref.dtype)

    @pl.when(jnp.logical_and(j >= n_rope, j < n_plain))
    def _():
        o_ref[...] = acc.astype(o_ref.dtype)

    @pl.when(j >= n_plain)
    def _():
        o_ref[...] = jax.nn.sigmoid(acc).astype(o_ref.dtype)


def inproj_call(xn, w_in, rope_c, rope_s1, rope_s2, d_model):
    s, d = xn.shape
    n = w_in.shape[1]
    n_rope = (A_Q_W + A_KV_W) // MM_TN
    n_plain = (n - 2 * d_model) // MM_TN
    tab = pl.BlockSpec((MM_TM, LANES), lambda j, i: (i, 0))
    return pl.pallas_call(
        functools.partial(_inproj_kernel, n_rope=n_rope, n_plain=n_plain),
        grid=(n // MM_TN, s // MM_TM),
        in_specs=[pl.BlockSpec((MM_TM, d), lambda j, i: (i, 0)),
                  pl.BlockSpec((d, MM_TN), lambda j, i: (0, j)),
                  tab, tab, tab],
        out_specs=pl.BlockSpec((MM_TM, MM_TN), lambda j, i: (i, j)),
        out_shape=jax.ShapeDtypeStruct((s, n), BF16),
        scratch_shapes=[pltpu.VMEM((d, MM_TN), BF16)],
        compiler_params=_params(("parallel", "arbitrary")),
        name="inproj",
    )(xn, w_in, rope_c, rope_s1, rope_s2)


def rope_tables(s):
    half = ROT_DIM // 2
    inv = jnp.float32(ROPE_THETA) ** (-jnp.arange(half, dtype=F32) * (2.0 / ROT_DIM))
    ang = jnp.arange(s, dtype=F32)[:, None] * inv[None, :]
    cos, sin = jnp.cos(ang), jnp.sin(ang)
    ones = jnp.ones((s, HEAD_DIM - ROT_DIM), F32)
    zeros = jnp.zeros((s, HEAD_DIM - ROT_DIM), F32)
    zh = jnp.zeros((s, half), F32)
    c = jnp.concatenate([cos, cos, ones], axis=1)
    s1 = jnp.concatenate([-sin, zh, zeros], axis=1)
    s2 = jnp.concatenate([zh, sin, zeros], axis=1)
    rep = LANES // HEAD_DIM
    return jnp.tile(c, (1, rep)), jnp.tile(s1, (1, rep)), jnp.tile(s2, (1, rep))


def _win_kernel(sink_ref, q_ref, kp_ref, kc_ref, kn_ref, vp_ref, vc_ref, vn_ref, o_ref, *, seq):
    n = pl.program_id(0)
    pr = pl.program_id(1)
    kcat = jnp.concatenate([kp_ref[...], kc_ref[...], kn_ref[...]], axis=0)
    vcat = jnp.concatenate([vp_ref[...], vc_ref[...], vn_ref[...]], axis=0)
    q = q_ref[...] * jnp.asarray(HEAD_DIM ** -0.5, BF16)
    rows = A_GROUP * ATT_BLK
    nk = 3 * ATT_BLK
    qpos = n * ATT_BLK + lax.broadcasted_iota(jnp.int32, (ATT_BLK, nk), 0)
    kpos = (n - 1) * ATT_BLK + lax.broadcasted_iota(jnp.int32, (ATT_BLK, nk), 1)
    mask = ((jnp.abs(qpos - kpos) <= WINDOW) & (kpos >= 0) & (kpos < seq))[None]
    grp = lax.broadcasted_iota(jnp.int32, (A_GROUP, 1, 1), 0)
    kv_per_blk = LANES // HEAD_DIM
    for kvh in range(kv_per_blk):
        k_h = kcat[:, kvh * HEAD_DIM:(kvh + 1) * HEAD_DIM]
        v_h = vcat[:, kvh * HEAD_DIM:(kvh + 1) * HEAD_DIM]
        h0 = kvh * A_GROUP
        qs = jnp.concatenate(
            [q[:, (h0 + g) * HEAD_DIM:(h0 + g + 1) * HEAD_DIM] for g in range(A_GROUP)], axis=0)
        sc = lax.dot_general(qs, k_h, (((1,), (1,)), ((), ())), preferred_element_type=F32)
        sc = jnp.where(mask, sc.reshape(A_GROUP, ATT_BLK, nk), NEG_INF)
        snk = jnp.zeros((A_GROUP, 1, 1), F32)
        for g in range(A_GROUP):
            sv = sink_ref[(pr * kv_per_blk + kvh) * A_GROUP + g]
            snk = jnp.where(grp == g, sv, snk)
        m = jnp.maximum(jnp.max(sc, axis=-1, keepdims=True), snk)
        p = jnp.exp(sc - m)
        denom = jnp.sum(p, axis=-1, keepdims=True) + jnp.exp(snk - m)
        attn = (p * (1.0 / denom)).astype(BF16).reshape(rows, nk)
        out = jnp.dot(attn, v_h, preferred_element_type=F32)
        for g in range(A_GROUP):
            o_ref[:, (h0 + g) * HEAD_DIM:(h0 + g + 1) * HEAD_DIM] = (
                out[g * ATT_BLK:(g + 1) * ATT_BLK].astype(o_ref.dtype))


def window_attn_call(proj, sink):
    s = proj.shape[0]
    nb = s // ATT_BLK
    qw = A_GROUP * LANES
    k0 = A_Q_W // LANES
    v0 = (A_Q_W + A_KV_W) // LANES
    blk = (ATT_BLK, LANES)
    prev = lambda n: jnp.maximum(n - 1, 0)
    nxt = lambda n: jnp.minimum(n + 1, nb - 1)
    return pl.pallas_call(
        functools.partial(_win_kernel, seq=s),
        grid=(nb, A_Q_W // qw),
        in_specs=[pl.BlockSpec(memory_space=pltpu.SMEM),
                  pl.BlockSpec((ATT_BLK, qw), lambda n, p: (n, p)),
                  pl.BlockSpec(blk, lambda n, p: (prev(n), k0 + p)),
                  pl.BlockSpec(blk, lambda n, p: (n, k0 + p)),
                  pl.BlockSpec(blk, lambda n, p: (nxt(n), k0 + p)),
                  pl.BlockSpec(blk, lambda n, p: (prev(n), v0 + p)),
                  pl.BlockSpec(blk, lambda n, p: (n, v0 + p)),
                  pl.BlockSpec(blk, lambda n, p: (nxt(n), v0 + p))],
        out_specs=pl.BlockSpec((ATT_BLK, qw), lambda n, p: (n, p)),
        out_shape=jax.ShapeDtypeStruct((s, A_Q_W), BF16),
        compiler_params=_params(("parallel", "parallel")),
        name="window_attn",
    )(sink, proj, proj, proj, proj, proj, proj, proj)


def na_bias_tables(rpb):
    c = np.arange(GRID_W)
    kc = np.arange(GRID_W)
    dc = np.clip(kc[None, :] - c[:, None] + NA_COLS - 1, 0, 2 * NA_COLS - 2)
    qcs = np.clip(c - NA_COLS // 2, 0, GRID_W - NA_COLS)
    cmask = (kc[None, :] >= qcs[:, None]) & (kc[None, :] < qcs[:, None] + NA_COLS)
    t = jnp.where(jnp.asarray(cmask)[None, None], rpb[:, :, dc].astype(F32), NEG_INF)
    t = jnp.transpose(t, (0, 2, 1, 3)).reshape(rpb.shape[0], GRID_W, (2 * NA_ROWS - 1) * GRID_W)
    return jnp.pad(t, ((0, 0), (0, 0), (GRID_W, 0)))


def _na_kernel(q_ref, kp_ref, kc_ref, kn_ref, vp_ref, vc_ref, vn_ref, b_ref, o_ref,
               kbuf, vbuf, s_scr, p_scr, *, grid_rows):
    rb = pl.program_id(1)
    blk = NA_RB * GRID_W
    kbuf[0:blk] = kp_ref[...]
    kbuf[blk:2 * blk] = kc_ref[...]
    kbuf[2 * blk:3 * blk] = kn_ref[...]
    vbuf[0:blk] = vp_ref[...]
    vbuf[blk:2 * blk] = vc_ref[...]
    vbuf[2 * blk:3 * blk] = vn_ref[...]
    nkeys = NA_ROWS * GRID_W
    width = NA_HPS * HEAD_DIM
    grp = NA_HPS * GRID_W
    row_head = lax.broadcasted_iota(jnp.int32, (grp, width), 0) // GRID_W
    lane_head = lax.broadcasted_iota(jnp.int32, (grp, width), 1) // HEAD_DIM
    own = row_head == lane_head
    out_head = lax.broadcasted_iota(jnp.int32, (GRID_W, width), 1) // HEAD_DIM
    nrb = grid_rows // NA_RB
    half = NA_ROWS // 2

    def rows(first_rb):
        starts = []
        for i in range(NA_RB):
            r = first_rb * NA_RB + i
            rs = min(max(r - half, 0), grid_rows - NA_ROWS)
            typ = rs - (r - half) + (half - 1)
            start = (rs - (first_rb - 1) * NA_RB) * GRID_W
            starts.append(start)
            qi = q_ref[i * GRID_W:(i + 1) * GRID_W, :] * jnp.asarray(HEAD_DIM ** -0.5, BF16)
            qs = jnp.where(own, jnp.concatenate([qi] * NA_HPS, axis=0), jnp.zeros((grp, width), BF16))
            sc = lax.dot_general(qs, kbuf[start:start + nkeys, :], (((1,), (1,)), ((), ())),
                                 preferred_element_type=F32)
            boff = (typ + 1) * GRID_W
            s_scr[i * grp:(i + 1) * grp, :] = sc + b_ref[:, :, boff:boff + nkeys].reshape(grp, nkeys)
        for i in range(NA_RB):
            sc = s_scr[i * grp:(i + 1) * grp, :]
            m = jnp.max(sc, axis=-1, keepdims=True)
            p = jnp.exp(sc - m)
            p_scr[i * grp:(i + 1) * grp, :] = (
                p * (1.0 / jnp.sum(p, axis=-1, keepdims=True))).astype(BF16)
        for i in range(NA_RB):
            start = starts[i]
            out = jnp.dot(p_scr[i * grp:(i + 1) * grp, :], vbuf[start:start + nkeys, :],
                          preferred_element_type=F32)
            res = out[:GRID_W]
            for h in range(1, NA_HPS):
                res = jnp.where(out_head == h, out[h * GRID_W:(h + 1) * GRID_W], res)
            o_ref[i * GRID_W:(i + 1) * GRID_W, :] = res.astype(o_ref.dtype)

    @pl.when(rb == 0)
    def _():
        rows(0)

    @pl.when(rb == nrb - 1)
    def _():
        rows(nrb - 1)

    @pl.when(jnp.logical_and(rb > 0, rb < nrb - 1))
    def _():
        rows(1)


def na_attn_call(proj, bias_tab):
    s = proj.shape[0]
    grid_rows = s // GRID_W
    nrb = grid_rows // NA_RB
    blk_rows = NA_RB * GRID_W
    width = NA_HPS * HEAD_DIM
    q0 = (A_Q_W + 2 * A_KV_W) // width
    k0 = q0 + B_W // width
    v0 = k0 + B_W // width
    blk = (blk_rows, width)
    hpb = NA_HPS
    prev = lambda r: jnp.maximum(r - 1, 0)
    cur = lambda r: r
    nxt = lambda r: jnp.minimum(r + 1, nrb - 1)
    return pl.pallas_call(
        functools.partial(_na_kernel, grid_rows=grid_rows),
        grid=(B_HEADS // hpb, nrb),
        in_specs=[pl.BlockSpec(blk, lambda h, r: (r, q0 + h)),
                  pl.BlockSpec(blk, lambda h, r: (prev(r), k0 + h)),
                  pl.BlockSpec(blk, lambda h, r: (cur(r), k0 + h)),
                  pl.BlockSpec(blk, lambda h, r: (nxt(r), k0 + h)),
                  pl.BlockSpec(blk, lambda h, r: (prev(r), v0 + h)),
                  pl.BlockSpec(blk, lambda h, r: (cur(r), v0 + h)),
                  pl.BlockSpec(blk, lambda h, r: (nxt(r), v0 + h)),
                  pl.BlockSpec((hpb, GRID_W, 2 * NA_ROWS * GRID_W), lambda h, r: (h, 0, 0))],
        out_specs=pl.BlockSpec(blk, lambda h, r: (r, h)),
        out_shape=jax.ShapeDtypeStruct((s, B_W), BF16),
        scratch_shapes=[pltpu.VMEM((3 * blk_rows, width), BF16),
                        pltpu.VMEM((3 * blk_rows, width), BF16),
                        pltpu.VMEM((NA_RB * hpb * GRID_W, NA_ROWS * GRID_W), F32),
                        pltpu.VMEM((NA_RB * hpb * GRID_W, NA_ROWS * GRID_W), BF16)],
        compiler_params=_params(("parallel", "arbitrary")),
        name="na_attn",
    )(proj, proj, proj, proj, proj, proj, proj, bias_tab)


def _merge_kernel(ya_ref, yb_ref, wa_ref, wb_ref, ga_ref, gb_ref, o_ref, wab, wbb):
    @pl.when(pl.program_id(1) == 0)
    def _():
        wab[...] = wa_ref[...].astype(BF16)
        wbb[...] = wb_ref[...].astype(BF16)

    a = jnp.dot(ya_ref[...], wab[...], preferred_element_type=F32)
    b = jnp.dot(yb_ref[...], wbb[...], preferred_element_type=F32)
    o_ref[...] = (ga_ref[...].astype(F32) * a + gb_ref[...].astype(F32) * b).astype(o_ref.dtype)


def merge_call(ya, yb, w_a, w_b, proj, d_model):
    s = ya.shape[0]
    ga0 = (proj.shape[1] - 2 * d_model) // MM_TN
    gb0 = ga0 + d_model // MM_TN
    return pl.pallas_call(
        _merge_kernel,
        grid=(d_model // MM_TN, s // MM_TM),
        in_specs=[pl.BlockSpec((MM_TM, A_Q_W), lambda j, i: (i, 0)),
                  pl.BlockSpec((MM_TM, B_W), lambda j, i: (i, 0)),
                  pl.BlockSpec((A_Q_W, MM_TN), lambda j, i: (0, j)),
                  pl.BlockSpec((B_W, MM_TN), lambda j, i: (0, j)),
                  pl.BlockSpec((MM_TM, MM_TN), lambda j, i: (i, ga0 + j)),
                  pl.BlockSpec((MM_TM, MM_TN), lambda j, i: (i, gb0 + j))],
        out_specs=pl.BlockSpec((MM_TM, MM_TN), lambda j, i: (i, j)),
        out_shape=jax.ShapeDtypeStruct((s, d_model), BF16),
        scratch_shapes=[pltpu.VMEM((A_Q_W, MM_TN), BF16), pltpu.VMEM((B_W, MM_TN), BF16)],
        compiler_params=_params(("parallel", "arbitrary")),
        name="merge",
    )(ya, yb, w_a, w_b, proj, proj)


def _resmm_kernel(a_ref, w_ref, r_ref, o_ref, wb_ref):
    @pl.when(pl.program_id(1) == 0)
    def _():
        wb_ref[...] = w_ref[...].astype(BF16)

    o_ref[...] = r_ref[...] + jnp.dot(a_ref[...], wb_ref[...], preferred_element_type=F32)


def resmm_call(a, w, res):
    s, k = a.shape
    n = w.shape[1]
    return pl.pallas_call(
        _resmm_kernel,
        grid=(n // MM_TN, s // MM_TM),
        in_specs=[pl.BlockSpec((MM_TM, k), lambda j, i: (i, 0)),
                  pl.BlockSpec((k, MM_TN), lambda j, i: (0, j)),
                  pl.BlockSpec((MM_TM, MM_TN), lambda j, i: (i, j))],
        out_specs=pl.BlockSpec((MM_TM, MM_TN), lambda j, i: (i, j)),
        out_shape=jax.ShapeDtypeStruct((s, n), F32),
        scratch_shapes=[pltpu.VMEM((k, MM_TN), BF16)],
        compiler_params=_params(("parallel", "arbitrary")),
        name="resmm",
    )(a, w, res)


def _ple_kernel(a_ref, wg_ref, p_ref, wp_ref, h_ref, o_ref, wgb, wpb):
    @pl.when(pl.program_id(1) == 0)
    def _():
        wgb[...] = wg_ref[...].astype(BF16)
        wpb[...] = wp_ref[...].astype(BF16)

    gate = jax.nn.sigmoid(jnp.dot(a_ref[...], wgb[...], preferred_element_type=F32))
    emb = jnp.dot(p_ref[...].astype(BF16), wpb[...], preferred_element_type=F32)
    o_ref[...] = h_ref[...] + gate * emb


def ple_call(xn, w_gate, p, w_ple, h):
    s, d = xn.shape
    pd = p.shape[1]
    return pl.pallas_call(
        _ple_kernel,
        grid=(d // MM_TN, s // MM_TM),
        in_specs=[pl.BlockSpec((MM_TM, d), lambda j, i: (i, 0)),
                  pl.BlockSpec((d, MM_TN), lambda j, i: (0, j)),
                  pl.BlockSpec((MM_TM, pd), lambda j, i: (i, 0)),
                  pl.BlockSpec((pd, MM_TN), lambda j, i: (0, j)),
                  pl.BlockSpec((MM_TM, MM_TN), lambda j, i: (i, j))],
        out_specs=pl.BlockSpec((MM_TM, MM_TN), lambda j, i: (i, j)),
        out_shape=jax.ShapeDtypeStruct((s, d), F32),
        scratch_shapes=[pltpu.VMEM((d, MM_TN), BF16), pltpu.VMEM((pd, MM_TN), BF16)],
        compiler_params=_params(("parallel", "arbitrary")),
        name="ple",
    )(xn, w_gate, p, w_ple, h)


def _dispatch_kernel(nused_ref, tok_ref, tokn_ref, x_hbm, o_ref, buf, sem):
    m = pl.program_id(0)
    nused = nused_ref[0]
    slot = m % 2

    def row_copy(tok, sl, r):
        return pltpu.make_async_copy(x_hbm.at[tok], buf.at[sl, r], sem.at[sl])

    def issue(t_ref, sl):
        def body(g, c):
            toks = [t_ref[0, 0, g * DMA_UNROLL + u] for u in range(DMA_UNROLL)]
            for u in range(DMA_UNROLL):
                row_copy(toks[u], sl, g * DMA_UNROLL + u).start(priority=u % 2)
            return c

        lax.fori_loop(0, MOE_TM // DMA_UNROLL, body, 0)

    @pl.when(m == 0)
    def _():
        issue(tok_ref, 0)

    @pl.when(m + 1 < nused)
    def _():
        issue(tokn_ref, 1 - slot)

    @pl.when(m < nused)
    def _():
        pltpu.make_async_copy(x_hbm.at[pl.ds(0, MOE_TM)], buf.at[slot], sem.at[slot]).wait()
        o_ref[...] = buf[slot].reshape(o_ref.shape)

    @pl.when(m >= nused)
    def _():
        o_ref[...] = jnp.zeros(o_ref.shape, o_ref.dtype)


def dispatch_call(n_used, row_tok, xn3d):
    s, dg, _ = xn3d.shape
    d = dg * LANES
    n_rows = row_tok.shape[0]
    nblk = n_rows // MOE_TM
    tok3 = row_tok.reshape(nblk, 1, MOE_TM)
    tok_spec = lambda f: pl.BlockSpec((1, 1, MOE_TM), f, memory_space=pltpu.SMEM)
    return pl.pallas_call(
        _dispatch_kernel,
        grid_spec=pltpu.PrefetchScalarGridSpec(
            num_scalar_prefetch=1,
            grid=(nblk,),
            in_specs=[tok_spec(lambda m, nu: (m, 0, 0)),
                      tok_spec(lambda m, nu: (jnp.minimum(m + 1, nblk - 1), 0, 0)),
                      pl.BlockSpec(memory_space=pl.ANY)],
            out_specs=pl.BlockSpec((MOE_TM, d), lambda m, nu: (m, 0)),
            scratch_shapes=[pltpu.VMEM((2, MOE_TM, dg, LANES), BF16), pltpu.SemaphoreType.DMA((2,))]),
        out_shape=jax.ShapeDtypeStruct((n_rows, d), BF16),
        compiler_params=_params(("arbitrary",)),
        name="dispatch",
    )(n_used, tok3, tok3, xn3d)


def _expert_changed(be_ref, m):
    return jnp.logical_or(m == 0, be_ref[m] != be_ref[jnp.maximum(m - 1, 0)])


def _for_live_block(bv_ref, m, o_ref, compute):
    live = bv_ref[m] > 0

    @pl.when(live)
    def _():
        compute(slice(None))

    @pl.when(jnp.logical_not(live))
    def _():
        o_ref[...] = jnp.zeros(o_ref.shape, o_ref.dtype)


def _gateup_kernel(be_ref, bv_ref, nused_ref, x_ref, wgu_ref, bgu_ref, o_ref, wb):
    m = pl.program_id(1)

    @pl.when(jnp.logical_and(bv_ref[m] > 0, _expert_changed(be_ref, m)))
    def _():
        wb[...] = wgu_ref[...].astype(BF16)

    def compute_rows(rows):
        hgu = jnp.dot(x_ref[rows], wb[...], preferred_element_type=F32) + bgu_ref[...]
        gate = jnp.minimum(hgu, SWIGLU_LIMIT)
        up = jnp.clip(hgu, -SWIGLU_LIMIT, SWIGLU_LIMIT)
        up = pltpu.roll(up, 2 * MOE_TF - 1, 1)
        act = gate * jax.nn.sigmoid(gate * SWIGLU_ALPHA) * (up + 1.0)
        rr = lax.broadcasted_iota(jnp.int32, (2 * MOE_TF, MOE_TF), 0)
        cc = lax.broadcasted_iota(jnp.int32, (2 * MOE_TF, MOE_TF), 1)
        sel = (rr == 2 * cc).astype(BF16)
        o_ref[rows] = jnp.dot(act.astype(BF16), sel, preferred_element_type=F32).astype(o_ref.dtype)

    _for_live_block(bv_ref, m, o_ref, compute_rows)


def gateup_call(block_e, block_valid, n_used, xs, w_gu, b_gu):
    n_rows, dh = xs.shape
    e, d, f2 = w_gu.shape
    f = f2 // 2
    nblk = n_rows // MOE_TM

    def meff(m, nu):
        return jnp.minimum(m, nu[0] - 1)

    return pl.pallas_call(
        _gateup_kernel,
        grid_spec=pltpu.PrefetchScalarGridSpec(
            num_scalar_prefetch=3,
            grid=(f // MOE_TF, nblk),
            in_specs=[pl.BlockSpec((MOE_TM, dh), lambda j, m, be, bv, nu: (meff(m, nu), 0)),
                      pl.BlockSpec((None, d, 2 * MOE_TF), lambda j, m, be, bv, nu: (be[m], 0, j)),
                      pl.BlockSpec((None, 1, 2 * MOE_TF), lambda j, m, be, bv, nu: (be[m], 0, j))],
            out_specs=pl.BlockSpec((MOE_TM, MOE_TF), lambda j, m, be, bv, nu: (m, j)),
            scratch_shapes=[pltpu.VMEM((d, 2 * MOE_TF), BF16)]),
        out_shape=jax.ShapeDtypeStruct((n_rows, f), BF16),
        compiler_params=_params(("arbitrary", "arbitrary")),
        name="expert_gateup",
    )(block_e, block_valid, n_used, xs, w_gu, b_gu.reshape(e, 1, f2))


def _down_kernel(be_ref, bv_ref, nused_ref, a_ref, wdn_ref, bdn_ref, o_ref, wb):
    m = pl.program_id(1)

    @pl.when(jnp.logical_and(bv_ref[m] > 0, _expert_changed(be_ref, m)))
    def _():
        wb[...] = wdn_ref[...].astype(BF16)

    def compute_rows(rows):
        out = jnp.dot(a_ref[rows], wb[...], preferred_element_type=F32) + bdn_ref[...]
        o_ref[rows] = _to_token_rows(out.astype(o_ref.dtype))

    _for_live_block(bv_ref, m, o_ref, compute_rows)


def down_call(block_e, block_valid, n_used, act, w_dn, b_dn):
    n_rows, f = act.shape
    e, _, d = w_dn.shape
    nblk = n_rows // MOE_TM

    def meff(m, nu):
        return jnp.minimum(m, nu[0] - 1)

    return pl.pallas_call(
        _down_kernel,
        grid_spec=pltpu.PrefetchScalarGridSpec(
            num_scalar_prefetch=3,
            grid=(d // MOE_TN, nblk),
            in_specs=[pl.BlockSpec((MOE_TM, f), lambda j, m, be, bv, nu: (meff(m, nu), 0)),
                      pl.BlockSpec((None, f, MOE_TN), lambda j, m, be, bv, nu: (be[m], 0, j)),
                      pl.BlockSpec((None, 1, MOE_TN), lambda j, m, be, bv, nu: (be[m], 0, j))],
            out_specs=pl.BlockSpec((MOE_TM, MOE_TN // LANES, LANES), lambda j, m, be, bv, nu: (m, j, 0)),
            scratch_shapes=[pltpu.VMEM((f, MOE_TN), BF16)]),
        out_shape=jax.ShapeDtypeStruct((n_rows, d // LANES, LANES), BF16),
        compiler_params=_params(("arbitrary", "arbitrary")),
        name="expert_down",
    )(block_e, block_valid, n_used, act, w_dn, b_dn.reshape(e, 1, d))


def _combine_kernel(pos_ref, posn_ref, ys_hbm, h_ref, gt_ref, g_ref, h_out, xn_out, buf, sem):
    i = pl.program_id(0)
    n = pl.num_programs(0)
    slot = i % 2

    def row_copy(row, sl, k, t):
        return pltpu.make_async_copy(ys_hbm.at[row], buf.at[sl, k, t], sem.at[sl])

    def issue(p_ref, sl):
        def body(g, c):
            rows = [p_ref[0, 0, g * DMA_UNROLL + u] for u in range(DMA_UNROLL)]
            for u in range(DMA_UNROLL):
                t = g * (DMA_UNROLL // TOP_K) + u // TOP_K
                row_copy(rows[u], sl, u % TOP_K, t).start(priority=u % 2)
            return c

        lax.fori_loop(0, CMB_TB * TOP_K // DMA_UNROLL, body, 0)

    @pl.when(i == 0)
    def _():
        issue(pos_ref, 0)

    @pl.when(i + 1 < n)
    def _():
        issue(posn_ref, 1 - slot)

    for k in range(TOP_K):
        pltpu.make_async_copy(ys_hbm.at[pl.ds(0, CMB_TB)], buf.at[slot, k], sem.at[slot]).wait()

    h = h_ref[...]
    for k in range(TOP_K):
        rows = buf[slot, k].reshape(h.shape).astype(F32)
        h = h + gt_ref[:, k:k + 1] * rows
    h_out[...] = h
    ms = jnp.mean(h * h, axis=-1, keepdims=True)
    xn_out[...] = (h * lax.rsqrt(ms + RMS_EPS) * g_ref[...]).astype(xn_out.dtype)


def combine_call(pos, ys, h, gates, g):
    s, d = h.shape
    nb = s // CMB_TB
    pos3 = pos.reshape(nb, 1, CMB_TB * TOP_K)
    pos_spec = lambda f: pl.BlockSpec((1, 1, CMB_TB * TOP_K), f, memory_space=pltpu.SMEM)
    return pl.pallas_call(
        _combine_kernel,
        grid=(nb,),
        in_specs=[pos_spec(lambda i: (i, 0, 0)),
                  pos_spec(lambda i: (jnp.minimum(i + 1, nb - 1), 0, 0)),
                  pl.BlockSpec(memory_space=pl.ANY),
                  pl.BlockSpec((CMB_TB, d), lambda i: (i, 0)),
                  pl.BlockSpec((CMB_TB, LANES), lambda i: (i, 0)),
                  pl.BlockSpec((1, d), lambda i: (0, 0))],
        out_specs=[pl.BlockSpec((CMB_TB, d), lambda i: (i, 0)),
                   pl.BlockSpec((CMB_TB, d), lambda i: (i, 0))],
        out_shape=[jax.ShapeDtypeStruct((s, d), F32), jax.ShapeDtypeStruct((s, d), BF16)],
        scratch_shapes=[pltpu.VMEM((2, TOP_K, CMB_TB, d // LANES, LANES), BF16),
                        pltpu.SemaphoreType.DMA((2,))],
        compiler_params=_params(("arbitrary",)),
        name="combine",
    )(pos3, pos3, ys, h, gates, g.reshape(1, d))


def moe_routing(top_e, n_tokens):
    e_flat = top_e.reshape(-1)
    onehot = (e_flat[:, None] == jnp.arange(N_EXPERTS, dtype=jnp.int32)[None, :]).astype(jnp.int32)
    csum = jnp.cumsum(onehot, axis=0)
    counts = csum[-1]
    rank = jnp.sum(csum * onehot, axis=1) - 1
    padded = (counts + MOE_TM - 1) // MOE_TM * MOE_TM
    pends = jnp.cumsum(padded).astype(jnp.int32)
    pstarts = pends - padded
    pos = pstarts[e_flat] + rank
    n_blocks = (n_tokens * TOP_K) // MOE_TM + N_EXPERTS
    tok_flat = jnp.arange(n_tokens * TOP_K, dtype=jnp.int32) // TOP_K
    row_tok = jnp.zeros((n_blocks * MOE_TM,), jnp.int32).at[pos].set(tok_flat)
    n_used = pends[-1] // MOE_TM
    blk_start = jnp.arange(n_blocks, dtype=jnp.int32) * MOE_TM
    block_e = jnp.sum((pends[None, :] <= blk_start[:, None]).astype(jnp.int32), axis=1)
    block_e = jnp.minimum(block_e, N_EXPERTS - 1)
    last_e = block_e[jnp.maximum(n_used - 1, 0)]
    used = jnp.arange(n_blocks) < n_used
    real_end = pstarts + counts
    block_valid = jnp.clip(real_end[block_e] - blk_start, 0, MOE_TM)
    block_valid = jnp.where(used, block_valid, 0).astype(jnp.int32)
    block_e = jnp.where(used, block_e, last_e).astype(jnp.int32)
    return pos.astype(jnp.int32), row_tok, block_e, block_valid, n_used.reshape(1).astype(jnp.int32)


def kernel(x, p, g_mix, w_in, attn_sink, na_rpb, w_branch_a, w_branch_b, w_out, g_ffn, w_router,
           b_router, w_gate_up, b_gate_up, w_down, b_down, g_ple, w_ple_gate, w_ple, g_final):
    b, s, d = x.shape
    assert b == 1 and w_in.shape[0] == 1
    h = x.reshape(s, d)
    rope_c, rope_s1, rope_s2 = rope_tables(s)

    xn = rmsnorm_call(h, g_mix[0], BF16)
    proj = inproj_call(xn, w_in[0], rope_c, rope_s1, rope_s2, d)
    ya = window_attn_call(proj, attn_sink[0])
    yb = na_attn_call(proj, na_bias_tables(na_rpb[0]))
    merged = merge_call(ya, yb, w_branch_a[0], w_branch_b[0], proj, d)
    h = resmm_call(merged, w_out[0], h)

    xn2, top_e, gates = rms_router_call(h, g_ffn[0], w_router[0], b_router[0])
    pos, row_tok, block_e, block_valid, n_used = moe_routing(top_e[:, :TOP_K], s)
    xs = dispatch_call(n_used, row_tok, xn2)
    act = gateup_call(block_e, block_valid, n_used, xs, w_gate_up[0], b_gate_up[0])
    ys = down_call(block_e, block_valid, n_used, act, w_down[0], b_down[0])
    h, xn3 = combine_call(pos, ys, h, gates, g_ple[0])

    h = ple_call(xn3, w_ple_gate[0], p[0].reshape(s, -1), w_ple[0], h)
    out = rmsnorm_call(h, g_final, F32)
    return out.reshape(b, s, d)
```

```python
import functools

import numpy as np
import jax
import jax.numpy as jnp
from jax import lax
from jax.experimental import pallas as pl
from jax.experimental.pallas import tpu as pltpu

F32 = jnp.float32
BF16 = jnp.bfloat16

HEAD_DIM = 64
A_HEADS = 32
A_KV_HEADS = 8
A_GROUP = A_HEADS // A_KV_HEADS
B_HEADS = 32
A_Q_W = A_HEADS * HEAD_DIM
A_KV_W = A_KV_HEADS * HEAD_DIM
B_W = B_HEADS * HEAD_DIM
ROT_DIM = HEAD_DIM // 4
ROPE_THETA = 500000.0
WINDOW = 128
GRID_W = 64
NA_ROWS = 8
NA_COLS = 16
N_EXPERTS = 32
TOP_K = 4
SWIGLU_LIMIT = 7.0
SWIGLU_ALPHA = 1.702
RMS_EPS = 1e-6
NEG_INF = -1e30

LANES = 128
VMEM_LIMIT = 56 * 1024 * 1024

RMS_ROWS = 256
MM_TM = 1024
MM_TN = 512
ATT_BLK = 128
NA_RB = 8
NA_HPS = 4
MOE_TM = 512
MOE_TF = 256
MOE_TN = 2048
CMB_TB = 128
DMA_UNROLL = 8
ISSUE_STRIDE = 67


def _params(sem):
    return pltpu.CompilerParams(dimension_semantics=sem, vmem_limit_bytes=VMEM_LIMIT)


def _rms_body(x_ref, g_ref):
    x = x_ref[...].astype(F32)
    ms = jnp.mean(x * x, axis=-1, keepdims=True)
    return x * lax.rsqrt(ms + RMS_EPS) * g_ref[...]


def _rms_kernel(x_ref, g_ref, o_ref):
    o_ref[...] = _rms_body(x_ref, g_ref).astype(o_ref.dtype)


def rmsnorm_call(x, g, out_dtype):
    s, d = x.shape
    return pl.pallas_call(
        _rms_kernel,
        grid=(s // RMS_ROWS,),
        in_specs=[pl.BlockSpec((RMS_ROWS, d), lambda i: (i, 0)),
                  pl.BlockSpec((1, d), lambda i: (0, 0))],
        out_specs=pl.BlockSpec((RMS_ROWS, d), lambda i: (i, 0)),
        out_shape=jax.ShapeDtypeStruct((s, d), out_dtype),
        compiler_params=_params(("parallel",)),
        name="rmsnorm",
    )(x, g.reshape(1, d))


def _to_token_rows(x2d):
    return x2d.reshape(x2d.shape[0], x2d.shape[1] // LANES, LANES)


def _split_bf16(x):
    hi = x.astype(BF16)
    return hi, (x - hi.astype(F32)).astype(BF16)


def _rms_router_kernel(x_ref, g_ref, wh_ref, wl_ref, br_ref, xn_ref, e_ref, gt_ref):
    xn = _rms_body(x_ref, g_ref)
    xh, xl = _split_bf16(xn)
    xn_ref[...] = _to_token_rows(xh)
    logits = (jnp.dot(xh, wh_ref[...], preferred_element_type=F32)
              + jnp.dot(xh, wl_ref[...], preferred_element_type=F32)
              + jnp.dot(xl, wh_ref[...], preferred_element_type=F32)) + br_ref[...]
    lane = lax.broadcasted_iota(jnp.int32, logits.shape, 1)
    logits = jnp.where(lane < N_EXPERTS, logits, -jnp.inf)
    e_out = jnp.zeros(logits.shape, jnp.int32)
    v_out = jnp.zeros(logits.shape, F32)
    v0 = None
    for k in range(TOP_K):
        m = jnp.max(logits, axis=-1, keepdims=True)
        idx = jnp.min(jnp.where(logits == m, lane, LANES), axis=-1, keepdims=True)
        if k == 0:
            v0 = m
        e_out = jnp.where(lane == k, idx, e_out)
        v_out = jnp.where(lane == k, jnp.exp(m - v0), v_out)
        logits = jnp.where(lane == idx, -jnp.inf, logits)
    e_ref[...] = e_out
    gt_ref[...] = v_out / jnp.sum(v_out, axis=-1, keepdims=True)


def rms_router_call(h, g, w_router, b_router):
    s, d = h.shape
    wr = jnp.zeros((d, LANES), F32).at[:, :N_EXPERTS].set(w_router)
    br = jnp.zeros((1, LANES), F32).at[0, :N_EXPERTS].set(b_router)
    wr_hi, wr_lo = _split_bf16(wr)
    return pl.pallas_call(
        _rms_router_kernel,
        grid=(s // RMS_ROWS,),
        in_specs=[pl.BlockSpec((RMS_ROWS, d), lambda i: (i, 0)),
                  pl.BlockSpec((1, d), lambda i: (0, 0)),
                  pl.BlockSpec((d, LANES), lambda i: (0, 0)),
                  pl.BlockSpec((d, LANES), lambda i: (0, 0)),
                  pl.BlockSpec((1, LANES), lambda i: (0, 0))],
        out_specs=[pl.BlockSpec((RMS_ROWS, d // LANES, LANES), lambda i: (i, 0, 0)),
                   pl.BlockSpec((RMS_ROWS, LANES), lambda i: (i, 0)),
                   pl.BlockSpec((RMS_ROWS, LANES), lambda i: (i, 0))],
        out_shape=[jax.ShapeDtypeStruct((s, d // LANES, LANES), BF16),
                   jax.ShapeDtypeStruct((s, LANES), jnp.int32),
                   jax.ShapeDtypeStruct((s, LANES), F32)],
        compiler_params=_params(("parallel",)),
        name="rms_router",
    )(h, g.reshape(1, d), wr_hi, wr_lo, br)


def _rope(acc, c_ref, s1_ref, s2_ref):
    c, s1, s2 = c_ref[...], s1_ref[...], s2_ref[...]
    half = ROT_DIM // 2
    outs = []
    for t in range(acc.shape[1] // LANES):
        x = acc[:, t * LANES:(t + 1) * LANES]
        outs.append(x * c + pltpu.roll(x, LANES - half, 1) * s1 + pltpu.roll(x, half, 1) * s2)
    return jnp.concatenate(outs, axis=1)


def _stream_weight_chunk(w_hbm, wbuf, sem, wb_ref):
    j = pl.program_id(0)
    nj = pl.num_programs(0)
    tn = wbuf.shape[2]

    def chunk_copy(jj, slot):
        return pltpu.make_async_copy(w_hbm.at[:, pl.ds(pl.multiple_of(jj * tn, tn), tn)],
                                     wbuf.at[slot], sem.at[slot])

    @pl.when(pl.program_id(1) == 0)
    def _():
        slot = j % 2

        @pl.when(j == 0)
        def _():
            chunk_copy(0, 0).start()

        chunk_copy(j, slot).wait()

        @pl.when(j + 1 < nj)
        def _():
            chunk_copy(j + 1, 1 - slot).start()

        wb_ref[...] = wbuf[slot].astype(BF16)


def _inproj_kernel(a_ref, w_hbm, c_ref, s1_ref, s2_ref, o_ref, wb_ref, wbuf, wsem, *, n_rope, n_plain):
    j = pl.program_id(0)
    _stream_weight_chunk(w_hbm, wbuf, wsem, wb_ref)
    acc = jnp.dot(a_ref[...], wb_ref[...], preferred_element_type=F32)

    @pl.when(j < n_rope)
    def _():
        o_ref[...] = _rope(acc, c_ref, s1_ref, s2_ref).astype(o_ref.dtype)

    @pl.when(jnp.logical_and(j >= n_rope, j < n_plain))
    def _():
        o_ref[...] = acc.astype(o_ref.dtype)

    @pl.when(j >= n_plain)
    def _():
        o_ref[...] = jax.nn.sigmoid(acc).astype(o_ref.dtype)


def inproj_call(xn, w_in, rope_c, rope_s1, rope_s2, d_model):
    s, d = xn.shape
    n = w_in.shape[1]
    n_rope = (A_Q_W + A_KV_W) // MM_TN
    n_plain = (n - 2 * d_model) // MM_TN
    tab = pl.BlockSpec((MM_TM, LANES), lambda j, i: (i, 0))
    return pl.pallas_call(
        functools.partial(_inproj_kernel, n_rope=n_rope, n_plain=n_plain),
        grid=(n // MM_TN, s // MM_TM),
        in_specs=[pl.BlockSpec((MM_TM, d), lambda j, i: (i, 0)),
                  pl.BlockSpec(memory_space=pl.ANY),
                  tab, tab, tab],
        out_specs=pl.BlockSpec((MM_TM, MM_TN), lambda j, i: (i, j)),
        out_shape=jax.ShapeDtypeStruct((s, n), BF16),
        scratch_shapes=[pltpu.VMEM((d, MM_TN), BF16), pltpu.VMEM((2, d, MM_TN), F32),
                        pltpu.SemaphoreType.DMA((2,))],
        compiler_params=_params(("arbitrary", "arbitrary")),
        name="inproj",
    )(xn, w_in, rope_c, rope_s1, rope_s2)


def rope_tables(s):
    half = ROT_DIM // 2
    inv = jnp.float32(ROPE_THETA) ** (-jnp.arange(half, dtype=F32) * (2.0 / ROT_DIM))
    ang = jnp.arange(s, dtype=F32)[:, None] * inv[None, :]
    cos, sin = jnp.cos(ang), jnp.sin(ang)
    ones = jnp.ones((s, HEAD_DIM - ROT_DIM), F32)
    zeros = jnp.zeros((s, HEAD_DIM - ROT_DIM), F32)
    zh = jnp.zeros((s, half), F32)
    c = jnp.concatenate([cos, cos, ones], axis=1)
    s1 = jnp.concatenate([-sin, zh, zeros], axis=1)
    s2 = jnp.concatenate([zh, sin, zeros], axis=1)
    rep = LANES // HEAD_DIM
    return jnp.tile(c, (1, rep)), jnp.tile(s1, (1, rep)), jnp.tile(s2, (1, rep))


def _win_kernel(sink_ref, q_ref, kp_ref, kc_ref, kn_ref, vp_ref, vc_ref, vn_ref, o_ref, *, seq):
    n = pl.program_id(0)
    pr = pl.program_id(1)
    kcat = jnp.concatenate([kp_ref[...], kc_ref[...], kn_ref[...]], axis=0)
    vcat = jnp.concatenate([vp_ref[...], vc_ref[...], vn_ref[...]], axis=0)
    q = q_ref[...] * jnp.asarray(HEAD_DIM ** -0.5, BF16)
    rows = A_GROUP * ATT_BLK
    nk = 3 * ATT_BLK
    qpos = n * ATT_BLK + lax.broadcasted_iota(jnp.int32, (ATT_BLK, nk), 0)
    kpos = (n - 1) * ATT_BLK + lax.broadcasted_iota(jnp.int32, (ATT_BLK, nk), 1)
    mask = ((jnp.abs(qpos - kpos) <= WINDOW) & (kpos >= 0) & (kpos < seq))[None]
    grp = lax.broadcasted_iota(jnp.int32, (A_GROUP, 1, 1), 0)
    kv_per_blk = LANES // HEAD_DIM
    for kvh in range(kv_per_blk):
        k_h = kcat[:, kvh * HEAD_DIM:(kvh + 1) * HEAD_DIM]
        v_h = vcat[:, kvh * HEAD_DIM:(kvh + 1) * HEAD_DIM]
        h0 = kvh * A_GROUP
        qs = jnp.concatenate(
            [q[:, (h0 + g) * HEAD_DIM:(h0 + g + 1) * HEAD_DIM] for g in range(A_GROUP)], axis=0)
        sc = lax.dot_general(qs, k_h, (((1,), (1,)), ((), ())), preferred_element_type=F32)
        sc = jnp.where(mask, sc.reshape(A_GROUP, ATT_BLK, nk), NEG_INF)
        snk = jnp.zeros((A_GROUP, 1, 1), F32)
        for g in range(A_GROUP):
            sv = sink_ref[(pr * kv_per_blk + kvh) * A_GROUP + g]
            snk = jnp.where(grp == g, sv, snk)
        m = jnp.maximum(jnp.max(sc, axis=-1, keepdims=True), snk)
        p = jnp.exp(sc - m)
        denom = jnp.sum(p, axis=-1, keepdims=True) + jnp.exp(snk - m)
        attn = (p * (1.0 / denom)).astype(BF16).reshape(rows, nk)
        out = jnp.dot(attn, v_h, preferred_element_type=F32)
        for g in range(A_GROUP):
            o_ref[:, (h0 + g) * HEAD_DIM:(h0 + g + 1) * HEAD_DIM] = (
                out[g * ATT_BLK:(g + 1) * ATT_BLK].astype(o_ref.dtype))


def window_attn_call(proj, sink):
    s = proj.shape[0]
    nb = s // ATT_BLK
    qw = A_GROUP * LANES
    k0 = A_Q_W // LANES
    v0 = (A_Q_W + A_KV_W) // LANES
    blk = (ATT_BLK, LANES)
    prev = lambda n: jnp.maximum(n - 1, 0)
    nxt = lambda n: jnp.minimum(n + 1, nb - 1)
    return pl.pallas_call(
        functools.partial(_win_kernel, seq=s),
        grid=(nb, A_Q_W // qw),
        in_specs=[pl.BlockSpec(memory_space=pltpu.SMEM),
                  pl.BlockSpec((ATT_BLK, qw), lambda n, p: (n, p)),
                  pl.BlockSpec(blk, lambda n, p: (prev(n), k0 + p)),
                  pl.BlockSpec(blk, lambda n, p: (n, k0 + p)),
                  pl.BlockSpec(blk, lambda n, p: (nxt(n), k0 + p)),
                  pl.BlockSpec(blk, lambda n, p: (prev(n), v0 + p)),
                  pl.BlockSpec(blk, lambda n, p: (n, v0 + p)),
                  pl.BlockSpec(blk, lambda n, p: (nxt(n), v0 + p))],
        out_specs=pl.BlockSpec((ATT_BLK, qw), lambda n, p: (n, p)),
        out_shape=jax.ShapeDtypeStruct((s, A_Q_W), BF16),
        compiler_params=_params(("parallel", "parallel")),
        name="window_attn",
    )(sink, proj, proj, proj, proj, proj, proj, proj)


def na_bias_tables(rpb):
    c = np.arange(GRID_W)
    kc = np.arange(GRID_W)
    dc = np.clip(kc[None, :] - c[:, None] + NA_COLS - 1, 0, 2 * NA_COLS - 2)
    qcs = np.clip(c - NA_COLS // 2, 0, GRID_W - NA_COLS)
    cmask = (kc[None, :] >= qcs[:, None]) & (kc[None, :] < qcs[:, None] + NA_COLS)
    t = jnp.where(jnp.asarray(cmask)[None, None], rpb[:, :, dc].astype(F32), NEG_INF)
    t = jnp.transpose(t, (0, 2, 1, 3)).reshape(rpb.shape[0], GRID_W, (2 * NA_ROWS - 1) * GRID_W)
    return jnp.pad(t, ((0, 0), (0, 0), (GRID_W, 0)))


def _na_kernel(q_ref, kp_ref, kc_ref, kn_ref, vp_ref, vc_ref, vn_ref, b_ref, o_ref,
               kbuf, vbuf, s_scr, p_scr, *, grid_rows):
    rb = pl.program_id(1)
    blk = NA_RB * GRID_W
    kbuf[0:blk] = kp_ref[...]
    kbuf[blk:2 * blk] = kc_ref[...]
    kbuf[2 * blk:3 * blk] = kn_ref[...]
    vbuf[0:blk] = vp_ref[...]
    vbuf[blk:2 * blk] = vc_ref[...]
    vbuf[2 * blk:3 * blk] = vn_ref[...]
    nkeys = NA_ROWS * GRID_W
    width = NA_HPS * HEAD_DIM
    grp = NA_HPS * GRID_W
    row_head = lax.broadcasted_iota(jnp.int32, (grp, width), 0) // GRID_W
    lane_head = lax.broadcasted_iota(jnp.int32, (grp, width), 1) // HEAD_DIM
    own = row_head == lane_head
    out_head = lax.broadcasted_iota(jnp.int32, (GRID_W, width), 1) // HEAD_DIM
    nrb = grid_rows // NA_RB
    half = NA_ROWS // 2

    def rows(first_rb):
        starts = []
        for i in range(NA_RB):
            r = first_rb * NA_RB + i
            rs = min(max(r - half, 0), grid_rows - NA_ROWS)
            typ = rs - (r - half) + (half - 1)
            start = (rs - (first_rb - 1) * NA_RB) * GRID_W
            starts.append(start)
            qi = q_ref[i * GRID_W:(i + 1) * GRID_W, :] * jnp.asarray(HEAD_DIM ** -0.5, BF16)
            qs = jnp.where(own, jnp.concatenate([qi] * NA_HPS, axis=0), jnp.zeros((grp, width), BF16))
            sc = lax.dot_general(qs, kbuf[start:start + nkeys, :], (((1,), (1,)), ((), ())),
                                 preferred_element_type=F32)
            boff = (typ + 1) * GRID_W
            s_scr[i * grp:(i + 1) * grp, :] = sc + b_ref[:, :, boff:boff + nkeys].reshape(grp, nkeys)
        for i in range(NA_RB):
            sc = s_scr[i * grp:(i + 1) * grp, :]
            m = jnp.max(sc, axis=-1, keepdims=True)
            p = jnp.exp(sc - m)
            p_scr[i * grp:(i + 1) * grp, :] = (
                p * (1.0 / jnp.sum(p, axis=-1, keepdims=True))).astype(BF16)
        for i in range(NA_RB):
            start = starts[i]
            out = jnp.dot(p_scr[i * grp:(i + 1) * grp, :], vbuf[start:start + nkeys, :],
                          preferred_element_type=F32)
            res = out[:GRID_W]
            for h in range(1, NA_HPS):
                res = jnp.where(out_head == h, out[h * GRID_W:(h + 1) * GRID_W], res)
            o_ref[i * GRID_W:(i + 1) * GRID_W, :] = res.astype(o_ref.dtype)

    @pl.when(rb == 0)
    def _():
        rows(0)

    @pl.when(rb == nrb - 1)
    def _():
        rows(nrb - 1)

    @pl.when(jnp.logical_and(rb > 0, rb < nrb - 1))
    def _():
        rows(1)


def na_attn_call(proj, bias_tab):
    s = proj.shape[0]
    grid_rows = s // GRID_W
    nrb = grid_rows // NA_RB
    blk_rows = NA_RB * GRID_W
    width = NA_HPS * HEAD_DIM
    q0 = (A_Q_W + 2 * A_KV_W) // width
    k0 = q0 + B_W // width
    v0 = k0 + B_W // width
    blk = (blk_rows, width)
    hpb = NA_HPS
    prev = lambda r: jnp.maximum(r - 1, 0)
    cur = lambda r: r
    nxt = lambda r: jnp.minimum(r + 1, nrb - 1)
    return pl.pallas_call(
        functools.partial(_na_kernel, grid_rows=grid_rows),
        grid=(B_HEADS // hpb, nrb),
        in_specs=[pl.BlockSpec(blk, lambda h, r: (r, q0 + h)),
                  pl.BlockSpec(blk, lambda h, r: (prev(r), k0 + h)),
                  pl.BlockSpec(blk, lambda h, r: (cur(r), k0 + h)),
                  pl.BlockSpec(blk, lambda h, r: (nxt(r), k0 + h)),
                  pl.BlockSpec(blk, lambda h, r: (prev(r), v0 + h)),
                  pl.BlockSpec(blk, lambda h, r: (cur(r), v0 + h)),
                  pl.BlockSpec(blk, lambda h, r: (nxt(r), v0 + h)),
                  pl.BlockSpec((hpb, GRID_W, 2 * NA_ROWS * GRID_W), lambda h, r: (h, 0, 0))],
        out_specs=pl.BlockSpec(blk, lambda h, r: (r, h)),
        out_shape=jax.ShapeDtypeStruct((s, B_W), BF16),
        scratch_shapes=[pltpu.VMEM((3 * blk_rows, width), BF16),
                        pltpu.VMEM((3 * blk_rows, width), BF16),
                        pltpu.VMEM((NA_RB * hpb * GRID_W, NA_ROWS * GRID_W), F32),
                        pltpu.VMEM((NA_RB * hpb * GRID_W, NA_ROWS * GRID_W), BF16)],
        compiler_params=_params(("parallel", "arbitrary")),
        name="na_attn",
    )(proj, proj, proj, proj, proj, proj, proj, bias_tab)


def _merge_kernel(ya_ref, yb_ref, wa_ref, wb_ref, ga_ref, gb_ref, o_ref, wab, wbb):
    @pl.when(pl.program_id(1) == 0)
    def _():
        wab[...] = wa_ref[...].astype(BF16)
        wbb[...] = wb_ref[...].astype(BF16)

    a = jnp.dot(ya_ref[...], wab[...], preferred_element_type=F32)
    b = jnp.dot(yb_ref[...], wbb[...], preferred_element_type=F32)
    o_ref[...] = (ga_ref[...].astype(F32) * a + gb_ref[...].astype(F32) * b).astype(o_ref.dtype)


def merge_call(ya, yb, w_a, w_b, proj, d_model):
    s = ya.shape[0]
    ga0 = (proj.shape[1] - 2 * d_model) // MM_TN
    gb0 = ga0 + d_model // MM_TN
    return pl.pallas_call(
        _merge_kernel,
        grid=(d_model // MM_TN, s // MM_TM),
        in_specs=[pl.BlockSpec((MM_TM, A_Q_W), lambda j, i: (i, 0)),
                  pl.BlockSpec((MM_TM, B_W), lambda j, i: (i, 0)),
                  pl.BlockSpec((A_Q_W, MM_TN), lambda j, i: (0, j)),
                  pl.BlockSpec((B_W, MM_TN), lambda j, i: (0, j)),
                  pl.BlockSpec((MM_TM, MM_TN), lambda j, i: (i, ga0 + j)),
                  pl.BlockSpec((MM_TM, MM_TN), lambda j, i: (i, gb0 + j))],
        out_specs=pl.BlockSpec((MM_TM, MM_TN), lambda j, i: (i, j)),
        out_shape=jax.ShapeDtypeStruct((s, d_model), BF16),
        scratch_shapes=[pltpu.VMEM((A_Q_W, MM_TN), BF16), pltpu.VMEM((B_W, MM_TN), BF16)],
        compiler_params=_params(("parallel", "arbitrary")),
        name="merge",
    )(ya, yb, w_a, w_b, proj, proj)


def _resmm_kernel(a_ref, w_ref, r_ref, o_ref, wb_ref):
    @pl.when(pl.program_id(1) == 0)
    def _():
        wb_ref[...] = w_ref[...].astype(BF16)

    o_ref[...] = r_ref[...] + jnp.dot(a_ref[...], wb_ref[...], preferred_element_type=F32)


def resmm_call(a, w, res):
    s, k = a.shape
    n = w.shape[1]
    return pl.pallas_call(
        _resmm_kernel,
        grid=(n // MM_TN, s // MM_TM),
        in_specs=[pl.BlockSpec((MM_TM, k), lambda j, i: (i, 0)),
                  pl.BlockSpec((k, MM_TN), lambda j, i: (0, j)),
                  pl.BlockSpec((MM_TM, MM_TN), lambda j, i: (i, j))],
        out_specs=pl.BlockSpec((MM_TM, MM_TN), lambda j, i: (i, j)),
        out_shape=jax.ShapeDtypeStruct((s, n), F32),
        scratch_shapes=[pltpu.VMEM((k, MM_TN), BF16)],
        compiler_params=_params(("parallel", "arbitrary")),
        name="resmm",
    )(a, w, res)


def _ple_kernel(a_ref, wg_ref, p_ref, wp_ref, h_ref, o_ref, wgb, wpb):
    @pl.when(pl.program_id(1) == 0)
    def _():
        wgb[...] = wg_ref[...].astype(BF16)
        wpb[...] = wp_ref[...].astype(BF16)

    gate = jax.nn.sigmoid(jnp.dot(a_ref[...], wgb[...], preferred_element_type=F32))
    emb = jnp.dot(p_ref[...].astype(BF16), wpb[...], preferred_element_type=F32)
    o_ref[...] = h_ref[...] + gate * emb


def ple_call(xn, w_gate, p, w_ple, h):
    s, d = xn.shape
    pd = p.shape[1]
    return pl.pallas_call(
        _ple_kernel,
        grid=(d // MM_TN, s // MM_TM),
        in_specs=[pl.BlockSpec((MM_TM, d), lambda j, i: (i, 0)),
                  pl.BlockSpec((d, MM_TN), lambda j, i: (0, j)),
                  pl.BlockSpec((MM_TM, pd), lambda j, i: (i, 0)),
                  pl.BlockSpec((pd, MM_TN), lambda j, i: (0, j)),
                  pl.BlockSpec((MM_TM, MM_TN), lambda j, i: (i, j))],
        out_specs=pl.BlockSpec((MM_TM, MM_TN), lambda j, i: (i, j)),
        out_shape=jax.ShapeDtypeStruct((s, d), F32),
        scratch_shapes=[pltpu.VMEM((d, MM_TN), BF16), pltpu.VMEM((pd, MM_TN), BF16)],
        compiler_params=_params(("parallel", "arbitrary")),
        name="ple",
    )(xn, w_gate, p, w_ple, h)


def _dispatch_kernel(nused_ref, tok_ref, tokn_ref, x_hbm, o_ref, buf, sem):
    m = pl.program_id(0)
    nused = nused_ref[0]
    slot = m % 2

    def row_copy(tok, sl, r):
        return pltpu.make_async_copy(x_hbm.at[tok], buf.at[sl, r], sem.at[sl])

    def issue(t_ref, sl):
        def body(g, c):
            rows = [((g * DMA_UNROLL + u) * ISSUE_STRIDE) % MOE_TM for u in range(DMA_UNROLL)]
            toks = [t_ref[0, 0, r] for r in rows]
            for u in range(DMA_UNROLL):
                row_copy(toks[u], sl, rows[u]).start(priority=u % 2)
            return c

        lax.fori_loop(0, MOE_TM // DMA_UNROLL, body, 0)

    @pl.when(m == 0)
    def _():
        issue(tok_ref, 0)

    @pl.when(m + 1 < nused)
    def _():
        issue(tokn_ref, 1 - slot)

    @pl.when(m < nused)
    def _():
        pltpu.make_async_copy(x_hbm.at[pl.ds(0, MOE_TM)], buf.at[slot], sem.at[slot]).wait()
        o_ref[...] = buf[slot].reshape(o_ref.shape)

    @pl.when(m >= nused)
    def _():
        o_ref[...] = jnp.zeros(o_ref.shape, o_ref.dtype)


def dispatch_call(n_used, row_tok, xn3d):
    s, dg, _ = xn3d.shape
    d = dg * LANES
    n_rows = row_tok.shape[0]
    nblk = n_rows // MOE_TM
    tok3 = row_tok.reshape(nblk, 1, MOE_TM)
    tok_spec = lambda f: pl.BlockSpec((1, 1, MOE_TM), f, memory_space=pltpu.SMEM)
    return pl.pallas_call(
        _dispatch_kernel,
        grid_spec=pltpu.PrefetchScalarGridSpec(
            num_scalar_prefetch=1,
            grid=(nblk,),
            in_specs=[tok_spec(lambda m, nu: (m, 0, 0)),
                      tok_spec(lambda m, nu: (jnp.minimum(m + 1, nblk - 1), 0, 0)),
                      pl.BlockSpec(memory_space=pl.ANY)],
            out_specs=pl.BlockSpec((MOE_TM, d), lambda m, nu: (m, 0)),
            scratch_shapes=[pltpu.VMEM((2, MOE_TM, dg, LANES), BF16), pltpu.SemaphoreType.DMA((2,))]),
        out_shape=jax.ShapeDtypeStruct((n_rows, d), BF16),
        compiler_params=_params(("arbitrary",)),
        name="dispatch",
    )(n_used, tok3, tok3, xn3d)


def _expert_changed(be_ref, m):
    return jnp.logical_or(m == 0, be_ref[m] != be_ref[jnp.maximum(m - 1, 0)])


def _for_live_block(bv_ref, m, o_ref, compute):
    live = bv_ref[m] > 0

    @pl.when(live)
    def _():
        compute(slice(None))

    @pl.when(jnp.logical_not(live))
    def _():
        o_ref[...] = jnp.zeros(o_ref.shape, o_ref.dtype)


def _gateup_kernel(be_ref, bv_ref, nused_ref, sw_ref, nx_ref, nsw_ref, x_ref, w_hbm, bgu_ref, o_ref,
                   wb, wbuf, wsem):
    j = pl.program_id(0)
    m = pl.program_id(1)
    nj = pl.num_programs(0)
    tn = wbuf.shape[2]

    def chunk_copy(e, jj, slot):
        return pltpu.make_async_copy(w_hbm.at[e, :, pl.ds(pl.multiple_of(jj * tn, tn), tn)],
                                     wbuf.at[slot], wsem.at[slot])

    @pl.when(jnp.logical_and(bv_ref[m] > 0, _expert_changed(be_ref, m)))
    def _():
        k = j * nsw_ref[0] + sw_ref[m]
        slot = k % 2

        @pl.when(k == 0)
        def _():
            chunk_copy(be_ref[0], 0, 0).start()

        chunk_copy(be_ref[m], j, slot).wait()
        more_experts = nx_ref[m] >= 0

        @pl.when(more_experts)
        def _():
            chunk_copy(nx_ref[m], j, 1 - slot).start()

        @pl.when(jnp.logical_and(jnp.logical_not(more_experts), j + 1 < nj))
        def _():
            chunk_copy(be_ref[0], j + 1, 1 - slot).start()

        wb[...] = wbuf[slot].astype(BF16)

    def compute_rows(rows):
        hgu = jnp.dot(x_ref[rows], wb[...], preferred_element_type=F32) + bgu_ref[...]
        gate = jnp.minimum(hgu, SWIGLU_LIMIT)
        up = jnp.clip(hgu, -SWIGLU_LIMIT, SWIGLU_LIMIT)
        up = pltpu.roll(up, 2 * MOE_TF - 1, 1)
        act = gate * jax.nn.sigmoid(gate * SWIGLU_ALPHA) * (up + 1.0)
        rr = lax.broadcasted_iota(jnp.int32, (2 * MOE_TF, MOE_TF), 0)
        cc = lax.broadcasted_iota(jnp.int32, (2 * MOE_TF, MOE_TF), 1)
        sel = (rr == 2 * cc).astype(BF16)
        o_ref[rows] = jnp.dot(act.astype(BF16), sel, preferred_element_type=F32).astype(o_ref.dtype)

    _for_live_block(bv_ref, m, o_ref, compute_rows)


def gateup_call(block_e, block_valid, n_used, switch_idx, next_e, n_switch, xs, w_gu, b_gu):
    n_rows, dh = xs.shape
    e, d, f2 = w_gu.shape
    f = f2 // 2
    nblk = n_rows // MOE_TM

    return pl.pallas_call(
        _gateup_kernel,
        grid_spec=pltpu.PrefetchScalarGridSpec(
            num_scalar_prefetch=6,
            grid=(f // MOE_TF, nblk),
            in_specs=[pl.BlockSpec((MOE_TM, dh), lambda j, m, be, bv, nu, *_: (jnp.minimum(m, nu[0] - 1), 0)),
                      pl.BlockSpec(memory_space=pl.ANY),
                      pl.BlockSpec((None, 1, 2 * MOE_TF), lambda j, m, be, *_: (be[m], 0, j))],
            out_specs=pl.BlockSpec((MOE_TM, MOE_TF), lambda j, m, *_: (m, j)),
            scratch_shapes=[pltpu.VMEM((d, 2 * MOE_TF), BF16), pltpu.VMEM((2, d, 2 * MOE_TF), F32),
                            pltpu.SemaphoreType.DMA((2,))]),
        out_shape=jax.ShapeDtypeStruct((n_rows, f), BF16),
        compiler_params=_params(("arbitrary", "arbitrary")),
        name="expert_gateup",
    )(block_e, block_valid, n_used, switch_idx, next_e, n_switch, xs, w_gu, b_gu.reshape(e, 1, f2))


def _down_kernel(be_ref, bv_ref, nused_ref, a_ref, wdn_ref, bdn_ref, o_ref, wb):
    m = pl.program_id(1)

    @pl.when(jnp.logical_and(bv_ref[m] > 0, _expert_changed(be_ref, m)))
    def _():
        wb[...] = wdn_ref[...].astype(BF16)

    def compute_rows(rows):
        out = jnp.dot(a_ref[rows], wb[...], preferred_element_type=F32) + bdn_ref[...]
        o_ref[rows] = _to_token_rows(out.astype(o_ref.dtype))

    _for_live_block(bv_ref, m, o_ref, compute_rows)


def down_call(block_e, block_valid, n_used, act, w_dn, b_dn):
    n_rows, f = act.shape
    e, _, d = w_dn.shape
    nblk = n_rows // MOE_TM

    def meff(m, nu):
        return jnp.minimum(m, nu[0] - 1)

    return pl.pallas_call(
        _down_kernel,
        grid_spec=pltpu.PrefetchScalarGridSpec(
            num_scalar_prefetch=3,
            grid=(d // MOE_TN, nblk),
            in_specs=[pl.BlockSpec((MOE_TM, f), lambda j, m, be, bv, nu: (meff(m, nu), 0)),
                      pl.BlockSpec((None, f, MOE_TN), lambda j, m, be, bv, nu: (be[m], 0, j)),
                      pl.BlockSpec((None, 1, MOE_TN), lambda j, m, be, bv, nu: (be[m], 0, j))],
            out_specs=pl.BlockSpec((MOE_TM, MOE_TN // LANES, LANES), lambda j, m, be, bv, nu: (m, j, 0)),
            scratch_shapes=[pltpu.VMEM((f, MOE_TN), BF16)]),
        out_shape=jax.ShapeDtypeStruct((n_rows, d // LANES, LANES), BF16),
        compiler_params=_params(("arbitrary", "arbitrary")),
        name="expert_down",
    )(block_e, block_valid, n_used, act, w_dn, b_dn.reshape(e, 1, d))


def _combine_kernel(pos_ref, posn_ref, ys_hbm, h_ref, gt_ref, g_ref, h_out, xn_out, buf, sem):
    i = pl.program_id(0)
    n = pl.num_programs(0)
    slot = i % 2

    def row_copy(row, sl, k, t):
        return pltpu.make_async_copy(ys_hbm.at[row], buf.at[sl, k, t], sem.at[sl])

    def issue(p_ref, sl):
        def body(g, c):
            rows = [p_ref[0, 0, g * DMA_UNROLL + u] for u in range(DMA_UNROLL)]
            for u in range(DMA_UNROLL):
                t = g * (DMA_UNROLL // TOP_K) + u // TOP_K
                row_copy(rows[u], sl, u % TOP_K, t).start(priority=u % 2)
            return c

        lax.fori_loop(0, CMB_TB * TOP_K // DMA_UNROLL, body, 0)

    @pl.when(i == 0)
    def _():
        issue(pos_ref, 0)

    @pl.when(i + 1 < n)
    def _():
        issue(posn_ref, 1 - slot)

    for k in range(TOP_K):
        pltpu.make_async_copy(ys_hbm.at[pl.ds(0, CMB_TB)], buf.at[slot, k], sem.at[slot]).wait()

    h = h_ref[...]
    for k in range(TOP_K):
        rows = buf[slot, k].reshape(h.shape).astype(F32)
        h = h + gt_ref[:, k:k + 1] * rows
    h_out[...] = h
    ms = jnp.mean(h * h, axis=-1, keepdims=True)
    xn_out[...] = (h * lax.rsqrt(ms + RMS_EPS) * g_ref[...]).astype(xn_out.dtype)


def combine_call(pos, ys, h, gates, g):
    s, d = h.shape
    nb = s // CMB_TB
    pos3 = pos.reshape(nb, 1, CMB_TB * TOP_K)
    pos_spec = lambda f: pl.BlockSpec((1, 1, CMB_TB * TOP_K), f, memory_space=pltpu.SMEM)
    return pl.pallas_call(
        _combine_kernel,
        grid=(nb,),
        in_specs=[pos_spec(lambda i: (i, 0, 0)),
                  pos_spec(lambda i: (jnp.minimum(i + 1, nb - 1), 0, 0)),
                  pl.BlockSpec(memory_space=pl.ANY),
                  pl.BlockSpec((CMB_TB, d), lambda i: (i, 0)),
                  pl.BlockSpec((CMB_TB, LANES), lambda i: (i, 0)),
                  pl.BlockSpec((1, d), lambda i: (0, 0))],
        out_specs=[pl.BlockSpec((CMB_TB, d), lambda i: (i, 0)),
                   pl.BlockSpec((CMB_TB, d), lambda i: (i, 0))],
        out_shape=[jax.ShapeDtypeStruct((s, d), F32), jax.ShapeDtypeStruct((s, d), BF16)],
        scratch_shapes=[pltpu.VMEM((2, TOP_K, CMB_TB, d // LANES, LANES), BF16),
                        pltpu.SemaphoreType.DMA((2,))],
        compiler_params=_params(("arbitrary",)),
        name="combine",
    )(pos3, pos3, ys, h, gates, g.reshape(1, d))


def moe_routing(top_e, n_tokens):
    e_flat = top_e.reshape(-1)
    onehot = (e_flat[:, None] == jnp.arange(N_EXPERTS, dtype=jnp.int32)[None, :]).astype(jnp.int32)
    csum = jnp.cumsum(onehot, axis=0)
    counts = csum[-1]
    rank = jnp.sum(csum * onehot, axis=1) - 1
    padded = (counts + MOE_TM - 1) // MOE_TM * MOE_TM
    pends = jnp.cumsum(padded).astype(jnp.int32)
    pstarts = pends - padded
    pos = pstarts[e_flat] + rank
    n_blocks = (n_tokens * TOP_K) // MOE_TM + N_EXPERTS
    tok_flat = jnp.arange(n_tokens * TOP_K, dtype=jnp.int32) // TOP_K
    row_tok = jnp.zeros((n_blocks * MOE_TM,), jnp.int32).at[pos].set(tok_flat)
    n_used = pends[-1] // MOE_TM
    blk_start = jnp.arange(n_blocks, dtype=jnp.int32) * MOE_TM
    block_e = jnp.sum((pends[None, :] <= blk_start[:, None]).astype(jnp.int32), axis=1)
    block_e = jnp.minimum(block_e, N_EXPERTS - 1)
    last_e = block_e[jnp.maximum(n_used - 1, 0)]
    used = jnp.arange(n_blocks) < n_used
    real_end = pstarts + counts
    block_valid = jnp.clip(real_end[block_e] - blk_start, 0, MOE_TM)
    block_valid = jnp.where(used, block_valid, 0).astype(jnp.int32)
    block_e = jnp.where(used, block_e, last_e).astype(jnp.int32)
    has_rows = counts > 0
    eidx = jnp.arange(N_EXPERTS, dtype=jnp.int32)
    later = jnp.logical_and(eidx[None, :] > eidx[:, None], has_rows[None, :])
    next_of = jnp.min(jnp.where(later, eidx[None, :], N_EXPERTS), axis=1)
    next_of = jnp.where(next_of < N_EXPERTS, next_of, -1)
    order_of = jnp.cumsum(has_rows.astype(jnp.int32)) - 1
    routing = dict(block_e=block_e, block_valid=block_valid,
                   n_used=n_used.reshape(1).astype(jnp.int32),
                   switch_idx=order_of[block_e].astype(jnp.int32),
                   next_e=next_of[block_e].astype(jnp.int32),
                   n_switch=jnp.sum(has_rows.astype(jnp.int32)).reshape(1))
    return pos.astype(jnp.int32), row_tok, routing


def kernel(x, p, g_mix, w_in, attn_sink, na_rpb, w_branch_a, w_branch_b, w_out, g_ffn, w_router,
           b_router, w_gate_up, b_gate_up, w_down, b_down, g_ple, w_ple_gate, w_ple, g_final):
    b, s, d = x.shape
    assert b == 1 and w_in.shape[0] == 1
    h = x.reshape(s, d)
    rope_c, rope_s1, rope_s2 = rope_tables(s)

    xn = rmsnorm_call(h, g_mix[0], BF16)
    proj = inproj_call(xn, w_in[0], rope_c, rope_s1, rope_s2, d)
    ya = window_attn_call(proj, attn_sink[0])
    yb = na_attn_call(proj, na_bias_tables(na_rpb[0]))
    merged = merge_call(ya, yb, w_branch_a[0], w_branch_b[0], proj, d)
    h = resmm_call(merged, w_out[0], h)

    xn2, top_e, gates = rms_router_call(h, g_ffn[0], w_router[0], b_router[0])
    pos, row_tok, rt = moe_routing(top_e[:, :TOP_K], s)
    xs = dispatch_call(rt["n_used"], row_tok, xn2)
    act = gateup_call(rt["block_e"], rt["block_valid"], rt["n_used"], rt["switch_idx"], rt["next_e"],
                      rt["n_switch"], xs, w_gate_up[0], b_gate_up[0])
    ys = down_call(rt["block_e"], rt["block_valid"], rt["n_used"], act, w_down[0], b_down[0])
    h, xn3 = combine_call(pos, ys, h, gates, g_ple[0])

    h = ple_call(xn3, w_ple_gate[0], p[0].reshape(s, -1), w_ple[0], h)
    out = rmsnorm_call(h, g_final, F32)
    return out.reshape(b, s, d)
```

```python
import functools

import numpy as np
import jax
import jax.numpy as jnp
from jax import lax
from jax.experimental import pallas as pl
from jax.experimental.pallas import tpu as pltpu

F32 = jnp.float32
BF16 = jnp.bfloat16

HEAD_DIM = 64
A_HEADS = 32
A_KV_HEADS = 8
A_GROUP = A_HEADS // A_KV_HEADS
B_HEADS = 32
A_Q_W = A_HEADS * HEAD_DIM
A_KV_W = A_KV_HEADS * HEAD_DIM
B_W = B_HEADS * HEAD_DIM
ROT_DIM = HEAD_DIM // 4
ROPE_THETA = 500000.0
WINDOW = 128
GRID_W = 64
NA_ROWS = 8
NA_COLS = 16
N_EXPERTS = 32
TOP_K = 4
SWIGLU_LIMIT = 7.0
SWIGLU_ALPHA = 1.702
RMS_EPS = 1e-6
NEG_INF = -1e30

LANES = 128
VMEM_LIMIT = 56 * 1024 * 1024

RMS_ROWS = 256
MM_TM = 1024
MM_TN = 512
ATT_BLK = 128
NA_RB = 8
NA_HPS = 4
MOE_TM = 512
MOE_TF = 256
MOE_TN = 2048
CMB_TB = 128
DMA_UNROLL = 8


def _params(sem):
    return pltpu.CompilerParams(dimension_semantics=sem, vmem_limit_bytes=VMEM_LIMIT)


def _rms_body(x_ref, g_ref):
    x = x_ref[...].astype(F32)
    ms = jnp.mean(x * x, axis=-1, keepdims=True)
    return x * lax.rsqrt(ms + RMS_EPS) * g_ref[...]


def _rms_kernel(x_ref, g_ref, o_ref):
    o_ref[...] = _rms_body(x_ref, g_ref).astype(o_ref.dtype)


def rmsnorm_call(x, g, out_dtype):
    s, d = x.shape
    return pl.pallas_call(
        _rms_kernel,
        grid=(s // RMS_ROWS,),
        in_specs=[pl.BlockSpec((RMS_ROWS, d), lambda i: (i, 0)),
                  pl.BlockSpec((1, d), lambda i: (0, 0))],
        out_specs=pl.BlockSpec((RMS_ROWS, d), lambda i: (i, 0)),
        out_shape=jax.ShapeDtypeStruct((s, d), out_dtype),
        compiler_params=_params(("parallel",)),
        name="rmsnorm",
    )(x, g.reshape(1, d))


def _to_token_rows(x2d):
    return x2d.reshape(x2d.shape[0], x2d.shape[1] // LANES, LANES)


def _split_bf16(x):
    hi = x.astype(BF16)
    return hi, (x - hi.astype(F32)).astype(BF16)


def _rms_router_kernel(x_ref, g_ref, wh_ref, wl_ref, br_ref, xn_ref, e_ref, gt_ref):
    xn = _rms_body(x_ref, g_ref)
    xh, xl = _split_bf16(xn)
    xn_ref[...] = _to_token_rows(xh)
    logits = (jnp.dot(xh, wh_ref[...], preferred_element_type=F32)
              + jnp.dot(xh, wl_ref[...], preferred_element_type=F32)
              + jnp.dot(xl, wh_ref[...], preferred_element_type=F32)) + br_ref[...]
    lane = lax.broadcasted_iota(jnp.int32, logits.shape, 1)
    logits = jnp.where(lane < N_EXPERTS, logits, -jnp.inf)
    e_out = jnp.zeros(logits.shape, jnp.int32)
    v_out = jnp.zeros(logits.shape, F32)
    v0 = None
    for k in range(TOP_K):
        m = jnp.max(logits, axis=-1, keepdims=True)
        idx = jnp.min(jnp.where(logits == m, lane, LANES), axis=-1, keepdims=True)
        if k == 0:
            v0 = m
        e_out = jnp.where(lane == k, idx, e_out)
        v_out = jnp.where(lane == k, jnp.exp(m - v0), v_out)
        logits = jnp.where(lane == idx, -jnp.inf, logits)
    e_ref[...] = e_out
    gt_ref[...] = v_out / jnp.sum(v_out, axis=-1, keepdims=True)


def rms_router_call(h, g, w_router, b_router):
    s, d = h.shape
    wr = jnp.zeros((d, LANES), F32).at[:, :N_EXPERTS].set(w_router)
    br = jnp.zeros((1, LANES), F32).at[0, :N_EXPERTS].set(b_router)
    wr_hi, wr_lo = _split_bf16(wr)
    return pl.pallas_call(
        _rms_router_kernel,
        grid=(s // RMS_ROWS,),
        in_specs=[pl.BlockSpec((RMS_ROWS, d), lambda i: (i, 0)),
                  pl.BlockSpec((1, d), lambda i: (0, 0)),
                  pl.BlockSpec((d, LANES), lambda i: (0, 0)),
                  pl.BlockSpec((d, LANES), lambda i: (0, 0)),
                  pl.BlockSpec((1, LANES), lambda i: (0, 0))],
        out_specs=[pl.BlockSpec((RMS_ROWS, d // LANES, LANES), lambda i: (i, 0, 0)),
                   pl.BlockSpec((RMS_ROWS, LANES), lambda i: (i, 0)),
                   pl.BlockSpec((RMS_ROWS, LANES), lambda i: (i, 0))],
        out_shape=[jax.ShapeDtypeStruct((s, d // LANES, LANES), BF16),
                   jax.ShapeDtypeStruct((s, LANES), jnp.int32),
                   jax.ShapeDtypeStruct((s, LANES), F32)],
        compiler_params=_params(("parallel",)),
        name="rms_router",
    )(h, g.reshape(1, d), wr_hi, wr_lo, br)


def _rope(acc, c_ref, s1_ref, s2_ref):
    c, s1, s2 = c_ref[...], s1_ref[...], s2_ref[...]
    half = ROT_DIM // 2
    outs = []
    for t in range(acc.shape[1] // LANES):
        x = acc[:, t * LANES:(t + 1) * LANES]
        outs.append(x * c + pltpu.roll(x, LANES - half, 1) * s1 + pltpu.roll(x, half, 1) * s2)
    return jnp.concatenate(outs, axis=1)


def _stream_weight_chunk(w_hbm, wbuf, sem, wb_ref):
    j = pl.program_id(0)
    nj = pl.num_programs(0)
    tn = wbuf.shape[2]

    def chunk_copy(jj, slot):
        return pltpu.make_async_copy(w_hbm.at[:, pl.ds(pl.multiple_of(jj * tn, tn), tn)],
                                     wbuf.at[slot], sem.at[slot])

    @pl.when(pl.program_id(1) == 0)
    def _():
        slot = j % 2

        @pl.when(j == 0)
        def _():
            chunk_copy(0, 0).start()

        chunk_copy(j, slot).wait()

        @pl.when(j + 1 < nj)
        def _():
            chunk_copy(j + 1, 1 - slot).start()

        wb_ref[...] = wbuf[slot].astype(BF16)


def _inproj_kernel(a_ref, w_hbm, c_ref, s1_ref, s2_ref, o_ref, wb_ref, wbuf, wsem, *, n_rope, n_plain):
    j = pl.program_id(0)
    _stream_weight_chunk(w_hbm, wbuf, wsem, wb_ref)
    acc = jnp.dot(a_ref[...], wb_ref[...], preferred_element_type=F32)

    @pl.when(j < n_rope)
    def _():
        o_ref[...] = _rope(acc, c_ref, s1_ref, s2_ref).astype(o_ref.dtype)

    @pl.when(jnp.logical_and(j >= n_rope, j < n_plain))
    def _():
        o_ref[...] = acc.astype(o_ref.dtype)

    @pl.when(j >= n_plain)
    def _():
        o_ref[...] = jax.nn.sigmoid(acc).astype(o_ref.dtype)


def inproj_call(xn, w_in, rope_c, rope_s1, rope_s2, d_model):
    s, d = xn.shape
    n = w_in.shape[1]
    n_rope = (A_Q_W + A_KV_W) // MM_TN
    n_plain = (n - 2 * d_model) // MM_TN
    tab = pl.BlockSpec((MM_TM, LANES), lambda j, i: (i, 0))
    return pl.pallas_call(
        functools.partial(_inproj_kernel, n_rope=n_rope, n_plain=n_plain),
        grid=(n // MM_TN, s // MM_TM),
        in_specs=[pl.BlockSpec((MM_TM, d), lambda j, i: (i, 0)),
                  pl.BlockSpec(memory_space=pl.ANY),
                  tab, tab, tab],
        out_specs=pl.BlockSpec((MM_TM, MM_TN), lambda j, i: (i, j)),
        out_shape=jax.ShapeDtypeStruct((s, n), BF16),
        scratch_shapes=[pltpu.VMEM((d, MM_TN), BF16), pltpu.VMEM((2, d, MM_TN), F32),
                        pltpu.SemaphoreType.DMA((2,))],
        compiler_params=_params(("arbitrary", "arbitrary")),
        name="inproj",
    )(xn, w_in, rope_c, rope_s1, rope_s2)


def rope_tables(s):
    half = ROT_DIM // 2
    inv = jnp.float32(ROPE_THETA) ** (-jnp.arange(half, dtype=F32) * (2.0 / ROT_DIM))
    ang = jnp.arange(s, dtype=F32)[:, None] * inv[None, :]
    cos, sin = jnp.cos(ang), jnp.sin(ang)
    ones = jnp.ones((s, HEAD_DIM - ROT_DIM), F32)
    zeros = jnp.zeros((s, HEAD_DIM - ROT_DIM), F32)
    zh = jnp.zeros((s, half), F32)
    c = jnp.concatenate([cos, cos, ones], axis=1)
    s1 = jnp.concatenate([-sin, zh, zeros], axis=1)
    s2 = jnp.concatenate([zh, sin, zeros], axis=1)
    rep = LANES // HEAD_DIM
    return jnp.tile(c, (1, rep)), jnp.tile(s1, (1, rep)), jnp.tile(s2, (1, rep))


def _win_kernel(sink_ref, q_ref, kp_ref, kc_ref, kn_ref, vp_ref, vc_ref, vn_ref, o_ref, *, seq):
    n = pl.program_id(0)
    pr = pl.program_id(1)
    kcat = jnp.concatenate([kp_ref[...], kc_ref[...], kn_ref[...]], axis=0)
    vcat = jnp.concatenate([vp_ref[...], vc_ref[...], vn_ref[...]], axis=0)
    q = q_ref[...] * jnp.asarray(HEAD_DIM ** -0.5, BF16)
    rows = A_GROUP * ATT_BLK
    nk = 3 * ATT_BLK
    qpos = n * ATT_BLK + lax.broadcasted_iota(jnp.int32, (ATT_BLK, nk), 0)
    kpos = (n - 1) * ATT_BLK + lax.broadcasted_iota(jnp.int32, (ATT_BLK, nk), 1)
    mask = ((jnp.abs(qpos - kpos) <= WINDOW) & (kpos >= 0) & (kpos < seq))[None]
    grp = lax.broadcasted_iota(jnp.int32, (A_GROUP, 1, 1), 0)
    kv_per_blk = LANES // HEAD_DIM
    for kvh in range(kv_per_blk):
        k_h = kcat[:, kvh * HEAD_DIM:(kvh + 1) * HEAD_DIM]
        v_h = vcat[:, kvh * HEAD_DIM:(kvh + 1) * HEAD_DIM]
        h0 = kvh * A_GROUP
        qs = jnp.concatenate(
            [q[:, (h0 + g) * HEAD_DIM:(h0 + g + 1) * HEAD_DIM] for g in range(A_GROUP)], axis=0)
        sc = lax.dot_general(qs, k_h, (((1,), (1,)), ((), ())), preferred_element_type=F32)
        sc = jnp.where(mask, sc.reshape(A_GROUP, ATT_BLK, nk), NEG_INF)
        snk = jnp.zeros((A_GROUP, 1, 1), F32)
        for g in range(A_GROUP):
            sv = sink_ref[(pr * kv_per_blk + kvh) * A_GROUP + g]
            snk = jnp.where(grp == g, sv, snk)
        m = jnp.maximum(jnp.max(sc, axis=-1, keepdims=True), snk)
        p = jnp.exp(sc - m)
        denom = jnp.sum(p, axis=-1, keepdims=True) + jnp.exp(snk - m)
        attn = (p * (1.0 / denom)).astype(BF16).reshape(rows, nk)
        out = jnp.dot(attn, v_h, preferred_element_type=F32)
        for g in range(A_GROUP):
            o_ref[:, (h0 + g) * HEAD_DIM:(h0 + g + 1) * HEAD_DIM] = (
                out[g * ATT_BLK:(g + 1) * ATT_BLK].astype(o_ref.dtype))


def window_attn_call(proj, sink):
    s = proj.shape[0]
    nb = s // ATT_BLK
    qw = A_GROUP * LANES
    k0 = A_Q_W // LANES
    v0 = (A_Q_W + A_KV_W) // LANES
    blk = (ATT_BLK, LANES)
    prev = lambda n: jnp.maximum(n - 1, 0)
    nxt = lambda n: jnp.minimum(n + 1, nb - 1)
    return pl.pallas_call(
        functools.partial(_win_kernel, seq=s),
        grid=(nb, A_Q_W // qw),
        in_specs=[pl.BlockSpec(memory_space=pltpu.SMEM),
                  pl.BlockSpec((ATT_BLK, qw), lambda n, p: (n, p)),
                  pl.BlockSpec(blk, lambda n, p: (prev(n), k0 + p)),
                  pl.BlockSpec(blk, lambda n, p: (n, k0 + p)),
                  pl.BlockSpec(blk, lambda n, p: (nxt(n), k0 + p)),
                  pl.BlockSpec(blk, lambda n, p: (prev(n), v0 + p)),
                  pl.BlockSpec(blk, lambda n, p: (n, v0 + p)),
                  pl.BlockSpec(blk, lambda n, p: (nxt(n), v0 + p))],
        out_specs=pl.BlockSpec((ATT_BLK, qw), lambda n, p: (n, p)),
        out_shape=jax.ShapeDtypeStruct((s, A_Q_W), BF16),
        compiler_params=_params(("parallel", "parallel")),
        name="window_attn",
    )(sink, proj, proj, proj, proj, proj, proj, proj)


def na_bias_tables(rpb):
    c = np.arange(GRID_W)
    kc = np.arange(GRID_W)
    dc = np.clip(kc[None, :] - c[:, None] + NA_COLS - 1, 0, 2 * NA_COLS - 2)
    qcs = np.clip(c - NA_COLS // 2, 0, GRID_W - NA_COLS)
    cmask = (kc[None, :] >= qcs[:, None]) & (kc[None, :] < qcs[:, None] + NA_COLS)
    t = jnp.where(jnp.asarray(cmask)[None, None], rpb[:, :, dc].astype(F32), NEG_INF)
    t = jnp.transpose(t, (0, 2, 1, 3)).reshape(rpb.shape[0], GRID_W, (2 * NA_ROWS - 1) * GRID_W)
    return jnp.pad(t, ((0, 0), (0, 0), (GRID_W, 0)))


def _na_kernel(q_ref, kp_ref, kc_ref, kn_ref, vp_ref, vc_ref, vn_ref, b_ref, o_ref,
               kbuf, vbuf, s_scr, p_scr, *, grid_rows):
    rb = pl.program_id(1)
    blk = NA_RB * GRID_W
    kbuf[0:blk] = kp_ref[...]
    kbuf[blk:2 * blk] = kc_ref[...]
    kbuf[2 * blk:3 * blk] = kn_ref[...]
    vbuf[0:blk] = vp_ref[...]
    vbuf[blk:2 * blk] = vc_ref[...]
    vbuf[2 * blk:3 * blk] = vn_ref[...]
    nkeys = NA_ROWS * GRID_W
    width = NA_HPS * HEAD_DIM
    grp = NA_HPS * GRID_W
    row_head = lax.broadcasted_iota(jnp.int32, (grp, width), 0) // GRID_W
    lane_head = lax.broadcasted_iota(jnp.int32, (grp, width), 1) // HEAD_DIM
    own = row_head == lane_head
    out_head = lax.broadcasted_iota(jnp.int32, (GRID_W, width), 1) // HEAD_DIM
    nrb = grid_rows // NA_RB
    half = NA_ROWS // 2

    def rows(first_rb):
        starts = []
        for i in range(NA_RB):
            r = first_rb * NA_RB + i
            rs = min(max(r - half, 0), grid_rows - NA_ROWS)
            typ = rs - (r - half) + (half - 1)
            start = (rs - (first_rb - 1) * NA_RB) * GRID_W
            starts.append(start)
            qi = q_ref[i * GRID_W:(i + 1) * GRID_W, :] * jnp.asarray(HEAD_DIM ** -0.5, BF16)
            qs = jnp.where(own, jnp.concatenate([qi] * NA_HPS, axis=0), jnp.zeros((grp, width), BF16))
            sc = lax.dot_general(qs, kbuf[start:start + nkeys, :], (((1,), (1,)), ((), ())),
                                 preferred_element_type=F32)
            boff = (typ + 1) * GRID_W
            s_scr[i * grp:(i + 1) * grp, :] = sc + b_ref[:, :, boff:boff + nkeys].reshape(grp, nkeys)
        for i in range(NA_RB):
            sc = s_scr[i * grp:(i + 1) * grp, :]
            m = jnp.max(sc, axis=-1, keepdims=True)
            p = jnp.exp(sc - m)
            p_scr[i * grp:(i + 1) * grp, :] = (
                p * (1.0 / jnp.sum(p, axis=-1, keepdims=True))).astype(BF16)
        for i in range(NA_RB):
            start = starts[i]
            out = jnp.dot(p_scr[i * grp:(i + 1) * grp, :], vbuf[start:start + nkeys, :],
                          preferred_element_type=F32)
            res = out[:GRID_W]
            for h in range(1, NA_HPS):
                res = jnp.where(out_head == h, out[h * GRID_W:(h + 1) * GRID_W], res)
            o_ref[i * GRID_W:(i + 1) * GRID_W, :] = res.astype(o_ref.dtype)

    @pl.when(rb == 0)
    def _():
        rows(0)

    @pl.when(rb == nrb - 1)
    def _():
        rows(nrb - 1)

    @pl.when(jnp.logical_and(rb > 0, rb < nrb - 1))
    def _():
        rows(1)


def na_attn_call(proj, bias_tab):
    s = proj.shape[0]
    grid_rows = s // GRID_W
    nrb = grid_rows // NA_RB
    blk_rows = NA_RB * GRID_W
    width = NA_HPS * HEAD_DIM
    q0 = (A_Q_W + 2 * A_KV_W) // width
    k0 = q0 + B_W // width
    v0 = k0 + B_W // width
    blk = (blk_rows, width)
    hpb = NA_HPS
    prev = lambda r: jnp.maximum(r - 1, 0)
    cur = lambda r: r
    nxt = lambda r: jnp.minimum(r + 1, nrb - 1)
    return pl.pallas_call(
        functools.partial(_na_kernel, grid_rows=grid_rows),
        grid=(B_HEADS // hpb, nrb),
        in_specs=[pl.BlockSpec(blk, lambda h, r: (r, q0 + h)),
                  pl.BlockSpec(blk, lambda h, r: (prev(r), k0 + h)),
                  pl.BlockSpec(blk, lambda h, r: (cur(r), k0 + h)),
                  pl.BlockSpec(blk, lambda h, r: (nxt(r), k0 + h)),
                  pl.BlockSpec(blk, lambda h, r: (prev(r), v0 + h)),
                  pl.BlockSpec(blk, lambda h, r: (cur(r), v0 + h)),
                  pl.BlockSpec(blk, lambda h, r: (nxt(r), v0 + h)),
                  pl.BlockSpec((hpb, GRID_W, 2 * NA_ROWS * GRID_W), lambda h, r: (h, 0, 0))],
        out_specs=pl.BlockSpec(blk, lambda h, r: (r, h)),
        out_shape=jax.ShapeDtypeStruct((s, B_W), BF16),
        scratch_shapes=[pltpu.VMEM((3 * blk_rows, width), BF16),
                        pltpu.VMEM((3 * blk_rows, width), BF16),
                        pltpu.VMEM((NA_RB * hpb * GRID_W, NA_ROWS * GRID_W), F32),
                        pltpu.VMEM((NA_RB * hpb * GRID_W, NA_ROWS * GRID_W), BF16)],
        compiler_params=_params(("parallel", "arbitrary")),
        name="na_attn",
    )(proj, proj, proj, proj, proj, proj, proj, bias_tab)


def _merge_kernel(ya_ref, yb_ref, wa_ref, wb_ref, ga_ref, gb_ref, o_ref, wab, wbb):
    @pl.when(pl.program_id(1) == 0)
    def _():
        wab[...] = wa_ref[...].astype(BF16)
        wbb[...] = wb_ref[...].astype(BF16)

    a = jnp.dot(ya_ref[...], wab[...], preferred_element_type=F32)
    b = jnp.dot(yb_ref[...], wbb[...], preferred_element_type=F32)
    o_ref[...] = (ga_ref[...].astype(F32) * a + gb_ref[...].astype(F32) * b).astype(o_ref.dtype)


def merge_call(ya, yb, w_a, w_b, proj, d_model):
    s = ya.shape[0]
    ga0 = (proj.shape[1] - 2 * d_model) // MM_TN
    gb0 = ga0 + d_model // MM_TN
    return pl.pallas_call(
        _merge_kernel,
        grid=(d_model // MM_TN, s // MM_TM),
        in_specs=[pl.BlockSpec((MM_TM, A_Q_W), lambda j, i: (i, 0)),
                  pl.BlockSpec((MM_TM, B_W), lambda j, i: (i, 0)),
                  pl.BlockSpec((A_Q_W, MM_TN), lambda j, i: (0, j)),
                  pl.BlockSpec((B_W, MM_TN), lambda j, i: (0, j)),
                  pl.BlockSpec((MM_TM, MM_TN), lambda j, i: (i, ga0 + j)),
                  pl.BlockSpec((MM_TM, MM_TN), lambda j, i: (i, gb0 + j))],
        out_specs=pl.BlockSpec((MM_TM, MM_TN), lambda j, i: (i, j)),
        out_shape=jax.ShapeDtypeStruct((s, d_model), BF16),
        scratch_shapes=[pltpu.VMEM((A_Q_W, MM_TN), BF16), pltpu.VMEM((B_W, MM_TN), BF16)],
        compiler_params=_params(("parallel", "arbitrary")),
        name="merge",
    )(ya, yb, w_a, w_b, proj, proj)


def _resmm_kernel(a_ref, w_hbm, r_ref, o_ref, wb_ref, wbuf, wsem):
    _stream_weight_chunk(w_hbm, wbuf, wsem, wb_ref)
    o_ref[...] = r_ref[...] + jnp.dot(a_ref[...], wb_ref[...], preferred_element_type=F32)


def resmm_call(a, w, res):
    s, k = a.shape
    n = w.shape[1]
    return pl.pallas_call(
        _resmm_kernel,
        grid=(n // MM_TN, s // MM_TM),
        in_specs=[pl.BlockSpec((MM_TM, k), lambda j, i: (i, 0)),
                  pl.BlockSpec(memory_space=pl.ANY),
                  pl.BlockSpec((MM_TM, MM_TN), lambda j, i: (i, j))],
        out_specs=pl.BlockSpec((MM_TM, MM_TN), lambda j, i: (i, j)),
        out_shape=jax.ShapeDtypeStruct((s, n), F32),
        scratch_shapes=[pltpu.VMEM((k, MM_TN), BF16), pltpu.VMEM((2, k, MM_TN), F32),
                        pltpu.SemaphoreType.DMA((2,))],
        compiler_params=_params(("arbitrary", "arbitrary")),
        name="resmm",
    )(a, w, res)


def _ple_kernel(a_ref, wg_hbm, p_ref, wp_ref, h_ref, o_ref, wgb, wpb, wbuf, wsem):
    _stream_weight_chunk(wg_hbm, wbuf, wsem, wgb)

    @pl.when(pl.program_id(1) == 0)
    def _():
        wpb[...] = wp_ref[...].astype(BF16)

    gate = jax.nn.sigmoid(jnp.dot(a_ref[...], wgb[...], preferred_element_type=F32))
    emb = jnp.dot(p_ref[...].astype(BF16), wpb[...], preferred_element_type=F32)
    o_ref[...] = h_ref[...] + gate * emb


def ple_call(xn, w_gate, p, w_ple, h):
    s, d = xn.shape
    pd = p.shape[1]
    return pl.pallas_call(
        _ple_kernel,
        grid=(d // MM_TN, s // MM_TM),
        in_specs=[pl.BlockSpec((MM_TM, d), lambda j, i: (i, 0)),
                  pl.BlockSpec(memory_space=pl.ANY),
                  pl.BlockSpec((MM_TM, pd), lambda j, i: (i, 0)),
                  pl.BlockSpec((pd, MM_TN), lambda j, i: (0, j)),
                  pl.BlockSpec((MM_TM, MM_TN), lambda j, i: (i, j))],
        out_specs=pl.BlockSpec((MM_TM, MM_TN), lambda j, i: (i, j)),
        out_shape=jax.ShapeDtypeStruct((s, d), F32),
        scratch_shapes=[pltpu.VMEM((d, MM_TN), BF16), pltpu.VMEM((pd, MM_TN), BF16),
                        pltpu.VMEM((2, d, MM_TN), F32), pltpu.SemaphoreType.DMA((2,))],
        compiler_params=_params(("arbitrary", "arbitrary")),
        name="ple",
    )(xn, w_gate, p, w_ple, h)


def _dispatch_kernel(nused_ref, tok_ref, tokn_ref, x_hbm, o_ref, buf, sem):
    m = pl.program_id(0)
    nused = nused_ref[0]
    slot = m % 2

    def row_copy(tok, sl, r):
        return pltpu.make_async_copy(x_hbm.at[tok], buf.at[sl, r], sem.at[sl])

    def issue(t_ref, sl):
        def body(g, c):
            rows = [g * DMA_UNROLL + u for u in range(DMA_UNROLL)]
            toks = [t_ref[0, 0, r] for r in rows]
            for u in range(DMA_UNROLL):
                row_copy(toks[u], sl, rows[u]).start(priority=u % 2)
            return c

        lax.fori_loop(0, MOE_TM // DMA_UNROLL, body, 0)

    @pl.when(m == 0)
    def _():
        issue(tok_ref, 0)

    @pl.when(m + 1 < nused)
    def _():
        issue(tokn_ref, 1 - slot)

    @pl.when(m < nused)
    def _():
        pltpu.make_async_copy(x_hbm.at[pl.ds(0, MOE_TM)], buf.at[slot], sem.at[slot]).wait()
        o_ref[...] = buf[slot].reshape(o_ref.shape)

    @pl.when(m >= nused)
    def _():
        o_ref[...] = jnp.zeros(o_ref.shape, o_ref.dtype)


def dispatch_call(n_used, row_tok, xn3d):
    s, dg, _ = xn3d.shape
    d = dg * LANES
    n_rows = row_tok.shape[0]
    nblk = n_rows // MOE_TM
    tok3 = row_tok.reshape(nblk, 1, MOE_TM)
    tok_spec = lambda f: pl.BlockSpec((1, 1, MOE_TM), f, memory_space=pltpu.SMEM)
    return pl.pallas_call(
        _dispatch_kernel,
        grid_spec=pltpu.PrefetchScalarGridSpec(
            num_scalar_prefetch=1,
            grid=(nblk,),
            in_specs=[tok_spec(lambda m, nu: (m, 0, 0)),
                      tok_spec(lambda m, nu: (jnp.minimum(m + 1, nblk - 1), 0, 0)),
                      pl.BlockSpec(memory_space=pl.ANY)],
            out_specs=pl.BlockSpec((MOE_TM, d), lambda m, nu: (m, 0)),
            scratch_shapes=[pltpu.VMEM((2, MOE_TM, dg, LANES), BF16), pltpu.SemaphoreType.DMA((2,))]),
        out_shape=jax.ShapeDtypeStruct((n_rows, d), BF16),
        compiler_params=_params(("arbitrary",)),
        name="dispatch",
    )(n_used, tok3, tok3, xn3d)


def _expert_changed(be_ref, m):
    return jnp.logical_or(m == 0, be_ref[m] != be_ref[jnp.maximum(m - 1, 0)])


def _for_live_block(bv_ref, m, o_ref, compute):
    live = bv_ref[m] > 0

    @pl.when(live)
    def _():
        compute(slice(None))

    @pl.when(jnp.logical_not(live))
    def _():
        o_ref[...] = jnp.zeros(o_ref.shape, o_ref.dtype)


def _stream_expert_chunk(be_ref, bv_ref, sw_ref, nx_ref, nsw_ref, w_hbm, wbuf, wsem, wb):
    j = pl.program_id(0)
    m = pl.program_id(1)
    nj = pl.num_programs(0)
    tn = wbuf.shape[2]

    def chunk_copy(e, jj, slot):
        return pltpu.make_async_copy(w_hbm.at[e, :, pl.ds(pl.multiple_of(jj * tn, tn), tn)],
                                     wbuf.at[slot], wsem.at[slot])

    @pl.when(jnp.logical_and(bv_ref[m] > 0, _expert_changed(be_ref, m)))
    def _():
        k = j * nsw_ref[0] + sw_ref[m]
        slot = k % 2

        @pl.when(k == 0)
        def _():
            chunk_copy(be_ref[0], 0, 0).start()

        chunk_copy(be_ref[m], j, slot).wait()
        more_experts = nx_ref[m] >= 0

        @pl.when(more_experts)
        def _():
            chunk_copy(nx_ref[m], j, 1 - slot).start()

        @pl.when(jnp.logical_and(jnp.logical_not(more_experts), j + 1 < nj))
        def _():
            chunk_copy(be_ref[0], j + 1, 1 - slot).start()

        wb[...] = wbuf[slot].astype(BF16)


def _gateup_kernel(be_ref, bv_ref, nused_ref, sw_ref, nx_ref, nsw_ref, x_ref, w_hbm, bgu_ref, o_ref,
                   wb, wbuf, wsem):
    m = pl.program_id(1)
    _stream_expert_chunk(be_ref, bv_ref, sw_ref, nx_ref, nsw_ref, w_hbm, wbuf, wsem, wb)

    def compute_rows(rows):
        hgu = jnp.dot(x_ref[rows], wb[...], preferred_element_type=F32) + bgu_ref[...]
        gate = jnp.minimum(hgu, SWIGLU_LIMIT)
        up = jnp.clip(hgu, -SWIGLU_LIMIT, SWIGLU_LIMIT)
        up = pltpu.roll(up, 2 * MOE_TF - 1, 1)
        act = gate * jax.nn.sigmoid(gate * SWIGLU_ALPHA) * (up + 1.0)
        rr = lax.broadcasted_iota(jnp.int32, (2 * MOE_TF, MOE_TF), 0)
        cc = lax.broadcasted_iota(jnp.int32, (2 * MOE_TF, MOE_TF), 1)
        sel = (rr == 2 * cc).astype(BF16)
        o_ref[rows] = jnp.dot(act.astype(BF16), sel, preferred_element_type=F32).astype(o_ref.dtype)

    _for_live_block(bv_ref, m, o_ref, compute_rows)


def gateup_call(block_e, block_valid, n_used, switch_idx, next_e, n_switch, xs, w_gu, b_gu):
    n_rows, dh = xs.shape
    e, d, f2 = w_gu.shape
    f = f2 // 2
    nblk = n_rows // MOE_TM

    return pl.pallas_call(
        _gateup_kernel,
        grid_spec=pltpu.PrefetchScalarGridSpec(
            num_scalar_prefetch=6,
            grid=(f // MOE_TF, nblk),
            in_specs=[pl.BlockSpec((MOE_TM, dh), lambda j, m, be, bv, nu, *_: (jnp.minimum(m, nu[0] - 1), 0)),
                      pl.BlockSpec(memory_space=pl.ANY),
                      pl.BlockSpec((None, 1, 2 * MOE_TF), lambda j, m, be, *_: (be[m], 0, j))],
            out_specs=pl.BlockSpec((MOE_TM, MOE_TF), lambda j, m, *_: (m, j)),
            scratch_shapes=[pltpu.VMEM((d, 2 * MOE_TF), BF16), pltpu.VMEM((2, d, 2 * MOE_TF), F32),
                            pltpu.SemaphoreType.DMA((2,))]),
        out_shape=jax.ShapeDtypeStruct((n_rows, f), BF16),
        compiler_params=_params(("arbitrary", "arbitrary")),
        name="expert_gateup",
    )(block_e, block_valid, n_used, switch_idx, next_e, n_switch, xs, w_gu, b_gu.reshape(e, 1, f2))


def _down_kernel(be_ref, bv_ref, nused_ref, sw_ref, nx_ref, nsw_ref, a_ref, w_hbm, bdn_ref, o_ref,
                 wb, wbuf, wsem):
    m = pl.program_id(1)
    _stream_expert_chunk(be_ref, bv_ref, sw_ref, nx_ref, nsw_ref, w_hbm, wbuf, wsem, wb)

    def compute_rows(rows):
        out = jnp.dot(a_ref[rows], wb[...], preferred_element_type=F32) + bdn_ref[...]
        o_ref[rows] = _to_token_rows(out.astype(o_ref.dtype))

    _for_live_block(bv_ref, m, o_ref, compute_rows)


def down_call(block_e, block_valid, n_used, switch_idx, next_e, n_switch, act, w_dn, b_dn):
    n_rows, f = act.shape
    e, _, d = w_dn.shape
    nblk = n_rows // MOE_TM

    return pl.pallas_call(
        _down_kernel,
        grid_spec=pltpu.PrefetchScalarGridSpec(
            num_scalar_prefetch=6,
            grid=(d // MOE_TN, nblk),
            in_specs=[pl.BlockSpec((MOE_TM, f), lambda j, m, be, bv, nu, *_: (jnp.minimum(m, nu[0] - 1), 0)),
                      pl.BlockSpec(memory_space=pl.ANY),
                      pl.BlockSpec((None, 1, MOE_TN), lambda j, m, be, *_: (be[m], 0, j))],
            out_specs=pl.BlockSpec((MOE_TM, MOE_TN // LANES, LANES), lambda j, m, *_: (m, j, 0)),
            scratch_shapes=[pltpu.VMEM((f, MOE_TN), BF16), pltpu.VMEM((2, f, MOE_TN), F32),
                            pltpu.SemaphoreType.DMA((2,))]),
        out_shape=jax.ShapeDtypeStruct((n_rows, d // LANES, LANES), BF16),
        compiler_params=_params(("arbitrary", "arbitrary")),
        name="expert_down",
    )(block_e, block_valid, n_used, switch_idx, next_e, n_switch, act, w_dn, b_dn.reshape(e, 1, d))


def _combine_kernel(pos_ref, posn_ref, ys_hbm, h_ref, gt_ref, g_ref, h_out, xn_out, buf, sem):
    i = pl.program_id(0)
    n = pl.num_programs(0)
    slot = i % 2

    def row_copy(row, sl, k, t):
        return pltpu.make_async_copy(ys_hbm.at[row], buf.at[sl, k, t], sem.at[sl])

    def issue(p_ref, sl):
        def body(g, c):
            rows = [p_ref[0, 0, g * DMA_UNROLL + u] for u in range(DMA_UNROLL)]
            for u in range(DMA_UNROLL):
                t = g * (DMA_UNROLL // TOP_K) + u // TOP_K
                row_copy(rows[u], sl, u % TOP_K, t).start(priority=u % 2)
            return c

        lax.fori_loop(0, CMB_TB * TOP_K // DMA_UNROLL, body, 0)

    @pl.when(i == 0)
    def _():
        issue(pos_ref, 0)

    @pl.when(i + 1 < n)
    def _():
        issue(posn_ref, 1 - slot)

    for k in range(TOP_K):
        pltpu.make_async_copy(ys_hbm.at[pl.ds(0, CMB_TB)], buf.at[slot, k], sem.at[slot]).wait()

    h = h_ref[...]
    for k in range(TOP_K):
        rows = buf[slot, k].reshape(h.shape).astype(F32)
        h = h + gt_ref[:, k:k + 1] * rows
    h_out[...] = h
    ms = jnp.mean(h * h, axis=-1, keepdims=True)
    xn_out[...] = (h * lax.rsqrt(ms + RMS_EPS) * g_ref[...]).astype(xn_out.dtype)


def combine_call(pos, ys, h, gates, g):
    s, d = h.shape
    nb = s // CMB_TB
    pos3 = pos.reshape(nb, 1, CMB_TB * TOP_K)
    pos_spec = lambda f: pl.BlockSpec((1, 1, CMB_TB * TOP_K), f, memory_space=pltpu.SMEM)
    return pl.pallas_call(
        _combine_kernel,
        grid=(nb,),
        in_specs=[pos_spec(lambda i: (i, 0, 0)),
                  pos_spec(lambda i: (jnp.minimum(i + 1, nb - 1), 0, 0)),
                  pl.BlockSpec(memory_space=pl.ANY),
                  pl.BlockSpec((CMB_TB, d), lambda i: (i, 0)),
                  pl.BlockSpec((CMB_TB, LANES), lambda i: (i, 0)),
                  pl.BlockSpec((1, d), lambda i: (0, 0))],
        out_specs=[pl.BlockSpec((CMB_TB, d), lambda i: (i, 0)),
                   pl.BlockSpec((CMB_TB, d), lambda i: (i, 0))],
        out_shape=[jax.ShapeDtypeStruct((s, d), F32), jax.ShapeDtypeStruct((s, d), BF16)],
        scratch_shapes=[pltpu.VMEM((2, TOP_K, CMB_TB, d // LANES, LANES), BF16),
                        pltpu.SemaphoreType.DMA((2,))],
        compiler_params=_params(("arbitrary",)),
        name="combine",
    )(pos3, pos3, ys, h, gates, g.reshape(1, d))


def moe_routing(top_e, n_tokens):
    e_flat = top_e.reshape(-1)
    onehot = (e_flat[:, None] == jnp.arange(N_EXPERTS, dtype=jnp.int32)[None, :]).astype(jnp.int32)
    csum = jnp.cumsum(onehot, axis=0)
    counts = csum[-1]
    rank = jnp.sum(csum * onehot, axis=1) - 1
    padded = (counts + MOE_TM - 1) // MOE_TM * MOE_TM
    pends = jnp.cumsum(padded).astype(jnp.int32)
    pstarts = pends - padded
    pos = pstarts[e_flat] + rank
    n_blocks = (n_tokens * TOP_K) // MOE_TM + N_EXPERTS
    tok_flat = jnp.arange(n_tokens * TOP_K, dtype=jnp.int32) // TOP_K
    row_tok = jnp.zeros((n_blocks * MOE_TM,), jnp.int32).at[pos].set(tok_flat)
    n_used = pends[-1] // MOE_TM
    blk_start = jnp.arange(n_blocks, dtype=jnp.int32) * MOE_TM
    block_e = jnp.sum((pends[None, :] <= blk_start[:, None]).astype(jnp.int32), axis=1)
    block_e = jnp.minimum(block_e, N_EXPERTS - 1)
    last_e = block_e[jnp.maximum(n_used - 1, 0)]
    used = jnp.arange(n_blocks) < n_used
    real_end = pstarts + counts
    block_valid = jnp.clip(real_end[block_e] - blk_start, 0, MOE_TM)
    block_valid = jnp.where(used, block_valid, 0).astype(jnp.int32)
    block_e = jnp.where(used, block_e, last_e).astype(jnp.int32)
    has_rows = counts > 0
    eidx = jnp.arange(N_EXPERTS, dtype=jnp.int32)
    later = jnp.logical_and(eidx[None, :] > eidx[:, None], has_rows[None, :])
    next_of = jnp.min(jnp.where(later, eidx[None, :], N_EXPERTS), axis=1)
    next_of = jnp.where(next_of < N_EXPERTS, next_of, -1)
    order_of = jnp.cumsum(has_rows.astype(jnp.int32)) - 1
    routing = dict(block_e=block_e, block_valid=block_valid,
                   n_used=n_used.reshape(1).astype(jnp.int32),
                   switch_idx=order_of[block_e].astype(jnp.int32),
                   next_e=next_of[block_e].astype(jnp.int32),
                   n_switch=jnp.sum(has_rows.astype(jnp.int32)).reshape(1))
    return pos.astype(jnp.int32), row_tok, routing


def kernel(x, p, g_mix, w_in, attn_sink, na_rpb, w_branch_a, w_branch_b, w_out, g_ffn, w_router,
           b_router, w_gate_up, b_gate_up, w_down, b_down, g_ple, w_ple_gate, w_ple, g_final):
    b, s, d = x.shape
    assert b == 1 and w_in.shape[0] == 1
    h = x.reshape(s, d)
    rope_c, rope_s1, rope_s2 = rope_tables(s)

    xn = rmsnorm_call(h, g_mix[0], BF16)
    proj = inproj_call(xn, w_in[0], rope_c, rope_s1, rope_s2, d)
    ya = window_attn_call(proj, attn_sink[0])
    yb = na_attn_call(proj, na_bias_tables(na_rpb[0]))
    merged = merge_call(ya, yb, w_branch_a[0], w_branch_b[0], proj, d)
    h = resmm_call(merged, w_out[0], h)

    xn2, top_e, gates = rms_router_call(h, g_ffn[0], w_router[0], b_router[0])
    pos, row_tok, rt = moe_routing(top_e[:, :TOP_K], s)
    xs = dispatch_call(rt["n_used"], row_tok, xn2)
    sched = (rt["block_e"], rt["block_valid"], rt["n_used"], rt["switch_idx"], rt["next_e"], rt["n_switch"])
    act = gateup_call(*sched, xs, w_gate_up[0], b_gate_up[0])
    ys = down_call(*sched, act, w_down[0], b_down[0])
    h, xn3 = combine_call(pos, ys, h, gates, g_ple[0])

    h = ple_call(xn3, w_ple_gate[0], p[0].reshape(s, -1), w_ple[0], h)
    out = rmsnorm_call(h, g_final, F32)
    return out.reshape(b, s, d)
```

```python
import functools

import numpy as np
import jax
import jax.numpy as jnp
from jax import lax
from jax.experimental import pallas as pl
from jax.experimental.pallas import tpu as pltpu

F32 = jnp.float32
BF16 = jnp.bfloat16

HEAD_DIM = 64
A_HEADS = 32
A_KV_HEADS = 8
A_GROUP = A_HEADS // A_KV_HEADS
B_HEADS = 32
A_Q_W = A_HEADS * HEAD_DIM
A_KV_W = A_KV_HEADS * HEAD_DIM
B_W = B_HEADS * HEAD_DIM
ROT_DIM = HEAD_DIM // 4
ROPE_THETA = 500000.0
WINDOW = 128
GRID_W = 64
NA_ROWS = 8
NA_COLS = 16
N_EXPERTS = 32
TOP_K = 4
SWIGLU_LIMIT = 7.0
SWIGLU_ALPHA = 1.702
RMS_EPS = 1e-6
NEG_INF = -1e30

LANES = 128
VMEM_LIMIT = 56 * 1024 * 1024

RMS_ROWS = 256
MM_TM = 1024
MM_TN = 512
ATT_BLK = 128
NA_RB = 8
NA_HPS = 4
MOE_TM = 512
MOE_TF = 256
MOE_TN = 2048
CMB_TB = 128
DMA_UNROLL = 8
ROW_DMA_PRIORITY = 1


def _sigmoid(x):
    return 0.5 * jnp.tanh(0.5 * x) + 0.5


def _params(sem):
    return pltpu.CompilerParams(dimension_semantics=sem, vmem_limit_bytes=VMEM_LIMIT)


def _rms_body(x_ref, g_ref):
    x = x_ref[...].astype(F32)
    ms = jnp.mean(x * x, axis=-1, keepdims=True)
    return x * lax.rsqrt(ms + RMS_EPS) * g_ref[...]


def _rms_kernel(x_ref, g_ref, o_ref):
    o_ref[...] = _rms_body(x_ref, g_ref).astype(o_ref.dtype)


def rmsnorm_call(x, g, out_dtype):
    s, d = x.shape
    return pl.pallas_call(
        _rms_kernel,
        grid=(s // RMS_ROWS,),
        in_specs=[pl.BlockSpec((RMS_ROWS, d), lambda i: (i, 0)),
                  pl.BlockSpec((1, d), lambda i: (0, 0))],
        out_specs=pl.BlockSpec((RMS_ROWS, d), lambda i: (i, 0)),
        out_shape=jax.ShapeDtypeStruct((s, d), out_dtype),
        compiler_params=_params(("parallel",)),
        name="rmsnorm",
    )(x, g.reshape(1, d))


def _to_token_rows(x2d):
    return x2d.reshape(x2d.shape[0], x2d.shape[1] // LANES, LANES)


def _split_bf16(x):
    hi = x.astype(BF16)
    return hi, (x - hi.astype(F32)).astype(BF16)


def _rms_router_kernel(x_ref, g_ref, wh_ref, wl_ref, br_ref, xn_ref, e_ref, gt_ref):
    xn = _rms_body(x_ref, g_ref)
    xh, xl = _split_bf16(xn)
    xn_ref[...] = _to_token_rows(xh)
    logits = (jnp.dot(xh, wh_ref[...], preferred_element_type=F32)
              + jnp.dot(xh, wl_ref[...], preferred_element_type=F32)
              + jnp.dot(xl, wh_ref[...], preferred_element_type=F32)) + br_ref[...]
    lane = lax.broadcasted_iota(jnp.int32, logits.shape, 1)
    logits = jnp.where(lane < N_EXPERTS, logits, -jnp.inf)
    e_out = jnp.zeros(logits.shape, jnp.int32)
    v_out = jnp.zeros(logits.shape, F32)
    v0 = None
    for k in range(TOP_K):
        m = jnp.max(logits, axis=-1, keepdims=True)
        idx = jnp.min(jnp.where(logits == m, lane, LANES), axis=-1, keepdims=True)
        if k == 0:
            v0 = m
        e_out = jnp.where(lane == k, idx, e_out)
        v_out = jnp.where(lane == k, jnp.exp(m - v0), v_out)
        logits = jnp.where(lane == idx, -jnp.inf, logits)
    e_ref[...] = e_out
    gt_ref[...] = v_out / jnp.sum(v_out, axis=-1, keepdims=True)


def rms_router_call(h, g, w_router, b_router):
    s, d = h.shape
    wr = jnp.zeros((d, LANES), F32).at[:, :N_EXPERTS].set(w_router)
    br = jnp.zeros((1, LANES), F32).at[0, :N_EXPERTS].set(b_router)
    wr_hi, wr_lo = _split_bf16(wr)
    return pl.pallas_call(
        _rms_router_kernel,
        grid=(s // RMS_ROWS,),
        in_specs=[pl.BlockSpec((RMS_ROWS, d), lambda i: (i, 0)),
                  pl.BlockSpec((1, d), lambda i: (0, 0)),
                  pl.BlockSpec((d, LANES), lambda i: (0, 0)),
                  pl.BlockSpec((d, LANES), lambda i: (0, 0)),
                  pl.BlockSpec((1, LANES), lambda i: (0, 0))],
        out_specs=[pl.BlockSpec((RMS_ROWS, d // LANES, LANES), lambda i: (i, 0, 0)),
                   pl.BlockSpec((RMS_ROWS, LANES), lambda i: (i, 0)),
                   pl.BlockSpec((RMS_ROWS, LANES), lambda i: (i, 0))],
        out_shape=[jax.ShapeDtypeStruct((s, d // LANES, LANES), BF16),
                   jax.ShapeDtypeStruct((s, LANES), jnp.int32),
                   jax.ShapeDtypeStruct((s, LANES), F32)],
        compiler_params=_params(("parallel",)),
        name="rms_router",
    )(h, g.reshape(1, d), wr_hi, wr_lo, br)


def _rope(acc, c_ref, s1_ref, s2_ref):
    c, s1, s2 = c_ref[...], s1_ref[...], s2_ref[...]
    half = ROT_DIM // 2
    outs = []
    for t in range(acc.shape[1] // LANES):
        x = acc[:, t * LANES:(t + 1) * LANES]
        outs.append(x * c + pltpu.roll(x, LANES - half, 1) * s1 + pltpu.roll(x, half, 1) * s2)
    return jnp.concatenate(outs, axis=1)


def _stream_weight_chunk(w_hbm, wbuf, sem, wb_ref):
    j = pl.program_id(0)
    nj = pl.num_programs(0)
    tn = wbuf.shape[2]

    def chunk_copy(jj, slot):
        return pltpu.make_async_copy(w_hbm.at[:, pl.ds(pl.multiple_of(jj * tn, tn), tn)],
                                     wbuf.at[slot], sem.at[slot])

    @pl.when(pl.program_id(1) == 0)
    def _():
        slot = j % 2

        @pl.when(j == 0)
        def _():
            chunk_copy(0, 0).start()

        chunk_copy(j, slot).wait()

        @pl.when(j + 1 < nj)
        def _():
            chunk_copy(j + 1, 1 - slot).start()

        wb_ref[...] = wbuf[slot].astype(BF16)


def _inproj_kernel(a_ref, w_hbm, c_ref, s1_ref, s2_ref, o_ref, wb_ref, wbuf, wsem, *, n_rope, n_plain):
    j = pl.program_id(0)
    _stream_weight_chunk(w_hbm, wbuf, wsem, wb_ref)
    acc = jnp.dot(a_ref[...], wb_ref[...], preferred_element_type=F32)

    @pl.when(j < n_rope)
    def _():
        o_ref[...] = _rope(acc, c_ref, s1_ref, s2_ref).astype(o_ref.dtype)

    @pl.when(jnp.logical_and(j >= n_rope, j < n_plain))
    def _():
        o_ref[...] = acc.astype(o_ref.dtype)

    @pl.when(j >= n_plain)
    def _():
        o_ref[...] = _sigmoid(acc).astype(o_ref.dtype)


def inproj_call(xn, w_in, rope_c, rope_s1, rope_s2, d_model):
    s, d = xn.shape
    n = w_in.shape[1]
    n_rope = (A_Q_W + A_KV_W) // MM_TN
    n_plain = (n - 2 * d_model) // MM_TN
    tab = pl.BlockSpec((MM_TM, LANES), lambda j, i: (i, 0))
    return pl.pallas_call(
        functools.partial(_inproj_kernel, n_rope=n_rope, n_plain=n_plain),
        grid=(n // MM_TN, s // MM_TM),
        in_specs=[pl.BlockSpec((MM_TM, d), lambda j, i: (i, 0)),
                  pl.BlockSpec(memory_space=pl.ANY),
                  tab, tab, tab],
        out_specs=pl.BlockSpec((MM_TM, MM_TN), lambda j, i: (i, j)),
        out_shape=jax.ShapeDtypeStruct((s, n), BF16),
        scratch_shapes=[pltpu.VMEM((d, MM_TN), BF16), pltpu.VMEM((2, d, MM_TN), F32),
                        pltpu.SemaphoreType.DMA((2,))],
        compiler_params=_params(("arbitrary", "arbitrary")),
        name="inproj",
    )(xn, w_in, rope_c, rope_s1, rope_s2)


def rope_tables(s):
    half = ROT_DIM // 2
    inv = jnp.float32(ROPE_THETA) ** (-jnp.arange(half, dtype=F32) * (2.0 / ROT_DIM))
    ang = jnp.arange(s, dtype=F32)[:, None] * inv[None, :]
    cos, sin = jnp.cos(ang), jnp.sin(ang)
    ones = jnp.ones((s, HEAD_DIM - ROT_DIM), F32)
    zeros = jnp.zeros((s, HEAD_DIM - ROT_DIM), F32)
    zh = jnp.zeros((s, half), F32)
    c = jnp.concatenate([cos, cos, ones], axis=1)
    s1 = jnp.concatenate([-sin, zh, zeros], axis=1)
    s2 = jnp.concatenate([zh, sin, zeros], axis=1)
    rep = LANES // HEAD_DIM
    return jnp.tile(c, (1, rep)), jnp.tile(s1, (1, rep)), jnp.tile(s2, (1, rep))


def _win_kernel(sink_ref, q_ref, kp_ref, kc_ref, kn_ref, vp_ref, vc_ref, vn_ref, o_ref, *, seq):
    n = pl.program_id(0)
    pr = pl.program_id(1)
    kcat = jnp.concatenate([kp_ref[...], kc_ref[...], kn_ref[...]], axis=0)
    vcat = jnp.concatenate([vp_ref[...], vc_ref[...], vn_ref[...]], axis=0)
    q = q_ref[...] * jnp.asarray(HEAD_DIM ** -0.5, BF16)
    rows = A_GROUP * ATT_BLK
    nk = 3 * ATT_BLK
    qpos = n * ATT_BLK + lax.broadcasted_iota(jnp.int32, (ATT_BLK, nk), 0)
    kpos = (n - 1) * ATT_BLK + lax.broadcasted_iota(jnp.int32, (ATT_BLK, nk), 1)
    mask = ((jnp.abs(qpos - kpos) <= WINDOW) & (kpos >= 0) & (kpos < seq))[None]
    grp = lax.broadcasted_iota(jnp.int32, (A_GROUP, 1, 1), 0)
    kv_per_blk = LANES // HEAD_DIM
    for kvh in range(kv_per_blk):
        k_h = kcat[:, kvh * HEAD_DIM:(kvh + 1) * HEAD_DIM]
        v_h = vcat[:, kvh * HEAD_DIM:(kvh + 1) * HEAD_DIM]
        h0 = kvh * A_GROUP
        qs = jnp.concatenate(
            [q[:, (h0 + g) * HEAD_DIM:(h0 + g + 1) * HEAD_DIM] for g in range(A_GROUP)], axis=0)
        sc = lax.dot_general(qs, k_h, (((1,), (1,)), ((), ())), preferred_element_type=F32)
        sc = jnp.where(mask, sc.reshape(A_GROUP, ATT_BLK, nk), NEG_INF)
        snk = jnp.zeros((A_GROUP, 1, 1), F32)
        for g in range(A_GROUP):
            sv = sink_ref[(pr * kv_per_blk + kvh) * A_GROUP + g]
            snk = jnp.where(grp == g, sv, snk)
        m = jnp.maximum(jnp.max(sc, axis=-1, keepdims=True), snk)
        p = jnp.exp(sc - m)
        denom = jnp.sum(p, axis=-1, keepdims=True) + jnp.exp(snk - m)
        attn = (p * (1.0 / denom)).astype(BF16).reshape(rows, nk)
        out = jnp.dot(attn, v_h, preferred_element_type=F32)
        for g in range(A_GROUP):
            o_ref[:, (h0 + g) * HEAD_DIM:(h0 + g + 1) * HEAD_DIM] = (
                out[g * ATT_BLK:(g + 1) * ATT_BLK].astype(o_ref.dtype))


def window_attn_call(proj, sink):
    s = proj.shape[0]
    nb = s // ATT_BLK
    qw = A_GROUP * LANES
    k0 = A_Q_W // LANES
    v0 = (A_Q_W + A_KV_W) // LANES
    blk = (ATT_BLK, LANES)
    prev = lambda n: jnp.maximum(n - 1, 0)
    nxt = lambda n: jnp.minimum(n + 1, nb - 1)
    return pl.pallas_call(
        functools.partial(_win_kernel, seq=s),
        grid=(nb, A_Q_W // qw),
        in_specs=[pl.BlockSpec(memory_space=pltpu.SMEM),
                  pl.BlockSpec((ATT_BLK, qw), lambda n, p: (n, p)),
                  pl.BlockSpec(blk, lambda n, p: (prev(n), k0 + p)),
                  pl.BlockSpec(blk, lambda n, p: (n, k0 + p)),
                  pl.BlockSpec(blk, lambda n, p: (nxt(n), k0 + p)),
                  pl.BlockSpec(blk, lambda n, p: (prev(n), v0 + p)),
                  pl.BlockSpec(blk, lambda n, p: (n, v0 + p)),
                  pl.BlockSpec(blk, lambda n, p: (nxt(n), v0 + p))],
        out_specs=pl.BlockSpec((ATT_BLK, qw), lambda n, p: (n, p)),
        out_shape=jax.ShapeDtypeStruct((s, A_Q_W), BF16),
        compiler_params=_params(("parallel", "parallel")),
        name="window_attn",
    )(sink, proj, proj, proj, proj, proj, proj, proj)


def na_bias_tables(rpb):
    c = np.arange(GRID_W)
    kc = np.arange(GRID_W)
    dc = np.clip(kc[None, :] - c[:, None] + NA_COLS - 1, 0, 2 * NA_COLS - 2)
    qcs = np.clip(c - NA_COLS // 2, 0, GRID_W - NA_COLS)
    cmask = (kc[None, :] >= qcs[:, None]) & (kc[None, :] < qcs[:, None] + NA_COLS)
    t = jnp.where(jnp.asarray(cmask)[None, None], rpb[:, :, dc].astype(F32), NEG_INF)
    t = jnp.transpose(t, (0, 2, 1, 3)).reshape(rpb.shape[0], GRID_W, (2 * NA_ROWS - 1) * GRID_W)
    return jnp.pad(t, ((0, 0), (0, 0), (GRID_W, 0)))


def _na_kernel(q_ref, kp_ref, kc_ref, kn_ref, vp_ref, vc_ref, vn_ref, b_ref, o_ref,
               kbuf, vbuf, s_scr, p_scr, *, grid_rows):
    rb = pl.program_id(1)
    blk = NA_RB * GRID_W
    kbuf[0:blk] = kp_ref[...]
    kbuf[blk:2 * blk] = kc_ref[...]
    kbuf[2 * blk:3 * blk] = kn_ref[...]
    vbuf[0:blk] = vp_ref[...]
    vbuf[blk:2 * blk] = vc_ref[...]
    vbuf[2 * blk:3 * blk] = vn_ref[...]
    nkeys = NA_ROWS * GRID_W
    width = NA_HPS * HEAD_DIM
    grp = NA_HPS * GRID_W
    row_head = lax.broadcasted_iota(jnp.int32, (grp, width), 0) // GRID_W
    lane_head = lax.broadcasted_iota(jnp.int32, (grp, width), 1) // HEAD_DIM
    own = row_head == lane_head
    out_head = lax.broadcasted_iota(jnp.int32, (GRID_W, width), 1) // HEAD_DIM
    nrb = grid_rows // NA_RB
    half = NA_ROWS // 2

    def rows(first_rb):
        starts = []
        for i in range(NA_RB):
            r = first_rb * NA_RB + i
            rs = min(max(r - half, 0), grid_rows - NA_ROWS)
            typ = rs - (r - half) + (half - 1)
            start = (rs - (first_rb - 1) * NA_RB) * GRID_W
            starts.append(start)
            qi = q_ref[i * GRID_W:(i + 1) * GRID_W, :] * jnp.asarray(HEAD_DIM ** -0.5, BF16)
            qs = jnp.where(own, jnp.concatenate([qi] * NA_HPS, axis=0), jnp.zeros((grp, width), BF16))
            sc = lax.dot_general(qs, kbuf[start:start + nkeys, :], (((1,), (1,)), ((), ())),
                                 preferred_element_type=F32)
            boff = (typ + 1) * GRID_W
            s_scr[i * grp:(i + 1) * grp, :] = sc + b_ref[:, :, boff:boff + nkeys].reshape(grp, nkeys)
        for i in range(NA_RB):
            sc = s_scr[i * grp:(i + 1) * grp, :]
            m = jnp.max(sc, axis=-1, keepdims=True)
            p = jnp.exp(sc - m)
            p_scr[i * grp:(i + 1) * grp, :] = (
                p * (1.0 / jnp.sum(p, axis=-1, keepdims=True))).astype(BF16)
        for i in range(NA_RB):
            start = starts[i]
            out = jnp.dot(p_scr[i * grp:(i + 1) * grp, :], vbuf[start:start + nkeys, :],
                          preferred_element_type=F32)
            res = out[:GRID_W]
            for h in range(1, NA_HPS):
                res = jnp.where(out_head == h, out[h * GRID_W:(h + 1) * GRID_W], res)
            o_ref[i * GRID_W:(i + 1) * GRID_W, :] = res.astype(o_ref.dtype)

    @pl.when(rb == 0)
    def _():
        rows(0)

    @pl.when(rb == nrb - 1)
    def _():
        rows(nrb - 1)

    @pl.when(jnp.logical_and(rb > 0, rb < nrb - 1))
    def _():
        rows(1)


def na_attn_call(proj, bias_tab):
    s = proj.shape[0]
    grid_rows = s // GRID_W
    nrb = grid_rows // NA_RB
    blk_rows = NA_RB * GRID_W
    width = NA_HPS * HEAD_DIM
    q0 = (A_Q_W + 2 * A_KV_W) // width
    k0 = q0 + B_W // width
    v0 = k0 + B_W // width
    blk = (blk_rows, width)
    hpb = NA_HPS
    prev = lambda r: jnp.maximum(r - 1, 0)
    cur = lambda r: r
    nxt = lambda r: jnp.minimum(r + 1, nrb - 1)
    return pl.pallas_call(
        functools.partial(_na_kernel, grid_rows=grid_rows),
        grid=(B_HEADS // hpb, nrb),
        in_specs=[pl.BlockSpec(blk, lambda h, r: (r, q0 + h)),
                  pl.BlockSpec(blk, lambda h, r: (prev(r), k0 + h)),
                  pl.BlockSpec(blk, lambda h, r: (cur(r), k0 + h)),
                  pl.BlockSpec(blk, lambda h, r: (nxt(r), k0 + h)),
                  pl.BlockSpec(blk, lambda h, r: (prev(r), v0 + h)),
                  pl.BlockSpec(blk, lambda h, r: (cur(r), v0 + h)),
                  pl.BlockSpec(blk, lambda h, r: (nxt(r), v0 + h)),
                  pl.BlockSpec((hpb, GRID_W, 2 * NA_ROWS * GRID_W), lambda h, r: (h, 0, 0))],
        out_specs=pl.BlockSpec(blk, lambda h, r: (r, h)),
        out_shape=jax.ShapeDtypeStruct((s, B_W), BF16),
        scratch_shapes=[pltpu.VMEM((3 * blk_rows, width), BF16),
                        pltpu.VMEM((3 * blk_rows, width), BF16),
                        pltpu.VMEM((NA_RB * hpb * GRID_W, NA_ROWS * GRID_W), F32),
                        pltpu.VMEM((NA_RB * hpb * GRID_W, NA_ROWS * GRID_W), BF16)],
        compiler_params=_params(("parallel", "arbitrary")),
        name="na_attn",
    )(proj, proj, proj, proj, proj, proj, proj, bias_tab)


def _merge_kernel(ya_ref, yb_ref, wa_ref, wb_ref, ga_ref, gb_ref, o_ref, wab, wbb):
    @pl.when(pl.program_id(1) == 0)
    def _():
        wab[...] = wa_ref[...].astype(BF16)
        wbb[...] = wb_ref[...].astype(BF16)

    a = jnp.dot(ya_ref[...], wab[...], preferred_element_type=F32)
    b = jnp.dot(yb_ref[...], wbb[...], preferred_element_type=F32)
    o_ref[...] = (ga_ref[...].astype(F32) * a + gb_ref[...].astype(F32) * b).astype(o_ref.dtype)


def merge_call(ya, yb, w_a, w_b, proj, d_model):
    s = ya.shape[0]
    ga0 = (proj.shape[1] - 2 * d_model) // MM_TN
    gb0 = ga0 + d_model // MM_TN
    return pl.pallas_call(
        _merge_kernel,
        grid=(d_model // MM_TN, s // MM_TM),
        in_specs=[pl.BlockSpec((MM_TM, A_Q_W), lambda j, i: (i, 0)),
                  pl.BlockSpec((MM_TM, B_W), lambda j, i: (i, 0)),
                  pl.BlockSpec((A_Q_W, MM_TN), lambda j, i: (0, j)),
                  pl.BlockSpec((B_W, MM_TN), lambda j, i: (0, j)),
                  pl.BlockSpec((MM_TM, MM_TN), lambda j, i: (i, ga0 + j)),
                  pl.BlockSpec((MM_TM, MM_TN), lambda j, i: (i, gb0 + j))],
        out_specs=pl.BlockSpec((MM_TM, MM_TN), lambda j, i: (i, j)),
        out_shape=jax.ShapeDtypeStruct((s, d_model), BF16),
        scratch_shapes=[pltpu.VMEM((A_Q_W, MM_TN), BF16), pltpu.VMEM((B_W, MM_TN), BF16)],
        compiler_params=_params(("parallel", "arbitrary")),
        name="merge",
    )(ya, yb, w_a, w_b, proj, proj)


def _resmm_kernel(a_ref, w_hbm, r_ref, o_ref, wb_ref, wbuf, wsem):
    _stream_weight_chunk(w_hbm, wbuf, wsem, wb_ref)
    o_ref[...] = r_ref[...] + jnp.dot(a_ref[...], wb_ref[...], preferred_element_type=F32)


def resmm_call(a, w, res):
    s, k = a.shape
    n = w.shape[1]
    return pl.pallas_call(
        _resmm_kernel,
        grid=(n // MM_TN, s // MM_TM),
        in_specs=[pl.BlockSpec((MM_TM, k), lambda j, i: (i, 0)),
                  pl.BlockSpec(memory_space=pl.ANY),
                  pl.BlockSpec((MM_TM, MM_TN), lambda j, i: (i, j))],
        out_specs=pl.BlockSpec((MM_TM, MM_TN), lambda j, i: (i, j)),
        out_shape=jax.ShapeDtypeStruct((s, n), F32),
        scratch_shapes=[pltpu.VMEM((k, MM_TN), BF16), pltpu.VMEM((2, k, MM_TN), F32),
                        pltpu.SemaphoreType.DMA((2,))],
        compiler_params=_params(("arbitrary", "arbitrary")),
        name="resmm",
    )(a, w, res)


def _ple_kernel(a_ref, wg_hbm, p_ref, wp_ref, h_ref, o_ref, wgb, wpb, wbuf, wsem):
    _stream_weight_chunk(wg_hbm, wbuf, wsem, wgb)

    @pl.when(pl.program_id(1) == 0)
    def _():
        wpb[...] = wp_ref[...].astype(BF16)

    gate = _sigmoid(jnp.dot(a_ref[...], wgb[...], preferred_element_type=F32))
    emb = jnp.dot(p_ref[...].astype(BF16), wpb[...], preferred_element_type=F32)
    o_ref[...] = h_ref[...] + gate * emb


def ple_call(xn, w_gate, p, w_ple, h):
    s, d = xn.shape
    pd = p.shape[1]
    return pl.pallas_call(
        _ple_kernel,
        grid=(d // MM_TN, s // MM_TM),
        in_specs=[pl.BlockSpec((MM_TM, d), lambda j, i: (i, 0)),
                  pl.BlockSpec(memory_space=pl.ANY),
                  pl.BlockSpec((MM_TM, pd), lambda j, i: (i, 0)),
                  pl.BlockSpec((pd, MM_TN), lambda j, i: (0, j)),
                  pl.BlockSpec((MM_TM, MM_TN), lambda j, i: (i, j))],
        out_specs=pl.BlockSpec((MM_TM, MM_TN), lambda j, i: (i, j)),
        out_shape=jax.ShapeDtypeStruct((s, d), F32),
        scratch_shapes=[pltpu.VMEM((d, MM_TN), BF16), pltpu.VMEM((pd, MM_TN), BF16),
                        pltpu.VMEM((2, d, MM_TN), F32), pltpu.SemaphoreType.DMA((2,))],
        compiler_params=_params(("arbitrary", "arbitrary")),
        name="ple",
    )(xn, w_gate, p, w_ple, h)


def _dispatch_kernel(nused_ref, tok_ref, tokn_ref, x_hbm, o_ref, buf, sem):
    m = pl.program_id(0)
    nused = nused_ref[0]
    slot = m % 2

    def row_copy(tok, sl, r):
        return pltpu.make_async_copy(x_hbm.at[tok], buf.at[sl, r], sem.at[sl])

    def issue(t_ref, sl):
        def body(g, c):
            rows = [g * DMA_UNROLL + u for u in range(DMA_UNROLL)]
            toks = [t_ref[0, 0, r] for r in rows]
            for u in range(DMA_UNROLL):
                row_copy(toks[u], sl, rows[u]).start(priority=ROW_DMA_PRIORITY)
            return c

        lax.fori_loop(0, MOE_TM // DMA_UNROLL, body, 0)

    @pl.when(m == 0)
    def _():
        issue(tok_ref, 0)

    @pl.when(m + 1 < nused)
    def _():
        issue(tokn_ref, 1 - slot)

    @pl.when(m < nused)
    def _():
        pltpu.make_async_copy(x_hbm.at[pl.ds(0, MOE_TM)], buf.at[slot], sem.at[slot]).wait()
        o_ref[...] = buf[slot].reshape(o_ref.shape)

    @pl.when(m >= nused)
    def _():
        o_ref[...] = jnp.zeros(o_ref.shape, o_ref.dtype)


def dispatch_call(n_used, row_tok, xn3d):
    s, dg, _ = xn3d.shape
    d = dg * LANES
    n_rows = row_tok.shape[0]
    nblk = n_rows // MOE_TM
    tok3 = row_tok.reshape(nblk, 1, MOE_TM)
    tok_spec = lambda f: pl.BlockSpec((1, 1, MOE_TM), f, memory_space=pltpu.SMEM)
    return pl.pallas_call(
        _dispatch_kernel,
        grid_spec=pltpu.PrefetchScalarGridSpec(
            num_scalar_prefetch=1,
            grid=(nblk,),
            in_specs=[tok_spec(lambda m, nu: (m, 0, 0)),
                      tok_spec(lambda m, nu: (jnp.minimum(m + 1, nblk - 1), 0, 0)),
                      pl.BlockSpec(memory_space=pl.ANY)],
            out_specs=pl.BlockSpec((MOE_TM, d), lambda m, nu: (m, 0)),
            scratch_shapes=[pltpu.VMEM((2, MOE_TM, dg, LANES), BF16), pltpu.SemaphoreType.DMA((2,))]),
        out_shape=jax.ShapeDtypeStruct((n_rows, d), BF16),
        compiler_params=_params(("arbitrary",)),
        name="dispatch",
    )(n_used, tok3, tok3, xn3d)


def _expert_changed(be_ref, m):
    return jnp.logical_or(m == 0, be_ref[m] != be_ref[jnp.maximum(m - 1, 0)])


def _for_live_block(bv_ref, m, o_ref, compute):
    live = bv_ref[m] > 0

    @pl.when(live)
    def _():
        compute(slice(None))

    @pl.when(jnp.logical_not(live))
    def _():
        o_ref[...] = jnp.zeros(o_ref.shape, o_ref.dtype)


def _stream_expert_chunk(be_ref, bv_ref, sw_ref, nx_ref, nsw_ref, w_hbm, wbuf, wsem, wb):
    j = pl.program_id(0)
    m = pl.program_id(1)
    nj = pl.num_programs(0)
    tn = wbuf.shape[2]

    def chunk_copy(e, jj, slot):
        return pltpu.make_async_copy(w_hbm.at[e, :, pl.ds(pl.multiple_of(jj * tn, tn), tn)],
                                     wbuf.at[slot], wsem.at[slot])

    @pl.when(jnp.logical_and(bv_ref[m] > 0, _expert_changed(be_ref, m)))
    def _():
        k = j * nsw_ref[0] + sw_ref[m]
        slot = k % 2

        @pl.when(k == 0)
        def _():
            chunk_copy(be_ref[0], 0, 0).start()

        chunk_copy(be_ref[m], j, slot).wait()
        more_experts = nx_ref[m] >= 0

        @pl.when(more_experts)
        def _():
            chunk_copy(nx_ref[m], j, 1 - slot).start()

        @pl.when(jnp.logical_and(jnp.logical_not(more_experts), j + 1 < nj))
        def _():
            chunk_copy(be_ref[0], j + 1, 1 - slot).start()

        wb[...] = wbuf[slot].astype(BF16)


def _gateup_kernel(be_ref, bv_ref, nused_ref, sw_ref, nx_ref, nsw_ref, x_ref, w_hbm, bgu_ref, o_ref,
                   wb, wbuf, wsem):
    m = pl.program_id(1)
    _stream_expert_chunk(be_ref, bv_ref, sw_ref, nx_ref, nsw_ref, w_hbm, wbuf, wsem, wb)

    def compute_rows(rows):
        hgu = jnp.dot(x_ref[rows], wb[...], preferred_element_type=F32) + bgu_ref[...]
        gate = jnp.minimum(hgu, SWIGLU_LIMIT)
        up = jnp.clip(hgu, -SWIGLU_LIMIT, SWIGLU_LIMIT)
        up = pltpu.roll(up, 2 * MOE_TF - 1, 1)
        act = gate * _sigmoid(gate * SWIGLU_ALPHA) * (up + 1.0)
        rr = lax.broadcasted_iota(jnp.int32, (2 * MOE_TF, MOE_TF), 0)
        cc = lax.broadcasted_iota(jnp.int32, (2 * MOE_TF, MOE_TF), 1)
        sel = (rr == 2 * cc).astype(BF16)
        o_ref[rows] = jnp.dot(act.astype(BF16), sel, preferred_element_type=F32).astype(o_ref.dtype)

    _for_live_block(bv_ref, m, o_ref, compute_rows)


def gateup_call(block_e, block_valid, n_used, switch_idx, next_e, n_switch, xs, w_gu, b_gu):
    n_rows, dh = xs.shape
    e, d, f2 = w_gu.shape
    f = f2 // 2
    nblk = n_rows // MOE_TM

    return pl.pallas_call(
        _gateup_kernel,
        grid_spec=pltpu.PrefetchScalarGridSpec(
            num_scalar_prefetch=6,
            grid=(f // MOE_TF, nblk),
            in_specs=[pl.BlockSpec((MOE_TM, dh), lambda j, m, be, bv, nu, *_: (jnp.minimum(m, nu[0] - 1), 0)),
                      pl.BlockSpec(memory_space=pl.ANY),
                      pl.BlockSpec((None, 1, 2 * MOE_TF), lambda j, m, be, *_: (be[m], 0, j))],
            out_specs=pl.BlockSpec((MOE_TM, MOE_TF), lambda j, m, *_: (m, j)),
            scratch_shapes=[pltpu.VMEM((d, 2 * MOE_TF), BF16), pltpu.VMEM((2, d, 2 * MOE_TF), F32),
                            pltpu.SemaphoreType.DMA((2,))]),
        out_shape=jax.ShapeDtypeStruct((n_rows, f), BF16),
        compiler_params=_params(("arbitrary", "arbitrary")),
        name="expert_gateup",
    )(block_e, block_valid, n_used, switch_idx, next_e, n_switch, xs, w_gu, b_gu.reshape(e, 1, f2))


def _down_kernel(be_ref, bv_ref, nused_ref, sw_ref, nx_ref, nsw_ref, a_ref, w_hbm, bdn_ref, o_ref,
                 wb, wbuf, wsem):
    m = pl.program_id(1)
    _stream_expert_chunk(be_ref, bv_ref, sw_ref, nx_ref, nsw_ref, w_hbm, wbuf, wsem, wb)

    def compute_rows(rows):
        out = jnp.dot(a_ref[rows], wb[...], preferred_element_type=F32) + bdn_ref[...]
        o_ref[rows] = _to_token_rows(out.astype(o_ref.dtype))

    _for_live_block(bv_ref, m, o_ref, compute_rows)


def down_call(block_e, block_valid, n_used, switch_idx, next_e, n_switch, act, w_dn, b_dn):
    n_rows, f = act.shape
    e, _, d = w_dn.shape
    nblk = n_rows // MOE_TM

    return pl.pallas_call(
        _down_kernel,
        grid_spec=pltpu.PrefetchScalarGridSpec(
            num_scalar_prefetch=6,
            grid=(d // MOE_TN, nblk),
            in_specs=[pl.BlockSpec((MOE_TM, f), lambda j, m, be, bv, nu, *_: (jnp.minimum(m, nu[0] - 1), 0)),
                      pl.BlockSpec(memory_space=pl.ANY),
                      pl.BlockSpec((None, 1, MOE_TN), lambda j, m, be, *_: (be[m], 0, j))],
            out_specs=pl.BlockSpec((MOE_TM, MOE_TN // LANES, LANES), lambda j, m, *_: (m, j, 0)),
            scratch_shapes=[pltpu.VMEM((f, MOE_TN), BF16), pltpu.VMEM((2, f, MOE_TN), F32),
                            pltpu.SemaphoreType.DMA((2,))]),
        out_shape=jax.ShapeDtypeStruct((n_rows, d // LANES, LANES), BF16),
        compiler_params=_params(("arbitrary", "arbitrary")),
        name="expert_down",
    )(block_e, block_valid, n_used, switch_idx, next_e, n_switch, act, w_dn, b_dn.reshape(e, 1, d))


def _combine_kernel(pos_ref, posn_ref, ys_hbm, h_ref, gt_ref, g_ref, h_out, xn_out, buf, sem):
    i = pl.program_id(0)
    n = pl.num_programs(0)
    slot = i % 2

    def row_copy(row, sl, k, t):
        return pltpu.make_async_copy(ys_hbm.at[row], buf.at[sl, k, t], sem.at[sl])

    def issue(p_ref, sl):
        def body(g, c):
            rows = [p_ref[0, 0, g * DMA_UNROLL + u] for u in range(DMA_UNROLL)]
            for u in range(DMA_UNROLL):
                t = g * (DMA_UNROLL // TOP_K) + u // TOP_K
                row_copy(rows[u], sl, u % TOP_K, t).start(priority=ROW_DMA_PRIORITY)
            return c

        lax.fori_loop(0, CMB_TB * TOP_K // DMA_UNROLL, body, 0)

    @pl.when(i == 0)
    def _():
        issue(pos_ref, 0)

    @pl.when(i + 1 < n)
    def _():
        issue(posn_ref, 1 - slot)

    for k in range(TOP_K):
        pltpu.make_async_copy(ys_hbm.at[pl.ds(0, CMB_TB)], buf.at[slot, k], sem.at[slot]).wait()

    h = h_ref[...]
    for k in range(TOP_K):
        rows = buf[slot, k].reshape(h.shape).astype(F32)
        h = h + gt_ref[:, k:k + 1] * rows
    h_out[...] = h
    ms = jnp.mean(h * h, axis=-1, keepdims=True)
    xn_out[...] = (h * lax.rsqrt(ms + RMS_EPS) * g_ref[...]).astype(xn_out.dtype)


def combine_call(pos, ys, h, gates, g):
    s, d = h.shape
    nb = s // CMB_TB
    pos3 = pos.reshape(nb, 1, CMB_TB * TOP_K)
    pos_spec = lambda f: pl.BlockSpec((1, 1, CMB_TB * TOP_K), f, memory_space=pltpu.SMEM)
    return pl.pallas_call(
        _combine_kernel,
        grid=(nb,),
        in_specs=[pos_spec(lambda i: (i, 0, 0)),
                  pos_spec(lambda i: (jnp.minimum(i + 1, nb - 1), 0, 0)),
                  pl.BlockSpec(memory_space=pl.ANY),
                  pl.BlockSpec((CMB_TB, d), lambda i: (i, 0)),
                  pl.BlockSpec((CMB_TB, LANES), lambda i: (i, 0)),
                  pl.BlockSpec((1, d), lambda i: (0, 0))],
        out_specs=[pl.BlockSpec((CMB_TB, d), lambda i: (i, 0)),
                   pl.BlockSpec((CMB_TB, d), lambda i: (i, 0))],
        out_shape=[jax.ShapeDtypeStruct((s, d), F32), jax.ShapeDtypeStruct((s, d), BF16)],
        scratch_shapes=[pltpu.VMEM((2, TOP_K, CMB_TB, d // LANES, LANES), BF16),
                        pltpu.SemaphoreType.DMA((2,))],
        compiler_params=_params(("arbitrary",)),
        name="combine",
    )(pos3, pos3, ys, h, gates, g.reshape(1, d))


def moe_routing(top_e, n_tokens):
    e_flat = top_e.reshape(-1)
    onehot = (e_flat[:, None] == jnp.arange(N_EXPERTS, dtype=jnp.int32)[None, :]).astype(jnp.int32)
    csum = jnp.cumsum(onehot, axis=0)
    counts = csum[-1]
    rank = jnp.sum(csum * onehot, axis=1) - 1
    padded = (counts + MOE_TM - 1) // MOE_TM * MOE_TM
    pends = jnp.cumsum(padded).astype(jnp.int32)
    pstarts = pends - padded
    pos = pstarts[e_flat] + rank
    n_blocks = (n_tokens * TOP_K) // MOE_TM + N_EXPERTS
    tok_flat = jnp.arange(n_tokens * TOP_K, dtype=jnp.int32) // TOP_K
    row_tok = jnp.zeros((n_blocks * MOE_TM,), jnp.int32).at[pos].set(
        tok_flat, unique_indices=True, mode="promise_in_bounds")
    n_used = pends[-1] // MOE_TM
    blk_start = jnp.arange(n_blocks, dtype=jnp.int32) * MOE_TM
    block_e = jnp.sum((pends[None, :] <= blk_start[:, None]).astype(jnp.int32), axis=1)
    block_e = jnp.minimum(block_e, N_EXPERTS - 1)
    last_e = block_e[jnp.maximum(n_used - 1, 0)]
    used = jnp.arange(n_blocks) < n_used
    real_end = pstarts + counts
    block_valid = jnp.clip(real_end[block_e] - blk_start, 0, MOE_TM)
    block_valid = jnp.where(used, block_valid, 0).astype(jnp.int32)
    block_e = jnp.where(used, block_e, last_e).astype(jnp.int32)
    has_rows = counts > 0
    eidx = jnp.arange(N_EXPERTS, dtype=jnp.int32)
    later = jnp.logical_and(eidx[None, :] > eidx[:, None], has_rows[None, :])
    next_of = jnp.min(jnp.where(later, eidx[None, :], N_EXPERTS), axis=1)
    next_of = jnp.where(next_of < N_EXPERTS, next_of, -1)
    order_of = jnp.cumsum(has_rows.astype(jnp.int32)) - 1
    routing = dict(block_e=block_e, block_valid=block_valid,
                   n_used=n_used.reshape(1).astype(jnp.int32),
                   switch_idx=order_of[block_e].astype(jnp.int32),
                   next_e=next_of[block_e].astype(jnp.int32),
                   n_switch=jnp.sum(has_rows.astype(jnp.int32)).reshape(1))
    return pos.astype(jnp.int32), row_tok, routing


def kernel(x, p, g_mix, w_in, attn_sink, na_rpb, w_branch_a, w_branch_b, w_out, g_ffn, w_router,
           b_router, w_gate_up, b_gate_up, w_down, b_down, g_ple, w_ple_gate, w_ple, g_final):
    b, s, d = x.shape
    assert b == 1 and w_in.shape[0] == 1
    h = x.reshape(s, d)
    rope_c, rope_s1, rope_s2 = rope_tables(s)

    xn = rmsnorm_call(h, g_mix[0], BF16)
    proj = inproj_call(xn, w_in[0], rope_c, rope_s1, rope_s2, d)
    ya = window_attn_call(proj, attn_sink[0])
    yb = na_attn_call(proj, na_bias_tables(na_rpb[0]))
    merged = merge_call(ya, yb, w_branch_a[0], w_branch_b[0], proj, d)
    h = resmm_call(merged, w_out[0], h)

    xn2, top_e, gates = rms_router_call(h, g_ffn[0], w_router[0], b_router[0])
    pos, row_tok, rt = moe_routing(top_e[:, :TOP_K], s)
    xs = dispatch_call(rt["n_used"], row_tok, xn2)
    sched = (rt["block_e"], rt["block_valid"], rt["n_used"], rt["switch_idx"], rt["next_e"], rt["n_switch"])
    act = gateup_call(*sched, xs, w_gate_up[0], b_gate_up[0])
    ys = down_call(*sched, act, w_down[0], b_down[0])
    h, xn3 = combine_call(pos, ys, h, gates, g_ple[0])

    h = ple_call(xn3, w_ple_gate[0], p[0].reshape(s, -1), w_ple[0], h)
    out = rmsnorm_call(h, g_final, F32)
    return out.reshape(b, s, d)
```

```python
import functools

import numpy as np
import jax
import jax.numpy as jnp
from jax import lax
from jax.experimental import pallas as pl
from jax.experimental.pallas import tpu as pltpu

F32 = jnp.float32
BF16 = jnp.bfloat16

HEAD_DIM = 64
A_HEADS = 32
A_KV_HEADS = 8
A_GROUP = A_HEADS // A_KV_HEADS
B_HEADS = 32
A_Q_W = A_HEADS * HEAD_DIM
A_KV_W = A_KV_HEADS * HEAD_DIM
B_W = B_HEADS * HEAD_DIM
ROT_DIM = HEAD_DIM // 4
ROPE_THETA = 500000.0
WINDOW = 128
GRID_W = 64
NA_ROWS = 8
NA_COLS = 16
N_EXPERTS = 32
TOP_K = 4
SWIGLU_LIMIT = 7.0
SWIGLU_ALPHA = 1.702
RMS_EPS = 1e-6
NEG_INF = -1e30

LANES = 128
VMEM_LIMIT = 56 * 1024 * 1024

RMS_ROWS = 256
MM_TM = 1024
MM_TN = 512
ATT_BLK = 128
NA_RB = 8
NA_HPS = 4
MOE_TM = 512
MOE_TF = 256
MOE_TN = 2048
CMB_TB = 128
DMA_UNROLL = 8
ROW_DMA_PRIORITY = 1


def _sigmoid(x):
    return 0.5 * jnp.tanh(0.5 * x) + 0.5


def _params(sem):
    return pltpu.CompilerParams(dimension_semantics=sem, vmem_limit_bytes=VMEM_LIMIT)


def _rms_body(x_ref, g_ref):
    x = x_ref[...].astype(F32)
    ms = jnp.mean(x * x, axis=-1, keepdims=True)
    return x * lax.rsqrt(ms + RMS_EPS) * g_ref[...]


def _rms_kernel(x_ref, g_ref, o_ref):
    o_ref[...] = _rms_body(x_ref, g_ref).astype(o_ref.dtype)


def rmsnorm_call(x, g, out_dtype):
    s, d = x.shape
    return pl.pallas_call(
        _rms_kernel,
        grid=(s // RMS_ROWS,),
        in_specs=[pl.BlockSpec((RMS_ROWS, d), lambda i: (i, 0)),
                  pl.BlockSpec((1, d), lambda i: (0, 0))],
        out_specs=pl.BlockSpec((RMS_ROWS, d), lambda i: (i, 0)),
        out_shape=jax.ShapeDtypeStruct((s, d), out_dtype),
        compiler_params=_params(("parallel",)),
        name="rmsnorm",
    )(x, g.reshape(1, d))


def _to_token_rows(x2d):
    return x2d.reshape(x2d.shape[0], x2d.shape[1] // LANES, LANES)


def _split_bf16(x):
    hi = x.astype(BF16)
    return hi, (x - hi.astype(F32)).astype(BF16)


def _rms_router_kernel(x_ref, g_ref, wh_ref, wl_ref, br_ref, xn_ref, e_ref, gt_ref):
    xn = _rms_body(x_ref, g_ref)
    xh, xl = _split_bf16(xn)
    xn_ref[...] = _to_token_rows(xh)
    logits = (jnp.dot(xh, wh_ref[...], preferred_element_type=F32)
              + jnp.dot(xh, wl_ref[...], preferred_element_type=F32)
              + jnp.dot(xl, wh_ref[...], preferred_element_type=F32)) + br_ref[...]
    lane = lax.broadcasted_iota(jnp.int32, logits.shape, 1)
    logits = jnp.where(lane < N_EXPERTS, logits, -jnp.inf)
    e_out = jnp.zeros(logits.shape, jnp.int32)
    v_out = jnp.zeros(logits.shape, F32)
    v0 = None
    for k in range(TOP_K):
        m = jnp.max(logits, axis=-1, keepdims=True)
        idx = jnp.min(jnp.where(logits == m, lane, LANES), axis=-1, keepdims=True)
        if k == 0:
            v0 = m
        e_out = jnp.where(lane == k, idx, e_out)
        v_out = jnp.where(lane == k, jnp.exp(m - v0), v_out)
        logits = jnp.where(lane == idx, -jnp.inf, logits)
    e_ref[...] = e_out
    gt_ref[...] = v_out / jnp.sum(v_out, axis=-1, keepdims=True)


def rms_router_call(h, g, w_router, b_router):
    s, d = h.shape
    wr = jnp.zeros((d, LANES), F32).at[:, :N_EXPERTS].set(w_router)
    br = jnp.zeros((1, LANES), F32).at[0, :N_EXPERTS].set(b_router)
    wr_hi, wr_lo = _split_bf16(wr)
    return pl.pallas_call(
        _rms_router_kernel,
        grid=(s // RMS_ROWS,),
        in_specs=[pl.BlockSpec((RMS_ROWS, d), lambda i: (i, 0)),
                  pl.BlockSpec((1, d), lambda i: (0, 0)),
                  pl.BlockSpec((d, LANES), lambda i: (0, 0)),
                  pl.BlockSpec((d, LANES), lambda i: (0, 0)),
                  pl.BlockSpec((1, LANES), lambda i: (0, 0))],
        out_specs=[pl.BlockSpec((RMS_ROWS, d // LANES, LANES), lambda i: (i, 0, 0)),
                   pl.BlockSpec((RMS_ROWS, LANES), lambda i: (i, 0)),
                   pl.BlockSpec((RMS_ROWS, LANES), lambda i: (i, 0))],
        out_shape=[jax.ShapeDtypeStruct((s, d // LANES, LANES), BF16),
                   jax.ShapeDtypeStruct((s, LANES), jnp.int32),
                   jax.ShapeDtypeStruct((s, LANES), F32)],
        compiler_params=_params(("parallel",)),
        name="rms_router",
    )(h, g.reshape(1, d), wr_hi, wr_lo, br)


def _rope(acc, c_ref, s1_ref, s2_ref):
    c, s1, s2 = c_ref[...], s1_ref[...], s2_ref[...]
    half = ROT_DIM // 2
    outs = []
    for t in range(acc.shape[1] // LANES):
        x = acc[:, t * LANES:(t + 1) * LANES]
        outs.append(x * c + pltpu.roll(x, LANES - half, 1) * s1 + pltpu.roll(x, half, 1) * s2)
    return jnp.concatenate(outs, axis=1)


def _stream_weight_chunk(w_hbm, wbuf, sem, wb_ref):
    j = pl.program_id(0)
    nj = pl.num_programs(0)
    tn = wbuf.shape[2]

    def chunk_copy(jj, slot):
        return pltpu.make_async_copy(w_hbm.at[:, pl.ds(pl.multiple_of(jj * tn, tn), tn)],
                                     wbuf.at[slot], sem.at[slot])

    @pl.when(pl.program_id(1) == 0)
    def _():
        slot = j % 2

        @pl.when(j == 0)
        def _():
            chunk_copy(0, 0).start()

        chunk_copy(j, slot).wait()

        @pl.when(j + 1 < nj)
        def _():
            chunk_copy(j + 1, 1 - slot).start()

        wb_ref[...] = wbuf[slot].astype(BF16)


def _inproj_kernel(a_ref, w_hbm, c_ref, s1_ref, s2_ref, o_ref, wb_ref, wbuf, wsem, *, n_rope, n_plain):
    j = pl.program_id(0)
    _stream_weight_chunk(w_hbm, wbuf, wsem, wb_ref)
    acc = jnp.dot(a_ref[...], wb_ref[...], preferred_element_type=F32)

    @pl.when(j < n_rope)
    def _():
        o_ref[...] = _rope(acc, c_ref, s1_ref, s2_ref).astype(o_ref.dtype)

    @pl.when(jnp.logical_and(j >= n_rope, j < n_plain))
    def _():
        o_ref[...] = acc.astype(o_ref.dtype)

    @pl.when(j >= n_plain)
    def _():
        o_ref[...] = _sigmoid(acc).astype(o_ref.dtype)


def inproj_call(xn, w_in, rope_c, rope_s1, rope_s2, d_model):
    s, d = xn.shape
    n = w_in.shape[1]
    n_rope = (A_Q_W + A_KV_W) // MM_TN
    n_plain = (n - 2 * d_model) // MM_TN
    tab = pl.BlockSpec((MM_TM, LANES), lambda j, i: (i, 0))
    return pl.pallas_call(
        functools.partial(_inproj_kernel, n_rope=n_rope, n_plain=n_plain),
        grid=(n // MM_TN, s // MM_TM),
        in_specs=[pl.BlockSpec((MM_TM, d), lambda j, i: (i, 0)),
                  pl.BlockSpec(memory_space=pl.ANY),
                  tab, tab, tab],
        out_specs=pl.BlockSpec((MM_TM, MM_TN), lambda j, i: (i, j)),
        out_shape=jax.ShapeDtypeStruct((s, n), BF16),
        scratch_shapes=[pltpu.VMEM((d, MM_TN), BF16), pltpu.VMEM((2, d, MM_TN), F32),
                        pltpu.SemaphoreType.DMA((2,))],
        compiler_params=_params(("arbitrary", "arbitrary")),
        name="inproj",
    )(xn, w_in, rope_c, rope_s1, rope_s2)


def rope_tables(s):
    half = ROT_DIM // 2
    inv = jnp.float32(ROPE_THETA) ** (-jnp.arange(half, dtype=F32) * (2.0 / ROT_DIM))
    ang = jnp.arange(s, dtype=F32)[:, None] * inv[None, :]
    cos, sin = jnp.cos(ang), jnp.sin(ang)
    ones = jnp.ones((s, HEAD_DIM - ROT_DIM), F32)
    zeros = jnp.zeros((s, HEAD_DIM - ROT_DIM), F32)
    zh = jnp.zeros((s, half), F32)
    c = jnp.concatenate([cos, cos, ones], axis=1)
    s1 = jnp.concatenate([-sin, zh, zeros], axis=1)
    s2 = jnp.concatenate([zh, sin, zeros], axis=1)
    rep = LANES // HEAD_DIM
    return jnp.tile(c, (1, rep)), jnp.tile(s1, (1, rep)), jnp.tile(s2, (1, rep))


def _win_kernel(sink_ref, q_ref, kp_ref, kc_ref, kn_ref, vp_ref, vc_ref, vn_ref, o_ref, *, seq):
    n = pl.program_id(0)
    pr = pl.program_id(1)
    kcat = jnp.concatenate([kp_ref[...], kc_ref[...], kn_ref[...]], axis=0)
    vcat = jnp.concatenate([vp_ref[...], vc_ref[...], vn_ref[...]], axis=0)
    q = q_ref[...] * jnp.asarray(HEAD_DIM ** -0.5, BF16)
    rows = A_GROUP * ATT_BLK
    nk = 3 * ATT_BLK
    qpos = n * ATT_BLK + lax.broadcasted_iota(jnp.int32, (ATT_BLK, nk), 0)
    kpos = (n - 1) * ATT_BLK + lax.broadcasted_iota(jnp.int32, (ATT_BLK, nk), 1)
    mask = ((jnp.abs(qpos - kpos) <= WINDOW) & (kpos >= 0) & (kpos < seq))[None]
    grp = lax.broadcasted_iota(jnp.int32, (A_GROUP, 1, 1), 0)
    kv_per_blk = LANES // HEAD_DIM
    for kvh in range(kv_per_blk):
        k_h = kcat[:, kvh * HEAD_DIM:(kvh + 1) * HEAD_DIM]
        v_h = vcat[:, kvh * HEAD_DIM:(kvh + 1) * HEAD_DIM]
        h0 = kvh * A_GROUP
        qs = jnp.concatenate(
            [q[:, (h0 + g) * HEAD_DIM:(h0 + g + 1) * HEAD_DIM] for g in range(A_GROUP)], axis=0)
        sc = lax.dot_general(qs, k_h, (((1,), (1,)), ((), ())), preferred_element_type=F32)
        sc = jnp.where(mask, sc.reshape(A_GROUP, ATT_BLK, nk), NEG_INF)
        snk = jnp.zeros((A_GROUP, 1, 1), F32)
        for g in range(A_GROUP):
            sv = sink_ref[(pr * kv_per_blk + kvh) * A_GROUP + g]
            snk = jnp.where(grp == g, sv, snk)
        m = jnp.maximum(jnp.max(sc, axis=-1, keepdims=True), snk)
        p = jnp.exp(sc - m)
        denom = jnp.sum(p, axis=-1, keepdims=True) + jnp.exp(snk - m)
        attn = (p * (1.0 / denom)).astype(BF16).reshape(rows, nk)
        out = jnp.dot(attn, v_h, preferred_element_type=F32)
        for g in range(A_GROUP):
            o_ref[:, (h0 + g) * HEAD_DIM:(h0 + g + 1) * HEAD_DIM] = (
                out[g * ATT_BLK:(g + 1) * ATT_BLK].astype(o_ref.dtype))


def window_attn_call(proj, sink):
    s = proj.shape[0]
    nb = s // ATT_BLK
    qw = A_GROUP * LANES
    k0 = A_Q_W // LANES
    v0 = (A_Q_W + A_KV_W) // LANES
    blk = (ATT_BLK, LANES)
    prev = lambda n: jnp.maximum(n - 1, 0)
    nxt = lambda n: jnp.minimum(n + 1, nb - 1)
    return pl.pallas_call(
        functools.partial(_win_kernel, seq=s),
        grid=(nb, A_Q_W // qw),
        in_specs=[pl.BlockSpec(memory_space=pltpu.SMEM),
                  pl.BlockSpec((ATT_BLK, qw), lambda n, p: (n, p)),
                  pl.BlockSpec(blk, lambda n, p: (prev(n), k0 + p)),
                  pl.BlockSpec(blk, lambda n, p: (n, k0 + p)),
                  pl.BlockSpec(blk, lambda n, p: (nxt(n), k0 + p)),
                  pl.BlockSpec(blk, lambda n, p: (prev(n), v0 + p)),
                  pl.BlockSpec(blk, lambda n, p: (n, v0 + p)),
                  pl.BlockSpec(blk, lambda n, p: (nxt(n), v0 + p))],
        out_specs=pl.BlockSpec((ATT_BLK, qw), lambda n, p: (n, p)),
        out_shape=jax.ShapeDtypeStruct((s, A_Q_W), BF16),
        compiler_params=_params(("parallel", "parallel")),
        name="window_attn",
    )(sink, proj, proj, proj, proj, proj, proj, proj)


def na_bias_tables(rpb):
    c = np.arange(GRID_W)
    kc = np.arange(GRID_W)
    dc = np.clip(kc[None, :] - c[:, None] + NA_COLS - 1, 0, 2 * NA_COLS - 2)
    qcs = np.clip(c - NA_COLS // 2, 0, GRID_W - NA_COLS)
    cmask = (kc[None, :] >= qcs[:, None]) & (kc[None, :] < qcs[:, None] + NA_COLS)
    t = jnp.where(jnp.asarray(cmask)[None, None], rpb[:, :, dc].astype(F32), NEG_INF)
    t = jnp.transpose(t, (0, 2, 1, 3)).reshape(rpb.shape[0], GRID_W, (2 * NA_ROWS - 1) * GRID_W)
    return jnp.pad(t, ((0, 0), (0, 0), (GRID_W, 0)))


def _na_kernel(q_ref, kp_ref, kc_ref, kn_ref, vp_ref, vc_ref, vn_ref, b_ref, o_ref,
               kbuf, vbuf, s_scr, p_scr, *, grid_rows):
    rb = pl.program_id(1)
    blk = NA_RB * GRID_W
    kbuf[0:blk] = kp_ref[...]
    kbuf[blk:2 * blk] = kc_ref[...]
    kbuf[2 * blk:3 * blk] = kn_ref[...]
    vbuf[0:blk] = vp_ref[...]
    vbuf[blk:2 * blk] = vc_ref[...]
    vbuf[2 * blk:3 * blk] = vn_ref[...]
    nkeys = NA_ROWS * GRID_W
    width = NA_HPS * HEAD_DIM
    grp = NA_HPS * GRID_W
    row_head = lax.broadcasted_iota(jnp.int32, (grp, width), 0) // GRID_W
    lane_head = lax.broadcasted_iota(jnp.int32, (grp, width), 1) // HEAD_DIM
    own = row_head == lane_head
    out_head = lax.broadcasted_iota(jnp.int32, (GRID_W, width), 1) // HEAD_DIM
    nrb = grid_rows // NA_RB
    half = NA_ROWS // 2

    def rows(first_rb):
        starts = []
        for i in range(NA_RB):
            r = first_rb * NA_RB + i
            rs = min(max(r - half, 0), grid_rows - NA_ROWS)
            typ = rs - (r - half) + (half - 1)
            start = (rs - (first_rb - 1) * NA_RB) * GRID_W
            starts.append(start)
            qi = q_ref[i * GRID_W:(i + 1) * GRID_W, :] * jnp.asarray(HEAD_DIM ** -0.5, BF16)
            qs = jnp.where(own, jnp.concatenate([qi] * NA_HPS, axis=0), jnp.zeros((grp, width), BF16))
            sc = lax.dot_general(qs, kbuf[start:start + nkeys, :], (((1,), (1,)), ((), ())),
                                 preferred_element_type=F32)
            boff = (typ + 1) * GRID_W
            s_scr[i * grp:(i + 1) * grp, :] = sc + b_ref[:, :, boff:boff + nkeys].reshape(grp, nkeys)
        for i in range(NA_RB):
            sc = s_scr[i * grp:(i + 1) * grp, :]
            m = jnp.max(sc, axis=-1, keepdims=True)
            p = jnp.exp(sc - m)
            p_scr[i * grp:(i + 1) * grp, :] = (
                p * (1.0 / jnp.sum(p, axis=-1, keepdims=True))).astype(BF16)
        for i in range(NA_RB):
            start = starts[i]
            out = jnp.dot(p_scr[i * grp:(i + 1) * grp, :], vbuf[start:start + nkeys, :],
                          preferred_element_type=F32)
            res = out[:GRID_W]
            for h in range(1, NA_HPS):
                res = jnp.where(out_head == h, out[h * GRID_W:(h + 1) * GRID_W], res)
            o_ref[i * GRID_W:(i + 1) * GRID_W, :] = res.astype(o_ref.dtype)

    @pl.when(rb == 0)
    def _():
        rows(0)

    @pl.when(rb == nrb - 1)
    def _():
        rows(nrb - 1)

    @pl.when(jnp.logical_and(rb > 0, rb < nrb - 1))
    def _():
        rows(1)


def na_attn_call(proj, bias_tab):
    s = proj.shape[0]
    grid_rows = s // GRID_W
    nrb = grid_rows // NA_RB
    blk_rows = NA_RB * GRID_W
    width = NA_HPS * HEAD_DIM
    q0 = (A_Q_W + 2 * A_KV_W) // width
    k0 = q0 + B_W // width
    v0 = k0 + B_W // width
    blk = (blk_rows, width)
    hpb = NA_HPS
    prev = lambda r: jnp.maximum(r - 1, 0)
    cur = lambda r: r
    nxt = lambda r: jnp.minimum(r + 1, nrb - 1)
    return pl.pallas_call(
        functools.partial(_na_kernel, grid_rows=grid_rows),
        grid=(B_HEADS // hpb, nrb),
        in_specs=[pl.BlockSpec(blk, lambda h, r: (r, q0 + h)),
                  pl.BlockSpec(blk, lambda h, r: (prev(r), k0 + h)),
                  pl.BlockSpec(blk, lambda h, r: (cur(r), k0 + h)),
                  pl.BlockSpec(blk, lambda h, r: (nxt(r), k0 + h)),
                  pl.BlockSpec(blk, lambda h, r: (prev(r), v0 + h)),
                  pl.BlockSpec(blk, lambda h, r: (cur(r), v0 + h)),
                  pl.BlockSpec(blk, lambda h, r: (nxt(r), v0 + h)),
                  pl.BlockSpec((hpb, GRID_W, 2 * NA_ROWS * GRID_W), lambda h, r: (h, 0, 0))],
        out_specs=pl.BlockSpec(blk, lambda h, r: (r, h)),
        out_shape=jax.ShapeDtypeStruct((s, B_W), BF16),
        scratch_shapes=[pltpu.VMEM((3 * blk_rows, width), BF16),
                        pltpu.VMEM((3 * blk_rows, width), BF16),
                        pltpu.VMEM((NA_RB * hpb * GRID_W, NA_ROWS * GRID_W), F32),
                        pltpu.VMEM((NA_RB * hpb * GRID_W, NA_ROWS * GRID_W), BF16)],
        compiler_params=_params(("parallel", "arbitrary")),
        name="na_attn",
    )(proj, proj, proj, proj, proj, proj, proj, bias_tab)


def _merge_kernel(ya_ref, yb_ref, wa_ref, wb_ref, ga_ref, gb_ref, o_ref, wab, wbb):
    @pl.when(pl.program_id(1) == 0)
    def _():
        wab[...] = wa_ref[...].astype(BF16)
        wbb[...] = wb_ref[...].astype(BF16)

    a = jnp.dot(ya_ref[...], wab[...], preferred_element_type=F32)
    b = jnp.dot(yb_ref[...], wbb[...], preferred_element_type=F32)
    o_ref[...] = (ga_ref[...].astype(F32) * a + gb_ref[...].astype(F32) * b).astype(o_ref.dtype)


def merge_call(ya, yb, w_a, w_b, proj, d_model):
    s = ya.shape[0]
    ga0 = (proj.shape[1] - 2 * d_model) // MM_TN
    gb0 = ga0 + d_model // MM_TN
    return pl.pallas_call(
        _merge_kernel,
        grid=(d_model // MM_TN, s // MM_TM),
        in_specs=[pl.BlockSpec((MM_TM, A_Q_W), lambda j, i: (i, 0)),
                  pl.BlockSpec((MM_TM, B_W), lambda j, i: (i, 0)),
                  pl.BlockSpec((A_Q_W, MM_TN), lambda j, i: (0, j)),
                  pl.BlockSpec((B_W, MM_TN), lambda j, i: (0, j)),
                  pl.BlockSpec((MM_TM, MM_TN), lambda j, i: (i, ga0 + j)),
                  pl.BlockSpec((MM_TM, MM_TN), lambda j, i: (i, gb0 + j))],
        out_specs=pl.BlockSpec((MM_TM, MM_TN), lambda j, i: (i, j)),
        out_shape=jax.ShapeDtypeStruct((s, d_model), BF16),
        scratch_shapes=[pltpu.VMEM((A_Q_W, MM_TN), BF16), pltpu.VMEM((B_W, MM_TN), BF16)],
        compiler_params=_params(("parallel", "arbitrary")),
        name="merge",
    )(ya, yb, w_a, w_b, proj, proj)


def _resmm_kernel(a_ref, w_hbm, r_ref, o_ref, wb_ref, wbuf, wsem):
    _stream_weight_chunk(w_hbm, wbuf, wsem, wb_ref)
    o_ref[...] = r_ref[...] + jnp.dot(a_ref[...], wb_ref[...], preferred_element_type=F32)


def resmm_call(a, w, res):
    s, k = a.shape
    n = w.shape[1]
    return pl.pallas_call(
        _resmm_kernel,
        grid=(n // MM_TN, s // MM_TM),
        in_specs=[pl.BlockSpec((MM_TM, k), lambda j, i: (i, 0)),
                  pl.BlockSpec(memory_space=pl.ANY),
                  pl.BlockSpec((MM_TM, MM_TN), lambda j, i: (i, j))],
        out_specs=pl.BlockSpec((MM_TM, MM_TN), lambda j, i: (i, j)),
        out_shape=jax.ShapeDtypeStruct((s, n), F32),
        scratch_shapes=[pltpu.VMEM((k, MM_TN), BF16), pltpu.VMEM((2, k, MM_TN), F32),
                        pltpu.SemaphoreType.DMA((2,))],
        compiler_params=_params(("arbitrary", "arbitrary")),
        name="resmm",
    )(a, w, res)


def _ple_kernel(a_ref, wg_hbm, p_ref, wp_ref, h_ref, o_ref, wgb, wpb, wbuf, wsem):
    _stream_weight_chunk(wg_hbm, wbuf, wsem, wgb)

    @pl.when(pl.program_id(1) == 0)
    def _():
        wpb[...] = wp_ref[...].astype(BF16)

    gate = _sigmoid(jnp.dot(a_ref[...], wgb[...], preferred_element_type=F32))
    emb = jnp.dot(p_ref[...].astype(BF16), wpb[...], preferred_element_type=F32)
    o_ref[...] = h_ref[...] + gate * emb


def ple_call(xn, w_gate, p, w_ple, h):
    s, d = xn.shape
    pd = p.shape[1]
    return pl.pallas_call(
        _ple_kernel,
        grid=(d // MM_TN, s // MM_TM),
        in_specs=[pl.BlockSpec((MM_TM, d), lambda j, i: (i, 0)),
                  pl.BlockSpec(memory_space=pl.ANY),
                  pl.BlockSpec((MM_TM, pd), lambda j, i: (i, 0)),
                  pl.BlockSpec((pd, MM_TN), lambda j, i: (0, j)),
                  pl.BlockSpec((MM_TM, MM_TN), lambda j, i: (i, j))],
        out_specs=pl.BlockSpec((MM_TM, MM_TN), lambda j, i: (i, j)),
        out_shape=jax.ShapeDtypeStruct((s, d), F32),
        scratch_shapes=[pltpu.VMEM((d, MM_TN), BF16), pltpu.VMEM((pd, MM_TN), BF16),
                        pltpu.VMEM((2, d, MM_TN), F32), pltpu.SemaphoreType.DMA((2,))],
        compiler_params=_params(("arbitrary", "arbitrary")),
        name="ple",
    )(xn, w_gate, p, w_ple, h)


def _swiglu_chunk(x, wb, bgu_ref):
    hgu = jnp.dot(x, wb[...], preferred_element_type=F32) + bgu_ref[...]
    gate = jnp.minimum(hgu, SWIGLU_LIMIT)
    up = jnp.clip(hgu, -SWIGLU_LIMIT, SWIGLU_LIMIT)
    up = pltpu.roll(up, 2 * MOE_TF - 1, 1)
    act = gate * _sigmoid(gate * SWIGLU_ALPHA) * (up + 1.0)
    rr = lax.broadcasted_iota(jnp.int32, (2 * MOE_TF, MOE_TF), 0)
    cc = lax.broadcasted_iota(jnp.int32, (2 * MOE_TF, MOE_TF), 1)
    sel = (rr == 2 * cc).astype(BF16)
    return jnp.dot(act.astype(BF16), sel, preferred_element_type=F32)


def _dispatch_gateup_kernel(be_ref, bv_ref, nused_ref, sw_ref, nx_ref, nsw_ref, tok_ref, tokn_ref,
                            x_hbm, w_hbm, bgu_ref, xs_ref, act_ref, gbuf, gsem, wb, wbuf, wsem):
    m = pl.program_id(1)
    nused = nused_ref[0]
    slot = m % 2
    _stream_expert_chunk(be_ref, bv_ref, sw_ref, nx_ref, nsw_ref, w_hbm, wbuf, wsem, wb)

    def row_copy(tok, sl, r):
        return pltpu.make_async_copy(x_hbm.at[tok], gbuf.at[sl, r], gsem.at[sl])

    def issue(t_ref, sl):
        def body(g, c):
            rows = [g * DMA_UNROLL + u for u in range(DMA_UNROLL)]
            toks = [t_ref[0, 0, r] for r in rows]
            for u in range(DMA_UNROLL):
                row_copy(toks[u], sl, rows[u]).start(priority=ROW_DMA_PRIORITY)
            return c

        lax.fori_loop(0, MOE_TM // DMA_UNROLL, body, 0)

    @pl.when(m == 0)
    def _():
        issue(tok_ref, 0)

    @pl.when(m + 1 < nused)
    def _():
        issue(tokn_ref, 1 - slot)

    @pl.when(m < nused)
    def _():
        pltpu.make_async_copy(x_hbm.at[pl.ds(0, MOE_TM)], gbuf.at[slot], gsem.at[slot]).wait()
        x = gbuf[slot].reshape(xs_ref.shape)
        xs_ref[...] = x
        act_ref[...] = _swiglu_chunk(x, wb, bgu_ref).astype(act_ref.dtype)

    @pl.when(m >= nused)
    def _():
        xs_ref[...] = jnp.zeros(xs_ref.shape, xs_ref.dtype)
        act_ref[...] = jnp.zeros(act_ref.shape, act_ref.dtype)


def dispatch_gateup_call(sched, row_tok, xn3d, w_gu, b_gu):
    s, dg, _ = xn3d.shape
    e, d, f2 = w_gu.shape
    n_rows = row_tok.shape[0]
    nblk = n_rows // MOE_TM
    tok3 = row_tok.reshape(nblk, 1, MOE_TM)
    tok_spec = lambda f: pl.BlockSpec((1, 1, MOE_TM), f, memory_space=pltpu.SMEM)
    return pl.pallas_call(
        _dispatch_gateup_kernel,
        grid_spec=pltpu.PrefetchScalarGridSpec(
            num_scalar_prefetch=6,
            grid=(1, nblk),
            in_specs=[tok_spec(lambda j, m, *_: (m, 0, 0)),
                      tok_spec(lambda j, m, *_: (jnp.minimum(m + 1, nblk - 1), 0, 0)),
                      pl.BlockSpec(memory_space=pl.ANY),
                      pl.BlockSpec(memory_space=pl.ANY),
                      pl.BlockSpec((None, 1, 2 * MOE_TF), lambda j, m, be, *_: (be[m], 0, 0))],
            out_specs=[pl.BlockSpec((MOE_TM, d), lambda j, m, *_: (m, 0)),
                       pl.BlockSpec((MOE_TM, MOE_TF), lambda j, m, *_: (m, 0))],
            scratch_shapes=[pltpu.VMEM((2, MOE_TM, dg, LANES), BF16), pltpu.SemaphoreType.DMA((2,)),
                            pltpu.VMEM((d, 2 * MOE_TF), BF16), pltpu.VMEM((2, d, 2 * MOE_TF), F32),
                            pltpu.SemaphoreType.DMA((2,))]),
        out_shape=[jax.ShapeDtypeStruct((n_rows, d), BF16),
                   jax.ShapeDtypeStruct((n_rows, MOE_TF), BF16)],
        compiler_params=_params(("arbitrary", "arbitrary")),
        name="dispatch_gateup0",
    )(*sched, tok3, tok3, xn3d, w_gu, b_gu.reshape(e, 1, f2))


def _expert_changed(be_ref, m):
    return jnp.logical_or(m == 0, be_ref[m] != be_ref[jnp.maximum(m - 1, 0)])


def _for_live_block(bv_ref, m, o_ref, compute):
    live = bv_ref[m] > 0

    @pl.when(live)
    def _():
        compute(slice(None))

    @pl.when(jnp.logical_not(live))
    def _():
        o_ref[...] = jnp.zeros(o_ref.shape, o_ref.dtype)


def _stream_expert_chunk(be_ref, bv_ref, sw_ref, nx_ref, nsw_ref, w_hbm, wbuf, wsem, wb, chunk0=0):
    j = pl.program_id(0)
    m = pl.program_id(1)
    nj = pl.num_programs(0)
    tn = wbuf.shape[2]

    def chunk_copy(e, jj, slot):
        col = pl.multiple_of((jj + chunk0) * tn, tn)
        return pltpu.make_async_copy(w_hbm.at[e, :, pl.ds(col, tn)], wbuf.at[slot], wsem.at[slot])

    @pl.when(jnp.logical_and(bv_ref[m] > 0, _expert_changed(be_ref, m)))
    def _():
        k = j * nsw_ref[0] + sw_ref[m]
        slot = k % 2

        @pl.when(k == 0)
        def _():
            chunk_copy(be_ref[0], 0, 0).start()

        chunk_copy(be_ref[m], j, slot).wait()
        more_experts = nx_ref[m] >= 0

        @pl.when(more_experts)
        def _():
            chunk_copy(nx_ref[m], j, 1 - slot).start()

        @pl.when(jnp.logical_and(jnp.logical_not(more_experts), j + 1 < nj))
        def _():
            chunk_copy(be_ref[0], j + 1, 1 - slot).start()

        wb[...] = wbuf[slot].astype(BF16)


def _gateup_kernel(be_ref, bv_ref, nused_ref, sw_ref, nx_ref, nsw_ref, x_ref, w_hbm, bgu_ref, o_ref,
                   wb, wbuf, wsem):
    m = pl.program_id(1)
    _stream_expert_chunk(be_ref, bv_ref, sw_ref, nx_ref, nsw_ref, w_hbm, wbuf, wsem, wb, chunk0=1)

    def compute_rows(rows):
        o_ref[rows] = _swiglu_chunk(x_ref[rows], wb, bgu_ref).astype(o_ref.dtype)

    _for_live_block(bv_ref, m, o_ref, compute_rows)


def gateup_call(sched, xs, w_gu, b_gu):
    n_rows, dh = xs.shape
    e, d, f2 = w_gu.shape
    f = f2 // 2
    nblk = n_rows // MOE_TM
    npass = f // MOE_TF - 1

    return pl.pallas_call(
        _gateup_kernel,
        grid_spec=pltpu.PrefetchScalarGridSpec(
            num_scalar_prefetch=6,
            grid=(npass, nblk),
            in_specs=[pl.BlockSpec((MOE_TM, dh), lambda j, m, be, bv, nu, *_: (jnp.minimum(m, nu[0] - 1), 0)),
                      pl.BlockSpec(memory_space=pl.ANY),
                      pl.BlockSpec((None, 1, 2 * MOE_TF), lambda j, m, be, *_: (be[m], 0, j + 1))],
            out_specs=pl.BlockSpec((MOE_TM, MOE_TF), lambda j, m, *_: (m, j)),
            scratch_shapes=[pltpu.VMEM((d, 2 * MOE_TF), BF16), pltpu.VMEM((2, d, 2 * MOE_TF), F32),
                            pltpu.SemaphoreType.DMA((2,))]),
        out_shape=jax.ShapeDtypeStruct((n_rows, npass * MOE_TF), BF16),
        compiler_params=_params(("arbitrary", "arbitrary")),
        name="expert_gateup",
    )(*sched, xs, w_gu, b_gu.reshape(e, 1, f2))


def _down_kernel(be_ref, bv_ref, nused_ref, sw_ref, nx_ref, nsw_ref, a0_ref, a1_ref, w_hbm, bdn_ref,
                 o_ref, wb, wbuf, wsem):
    m = pl.program_id(1)
    _stream_expert_chunk(be_ref, bv_ref, sw_ref, nx_ref, nsw_ref, w_hbm, wbuf, wsem, wb)
    f0 = a0_ref.shape[1]

    def compute_rows(rows):
        out = (jnp.dot(a0_ref[rows], wb[:f0], preferred_element_type=F32)
               + jnp.dot(a1_ref[rows], wb[f0:], preferred_element_type=F32)) + bdn_ref[...]
        o_ref[rows] = _to_token_rows(out.astype(o_ref.dtype))

    _for_live_block(bv_ref, m, o_ref, compute_rows)


def down_call(sched, act0, act1, w_dn, b_dn):
    n_rows, f0 = act0.shape
    f1 = act1.shape[1]
    e, f, d = w_dn.shape
    assert f0 + f1 == f
    nblk = n_rows // MOE_TM
    row_blk = lambda j, m, be, bv, nu, *_: (jnp.minimum(m, nu[0] - 1), 0)

    return pl.pallas_call(
        _down_kernel,
        grid_spec=pltpu.PrefetchScalarGridSpec(
            num_scalar_prefetch=6,
            grid=(d // MOE_TN, nblk),
            in_specs=[pl.BlockSpec((MOE_TM, f0), row_blk),
                      pl.BlockSpec((MOE_TM, f1), row_blk),
                      pl.BlockSpec(memory_space=pl.ANY),
                      pl.BlockSpec((None, 1, MOE_TN), lambda j, m, be, *_: (be[m], 0, j))],
            out_specs=pl.BlockSpec((MOE_TM, MOE_TN // LANES, LANES), lambda j, m, *_: (m, j, 0)),
            scratch_shapes=[pltpu.VMEM((f, MOE_TN), BF16), pltpu.VMEM((2, f, MOE_TN), F32),
                            pltpu.SemaphoreType.DMA((2,))]),
        out_shape=jax.ShapeDtypeStruct((n_rows, d // LANES, LANES), BF16),
        compiler_params=_params(("arbitrary", "arbitrary")),
        name="expert_down",
    )(*sched, act0, act1, w_dn, b_dn.reshape(e, 1, d))


def _combine_kernel(pos_ref, posn_ref, ys_hbm, h_ref, gt_ref, g_ref, h_out, xn_out, buf, sem):
    i = pl.program_id(0)
    n = pl.num_programs(0)
    slot = i % 2

    def row_copy(row, sl, k, t):
        return pltpu.make_async_copy(ys_hbm.at[row], buf.at[sl, k, t], sem.at[sl])

    def issue(p_ref, sl):
        def body(g, c):
            rows = [p_ref[0, 0, g * DMA_UNROLL + u] for u in range(DMA_UNROLL)]
            for u in range(DMA_UNROLL):
                t = g * (DMA_UNROLL // TOP_K) + u // TOP_K
                row_copy(rows[u], sl, u % TOP_K, t).start(priority=ROW_DMA_PRIORITY)
            return c

        lax.fori_loop(0, CMB_TB * TOP_K // DMA_UNROLL, body, 0)

    @pl.when(i == 0)
    def _():
        issue(pos_ref, 0)

    @pl.when(i + 1 < n)
    def _():
        issue(posn_ref, 1 - slot)

    for k in range(TOP_K):
        pltpu.make_async_copy(ys_hbm.at[pl.ds(0, CMB_TB)], buf.at[slot, k], sem.at[slot]).wait()

    h = h_ref[...]
    for k in range(TOP_K):
        rows = buf[slot, k].reshape(h.shape).astype(F32)
        h = h + gt_ref[:, k:k + 1] * rows
    h_out[...] = h
    ms = jnp.mean(h * h, axis=-1, keepdims=True)
    xn_out[...] = (h * lax.rsqrt(ms + RMS_EPS) * g_ref[...]).astype(xn_out.dtype)


def combine_call(pos, ys, h, gates, g):
    s, d = h.shape
    nb = s // CMB_TB
    pos3 = pos.reshape(nb, 1, CMB_TB * TOP_K)
    pos_spec = lambda f: pl.BlockSpec((1, 1, CMB_TB * TOP_K), f, memory_space=pltpu.SMEM)
    return pl.pallas_call(
        _combine_kernel,
        grid=(nb,),
        in_specs=[pos_spec(lambda i: (i, 0, 0)),
                  pos_spec(lambda i: (jnp.minimum(i + 1, nb - 1), 0, 0)),
                  pl.BlockSpec(memory_space=pl.ANY),
                  pl.BlockSpec((CMB_TB, d), lambda i: (i, 0)),
                  pl.BlockSpec((CMB_TB, LANES), lambda i: (i, 0)),
                  pl.BlockSpec((1, d), lambda i: (0, 0))],
        out_specs=[pl.BlockSpec((CMB_TB, d), lambda i: (i, 0)),
                   pl.BlockSpec((CMB_TB, d), lambda i: (i, 0))],
        out_shape=[jax.ShapeDtypeStruct((s, d), F32), jax.ShapeDtypeStruct((s, d), BF16)],
        scratch_shapes=[pltpu.VMEM((2, TOP_K, CMB_TB, d // LANES, LANES), BF16),
                        pltpu.SemaphoreType.DMA((2,))],
        compiler_params=_params(("arbitrary",)),
        name="combine",
    )(pos3, pos3, ys, h, gates, g.reshape(1, d))


def moe_routing(top_e, n_tokens):
    e_flat = top_e.reshape(-1)
    onehot = (e_flat[:, None] == jnp.arange(N_EXPERTS, dtype=jnp.int32)[None, :]).astype(jnp.int32)
    csum = jnp.cumsum(onehot, axis=0)
    counts = csum[-1]
    rank = jnp.sum(csum * onehot, axis=1) - 1
    padded = (counts + MOE_TM - 1) // MOE_TM * MOE_TM
    pends = jnp.cumsum(padded).astype(jnp.int32)
    pstarts = pends - padded
    pos = pstarts[e_flat] + rank
    n_blocks = (n_tokens * TOP_K) // MOE_TM + N_EXPERTS
    tok_flat = jnp.arange(n_tokens * TOP_K, dtype=jnp.int32) // TOP_K
    row_tok = jnp.zeros((n_blocks * MOE_TM,), jnp.int32).at[pos].set(
        tok_flat, unique_indices=True, mode="promise_in_bounds")
    n_used = pends[-1] // MOE_TM
    blk_start = jnp.arange(n_blocks, dtype=jnp.int32) * MOE_TM
    block_e = jnp.sum((pends[None, :] <= blk_start[:, None]).astype(jnp.int32), axis=1)
    block_e = jnp.minimum(block_e, N_EXPERTS - 1)
    last_e = block_e[jnp.maximum(n_used - 1, 0)]
    used = jnp.arange(n_blocks) < n_used
    real_end = pstarts + counts
    block_valid = jnp.clip(real_end[block_e] - blk_start, 0, MOE_TM)
    block_valid = jnp.where(used, block_valid, 0).astype(jnp.int32)
    block_e = jnp.where(used, block_e, last_e).astype(jnp.int32)
    has_rows = counts > 0
    eidx = jnp.arange(N_EXPERTS, dtype=jnp.int32)
    later = jnp.logical_and(eidx[None, :] > eidx[:, None], has_rows[None, :])
    next_of = jnp.min(jnp.where(later, eidx[None, :], N_EXPERTS), axis=1)
    next_of = jnp.where(next_of < N_EXPERTS, next_of, -1)
    order_of = jnp.cumsum(has_rows.astype(jnp.int32)) - 1
    routing = dict(block_e=block_e, block_valid=block_valid,
                   n_used=n_used.reshape(1).astype(jnp.int32),
                   switch_idx=order_of[block_e].astype(jnp.int32),
                   next_e=next_of[block_e].astype(jnp.int32),
                   n_switch=jnp.sum(has_rows.astype(jnp.int32)).reshape(1))
    return pos.astype(jnp.int32), row_tok, routing


def kernel(x, p, g_mix, w_in, attn_sink, na_rpb, w_branch_a, w_branch_b, w_out, g_ffn, w_router,
           b_router, w_gate_up, b_gate_up, w_down, b_down, g_ple, w_ple_gate, w_ple, g_final):
    b, s, d = x.shape
    assert b == 1 and w_in.shape[0] == 1
    h = x.reshape(s, d)
    rope_c, rope_s1, rope_s2 = rope_tables(s)

    xn = rmsnorm_call(h, g_mix[0], BF16)
    proj = inproj_call(xn, w_in[0], rope_c, rope_s1, rope_s2, d)
    ya = window_attn_call(proj, attn_sink[0])
    yb = na_attn_call(proj, na_bias_tables(na_rpb[0]))
    merged = merge_call(ya, yb, w_branch_a[0], w_branch_b[0], proj, d)
    h = resmm_call(merged, w_out[0], h)

    xn2, top_e, gates = rms_router_call(h, g_ffn[0], w_router[0], b_router[0])
    pos, row_tok, rt = moe_routing(top_e[:, :TOP_K], s)
    sched = (rt["block_e"], rt["block_valid"], rt["n_used"], rt["switch_idx"], rt["next_e"], rt["n_switch"])
    xs, act0 = dispatch_gateup_call(sched, row_tok, xn2, w_gate_up[0], b_gate_up[0])
    act1 = gateup_call(sched, xs, w_gate_up[0], b_gate_up[0])
    ys = down_call(sched, act0, act1, w_down[0], b_down[0])
    h, xn3 = combine_call(pos, ys, h, gates, g_ple[0])

    h = ple_call(xn3, w_ple_gate[0], p[0].reshape(s, -1), w_ple[0], h)
    out = rmsnorm_call(h, g_final, F32)
    return out.reshape(b, s, d)
```

```python
import functools

import numpy as np
import jax
import jax.numpy as jnp
from jax import lax
from jax.experimental import pallas as pl
from jax.experimental.pallas import tpu as pltpu

F32 = jnp.float32
BF16 = jnp.bfloat16

HEAD_DIM = 64
A_HEADS = 32
A_KV_HEADS = 8
A_GROUP = A_HEADS // A_KV_HEADS
B_HEADS = 32
A_Q_W = A_HEADS * HEAD_DIM
A_KV_W = A_KV_HEADS * HEAD_DIM
B_W = B_HEADS * HEAD_DIM
ROT_DIM = HEAD_DIM // 4
ROPE_THETA = 500000.0
WINDOW = 128
GRID_W = 64
NA_ROWS = 8
NA_COLS = 16
N_EXPERTS = 32
TOP_K = 4
SWIGLU_LIMIT = 7.0
SWIGLU_ALPHA = 1.702
RMS_EPS = 1e-6
NEG_INF = -1e30

LANES = 128
VMEM_LIMIT = 56 * 1024 * 1024

RMS_ROWS = 256
MM_TM = 1024
MM_TN = 512
ATT_BLK = 128
ATT_QB = 2
NA_RB = 8
NA_HPS = 4
MOE_TM = 512
MOE_TF = 256
MOE_TN = 2048
CMB_TB = 128
DMA_UNROLL = 8
ROW_DMA_PRIORITY = 1


def _sigmoid(x):
    return 0.5 * jnp.tanh(0.5 * x) + 0.5


def _params(sem):
    return pltpu.CompilerParams(dimension_semantics=sem, vmem_limit_bytes=VMEM_LIMIT)


def _rms_body(x_ref, g_ref):
    x = x_ref[...].astype(F32)
    ms = jnp.mean(x * x, axis=-1, keepdims=True)
    return x * lax.rsqrt(ms + RMS_EPS) * g_ref[...]


def _rms_kernel(x_ref, g_ref, o_ref):
    o_ref[...] = _rms_body(x_ref, g_ref).astype(o_ref.dtype)


def rmsnorm_call(x, g, out_dtype):
    s, d = x.shape
    return pl.pallas_call(
        _rms_kernel,
        grid=(s // RMS_ROWS,),
        in_specs=[pl.BlockSpec((RMS_ROWS, d), lambda i: (i, 0)),
                  pl.BlockSpec((1, d), lambda i: (0, 0))],
        out_specs=pl.BlockSpec((RMS_ROWS, d), lambda i: (i, 0)),
        out_shape=jax.ShapeDtypeStruct((s, d), out_dtype),
        compiler_params=_params(("parallel",)),
        name="rmsnorm",
    )(x, g.reshape(1, d))


def _to_token_rows(x2d):
    return x2d.reshape(x2d.shape[0], x2d.shape[1] // LANES, LANES)


def _split_bf16(x):
    hi = x.astype(BF16)
    return hi, (x - hi.astype(F32)).astype(BF16)


def _rms_router_kernel(x_ref, g_ref, wh_ref, wl_ref, br_ref, xn_ref, e_ref, gt_ref):
    xn = _rms_body(x_ref, g_ref)
    xh, xl = _split_bf16(xn)
    xn_ref[...] = _to_token_rows(xh)
    logits = (jnp.dot(xh, wh_ref[...], preferred_element_type=F32)
              + jnp.dot(xh, wl_ref[...], preferred_element_type=F32)
              + jnp.dot(xl, wh_ref[...], preferred_element_type=F32)) + br_ref[...]
    lane = lax.broadcasted_iota(jnp.int32, logits.shape, 1)
    logits = jnp.where(lane < N_EXPERTS, logits, -jnp.inf)
    e_out = jnp.zeros(logits.shape, jnp.int32)
    v_out = jnp.zeros(logits.shape, F32)
    v0 = None
    for k in range(TOP_K):
        m = jnp.max(logits, axis=-1, keepdims=True)
        idx = jnp.min(jnp.where(logits == m, lane, LANES), axis=-1, keepdims=True)
        if k == 0:
            v0 = m
        e_out = jnp.where(lane == k, idx, e_out)
        v_out = jnp.where(lane == k, jnp.exp(m - v0), v_out)
        logits = jnp.where(lane == idx, -jnp.inf, logits)
    e_ref[...] = e_out
    gt_ref[...] = v_out / jnp.sum(v_out, axis=-1, keepdims=True)


def rms_router_call(h, g, w_router, b_router):
    s, d = h.shape
    wr = jnp.zeros((d, LANES), F32).at[:, :N_EXPERTS].set(w_router)
    br = jnp.zeros((1, LANES), F32).at[0, :N_EXPERTS].set(b_router)
    wr_hi, wr_lo = _split_bf16(wr)
    return pl.pallas_call(
        _rms_router_kernel,
        grid=(s // RMS_ROWS,),
        in_specs=[pl.BlockSpec((RMS_ROWS, d), lambda i: (i, 0)),
                  pl.BlockSpec((1, d), lambda i: (0, 0)),
                  pl.BlockSpec((d, LANES), lambda i: (0, 0)),
                  pl.BlockSpec((d, LANES), lambda i: (0, 0)),
                  pl.BlockSpec((1, LANES), lambda i: (0, 0))],
        out_specs=[pl.BlockSpec((RMS_ROWS, d // LANES, LANES), lambda i: (i, 0, 0)),
                   pl.BlockSpec((RMS_ROWS, LANES), lambda i: (i, 0)),
                   pl.BlockSpec((RMS_ROWS, LANES), lambda i: (i, 0))],
        out_shape=[jax.ShapeDtypeStruct((s, d // LANES, LANES), BF16),
                   jax.ShapeDtypeStruct((s, LANES), jnp.int32),
                   jax.ShapeDtypeStruct((s, LANES), F32)],
        compiler_params=_params(("parallel",)),
        name="rms_router",
    )(h, g.reshape(1, d), wr_hi, wr_lo, br)


def _rope(acc, c_ref, s1_ref, s2_ref):
    c, s1, s2 = c_ref[...], s1_ref[...], s2_ref[...]
    half = ROT_DIM // 2
    outs = []
    for t in range(acc.shape[1] // LANES):
        x = acc[:, t * LANES:(t + 1) * LANES]
        outs.append(x * c + pltpu.roll(x, LANES - half, 1) * s1 + pltpu.roll(x, half, 1) * s2)
    return jnp.concatenate(outs, axis=1)


def _stream_weight_chunk(w_hbm, wbuf, sem, wb_ref):
    j = pl.program_id(0)
    nj = pl.num_programs(0)
    tn = wbuf.shape[2]

    def chunk_copy(jj, slot):
        return pltpu.make_async_copy(w_hbm.at[:, pl.ds(pl.multiple_of(jj * tn, tn), tn)],
                                     wbuf.at[slot], sem.at[slot])

    @pl.when(pl.program_id(1) == 0)
    def _():
        slot = j % 2

        @pl.when(j == 0)
        def _():
            chunk_copy(0, 0).start()

        chunk_copy(j, slot).wait()

        @pl.when(j + 1 < nj)
        def _():
            chunk_copy(j + 1, 1 - slot).start()

        wb_ref[...] = wbuf[slot].astype(BF16)


def _inproj_kernel(a_ref, w_hbm, c_ref, s1_ref, s2_ref, o_ref, wb_ref, wbuf, wsem, *, n_rope, n_plain):
    j = pl.program_id(0)
    _stream_weight_chunk(w_hbm, wbuf, wsem, wb_ref)
    acc = jnp.dot(a_ref[...], wb_ref[...], preferred_element_type=F32)

    @pl.when(j < n_rope)
    def _():
        o_ref[...] = _rope(acc, c_ref, s1_ref, s2_ref).astype(o_ref.dtype)

    @pl.when(jnp.logical_and(j >= n_rope, j < n_plain))
    def _():
        o_ref[...] = acc.astype(o_ref.dtype)

    @pl.when(j >= n_plain)
    def _():
        o_ref[...] = _sigmoid(acc).astype(o_ref.dtype)


def inproj_call(xn, w_in, rope_c, rope_s1, rope_s2, d_model):
    s, d = xn.shape
    n = w_in.shape[1]
    n_rope = (A_Q_W + A_KV_W) // MM_TN
    n_plain = (n - 2 * d_model) // MM_TN
    tab = pl.BlockSpec((MM_TM, LANES), lambda j, i: (i, 0))
    return pl.pallas_call(
        functools.partial(_inproj_kernel, n_rope=n_rope, n_plain=n_plain),
        grid=(n // MM_TN, s // MM_TM),
        in_specs=[pl.BlockSpec((MM_TM, d), lambda j, i: (i, 0)),
                  pl.BlockSpec(memory_space=pl.ANY),
                  tab, tab, tab],
        out_specs=pl.BlockSpec((MM_TM, MM_TN), lambda j, i: (i, j)),
        out_shape=jax.ShapeDtypeStruct((s, n), BF16),
        scratch_shapes=[pltpu.VMEM((d, MM_TN), BF16), pltpu.VMEM((2, d, MM_TN), F32),
                        pltpu.SemaphoreType.DMA((2,))],
        compiler_params=_params(("arbitrary", "arbitrary")),
        name="inproj",
    )(xn, w_in, rope_c, rope_s1, rope_s2)


def rope_tables(s):
    half = ROT_DIM // 2
    inv = jnp.float32(ROPE_THETA) ** (-jnp.arange(half, dtype=F32) * (2.0 / ROT_DIM))
    ang = jnp.arange(s, dtype=F32)[:, None] * inv[None, :]
    cos, sin = jnp.cos(ang), jnp.sin(ang)
    ones = jnp.ones((s, HEAD_DIM - ROT_DIM), F32)
    zeros = jnp.zeros((s, HEAD_DIM - ROT_DIM), F32)
    zh = jnp.zeros((s, half), F32)
    c = jnp.concatenate([cos, cos, ones], axis=1)
    s1 = jnp.concatenate([-sin, zh, zeros], axis=1)
    s2 = jnp.concatenate([zh, sin, zeros], axis=1)
    rep = LANES // HEAD_DIM
    return jnp.tile(c, (1, rep)), jnp.tile(s1, (1, rep)), jnp.tile(s2, (1, rep))


def _win_kernel(sink_ref, q_ref, kp_ref, kc_ref, kn_ref, vp_ref, vc_ref, vn_ref, o_ref, *, seq):
    n0 = pl.program_id(0) * ATT_QB
    pr = pl.program_id(1)
    kcat = jnp.concatenate([kp_ref[...], kc_ref[...], kn_ref[...]], axis=0)
    vcat = jnp.concatenate([vp_ref[...], vc_ref[...], vn_ref[...]], axis=0)
    rows = A_GROUP * ATT_BLK
    nk = 3 * ATT_BLK
    grp = lax.broadcasted_iota(jnp.int32, (A_GROUP, 1, 1), 0)
    kv_per_blk = LANES // HEAD_DIM
    for b in range(ATT_QB):
        n = n0 + b
        q = q_ref[b * ATT_BLK:(b + 1) * ATT_BLK, :] * jnp.asarray(HEAD_DIM ** -0.5, BF16)
        qpos = n * ATT_BLK + lax.broadcasted_iota(jnp.int32, (ATT_BLK, nk), 0)
        kpos = (n - 1) * ATT_BLK + lax.broadcasted_iota(jnp.int32, (ATT_BLK, nk), 1)
        mask = ((jnp.abs(qpos - kpos) <= WINDOW) & (kpos >= 0) & (kpos < seq))[None]
        for kvh in range(kv_per_blk):
            k_h = kcat[b * ATT_BLK:b * ATT_BLK + nk, kvh * HEAD_DIM:(kvh + 1) * HEAD_DIM]
            v_h = vcat[b * ATT_BLK:b * ATT_BLK + nk, kvh * HEAD_DIM:(kvh + 1) * HEAD_DIM]
            h0 = kvh * A_GROUP
            qs = jnp.concatenate(
                [q[:, (h0 + g) * HEAD_DIM:(h0 + g + 1) * HEAD_DIM] for g in range(A_GROUP)], axis=0)
            sc = lax.dot_general(qs, k_h, (((1,), (1,)), ((), ())), preferred_element_type=F32)
            sc = jnp.where(mask, sc.reshape(A_GROUP, ATT_BLK, nk), NEG_INF)
            snk = jnp.zeros((A_GROUP, 1, 1), F32)
            for g in range(A_GROUP):
                sv = sink_ref[(pr * kv_per_blk + kvh) * A_GROUP + g]
                snk = jnp.where(grp == g, sv, snk)
            m = jnp.maximum(jnp.max(sc, axis=-1, keepdims=True), snk)
            p = jnp.exp(sc - m)
            denom = jnp.sum(p, axis=-1, keepdims=True) + jnp.exp(snk - m)
            attn = (p * (1.0 / denom)).astype(BF16).reshape(rows, nk)
            out = jnp.dot(attn, v_h, preferred_element_type=F32)
            for g in range(A_GROUP):
                o_ref[b * ATT_BLK:(b + 1) * ATT_BLK, (h0 + g) * HEAD_DIM:(h0 + g + 1) * HEAD_DIM] = (
                    out[g * ATT_BLK:(g + 1) * ATT_BLK].astype(o_ref.dtype))


def window_attn_call(proj, sink):
    s = proj.shape[0]
    nb = s // ATT_BLK
    qw = A_GROUP * LANES
    k0 = A_Q_W // LANES
    v0 = (A_Q_W + A_KV_W) // LANES
    edge = (ATT_BLK, LANES)
    own = (ATT_QB * ATT_BLK, LANES)
    prev = lambda n: jnp.maximum(n * ATT_QB - 1, 0)
    nxt = lambda n: jnp.minimum((n + 1) * ATT_QB, nb - 1)
    return pl.pallas_call(
        functools.partial(_win_kernel, seq=s),
        grid=(nb // ATT_QB, A_Q_W // qw),
        in_specs=[pl.BlockSpec(memory_space=pltpu.SMEM),
                  pl.BlockSpec((ATT_QB * ATT_BLK, qw), lambda n, p: (n, p)),
                  pl.BlockSpec(edge, lambda n, p: (prev(n), k0 + p)),
                  pl.BlockSpec(own, lambda n, p: (n, k0 + p)),
                  pl.BlockSpec(edge, lambda n, p: (nxt(n), k0 + p)),
                  pl.BlockSpec(edge, lambda n, p: (prev(n), v0 + p)),
                  pl.BlockSpec(own, lambda n, p: (n, v0 + p)),
                  pl.BlockSpec(edge, lambda n, p: (nxt(n), v0 + p))],
        out_specs=pl.BlockSpec((ATT_QB * ATT_BLK, qw), lambda n, p: (n, p)),
        out_shape=jax.ShapeDtypeStruct((s, A_Q_W), BF16),
        compiler_params=_params(("parallel", "parallel")),
        name="window_attn",
    )(sink, proj, proj, proj, proj, proj, proj, proj)


def na_bias_tables(rpb):
    c = np.arange(GRID_W)
    kc = np.arange(GRID_W)
    dc = np.clip(kc[None, :] - c[:, None] + NA_COLS - 1, 0, 2 * NA_COLS - 2)
    qcs = np.clip(c - NA_COLS // 2, 0, GRID_W - NA_COLS)
    cmask = (kc[None, :] >= qcs[:, None]) & (kc[None, :] < qcs[:, None] + NA_COLS)
    t = jnp.where(jnp.asarray(cmask)[None, None], rpb[:, :, dc].astype(F32), NEG_INF)
    t = jnp.transpose(t, (0, 2, 1, 3)).reshape(rpb.shape[0], GRID_W, (2 * NA_ROWS - 1) * GRID_W)
    return jnp.pad(t, ((0, 0), (0, 0), (GRID_W, 0)))


def _na_kernel(q_ref, kp_ref, kc_ref, kn_ref, vp_ref, vc_ref, vn_ref, b_ref, o_ref,
               kbuf, vbuf, s_scr, p_scr, *, grid_rows):
    rb = pl.program_id(1)
    blk = NA_RB * GRID_W
    kbuf[0:blk] = kp_ref[...]
    kbuf[blk:2 * blk] = kc_ref[...]
    kbuf[2 * blk:3 * blk] = kn_ref[...]
    vbuf[0:blk] = vp_ref[...]
    vbuf[blk:2 * blk] = vc_ref[...]
    vbuf[2 * blk:3 * blk] = vn_ref[...]
    nkeys = NA_ROWS * GRID_W
    width = NA_HPS * HEAD_DIM
    grp = NA_HPS * GRID_W
    row_head = lax.broadcasted_iota(jnp.int32, (grp, width), 0) // GRID_W
    lane_head = lax.broadcasted_iota(jnp.int32, (grp, width), 1) // HEAD_DIM
    own = row_head == lane_head
    out_head = lax.broadcasted_iota(jnp.int32, (GRID_W, width), 1) // HEAD_DIM
    nrb = grid_rows // NA_RB
    half = NA_ROWS // 2

    def rows(first_rb):
        starts = []
        for i in range(NA_RB):
            r = first_rb * NA_RB + i
            rs = min(max(r - half, 0), grid_rows - NA_ROWS)
            typ = rs - (r - half) + (half - 1)
            start = (rs - (first_rb - 1) * NA_RB) * GRID_W
            starts.append(start)
            qi = q_ref[i * GRID_W:(i + 1) * GRID_W, :] * jnp.asarray(HEAD_DIM ** -0.5, BF16)
            qs = jnp.where(own, jnp.concatenate([qi] * NA_HPS, axis=0), jnp.zeros((grp, width), BF16))
            sc = lax.dot_general(qs, kbuf[start:start + nkeys, :], (((1,), (1,)), ((), ())),
                                 preferred_element_type=F32)
            boff = (typ + 1) * GRID_W
            s_scr[i * grp:(i + 1) * grp, :] = sc + b_ref[:, :, boff:boff + nkeys].reshape(grp, nkeys)
        for i in range(NA_RB):
            sc = s_scr[i * grp:(i + 1) * grp, :]
            m = jnp.max(sc, axis=-1, keepdims=True)
            p = jnp.exp(sc - m)
            p_scr[i * grp:(i + 1) * grp, :] = (
                p * (1.0 / jnp.sum(p, axis=-1, keepdims=True))).astype(BF16)
        for i in range(NA_RB):
            start = starts[i]
            out = jnp.dot(p_scr[i * grp:(i + 1) * grp, :], vbuf[start:start + nkeys, :],
                          preferred_element_type=F32)
            res = out[:GRID_W]
            for h in range(1, NA_HPS):
                res = jnp.where(out_head == h, out[h * GRID_W:(h + 1) * GRID_W], res)
            o_ref[i * GRID_W:(i + 1) * GRID_W, :] = res.astype(o_ref.dtype)

    @pl.when(rb == 0)
    def _():
        rows(0)

    @pl.when(rb == nrb - 1)
    def _():
        rows(nrb - 1)

    @pl.when(jnp.logical_and(rb > 0, rb < nrb - 1))
    def _():
        rows(1)


def na_attn_call(proj, bias_tab):
    s = proj.shape[0]
    grid_rows = s // GRID_W
    nrb = grid_rows // NA_RB
    blk_rows = NA_RB * GRID_W
    width = NA_HPS * HEAD_DIM
    q0 = (A_Q_W + 2 * A_KV_W) // width
    k0 = q0 + B_W // width
    v0 = k0 + B_W // width
    blk = (blk_rows, width)
    hpb = NA_HPS
    prev = lambda r: jnp.maximum(r - 1, 0)
    cur = lambda r: r
    nxt = lambda r: jnp.minimum(r + 1, nrb - 1)
    return pl.pallas_call(
        functools.partial(_na_kernel, grid_rows=grid_rows),
        grid=(B_HEADS // hpb, nrb),
        in_specs=[pl.BlockSpec(blk, lambda h, r: (r, q0 + h)),
                  pl.BlockSpec(blk, lambda h, r: (prev(r), k0 + h)),
                  pl.BlockSpec(blk, lambda h, r: (cur(r), k0 + h)),
                  pl.BlockSpec(blk, lambda h, r: (nxt(r), k0 + h)),
                  pl.BlockSpec(blk, lambda h, r: (prev(r), v0 + h)),
                  pl.BlockSpec(blk, lambda h, r: (cur(r), v0 + h)),
                  pl.BlockSpec(blk, lambda h, r: (nxt(r), v0 + h)),
                  pl.BlockSpec((hpb, GRID_W, 2 * NA_ROWS * GRID_W), lambda h, r: (h, 0, 0))],
        out_specs=pl.BlockSpec(blk, lambda h, r: (r, h)),
        out_shape=jax.ShapeDtypeStruct((s, B_W), BF16),
        scratch_shapes=[pltpu.VMEM((3 * blk_rows, width), BF16),
                        pltpu.VMEM((3 * blk_rows, width), BF16),
                        pltpu.VMEM((NA_RB * hpb * GRID_W, NA_ROWS * GRID_W), F32),
                        pltpu.VMEM((NA_RB * hpb * GRID_W, NA_ROWS * GRID_W), BF16)],
        compiler_params=_params(("parallel", "arbitrary")),
        name="na_attn",
    )(proj, proj, proj, proj, proj, proj, proj, bias_tab)


def _merge_kernel(ya_ref, yb_ref, wa_ref, wb_ref, ga_ref, gb_ref, o_ref, wab, wbb):
    @pl.when(pl.program_id(1) == 0)
    def _():
        wab[...] = wa_ref[...].astype(BF16)
        wbb[...] = wb_ref[...].astype(BF16)

    a = jnp.dot(ya_ref[...], wab[...], preferred_element_type=F32)
    b = jnp.dot(yb_ref[...], wbb[...], preferred_element_type=F32)
    o_ref[...] = (ga_ref[...].astype(F32) * a + gb_ref[...].astype(F32) * b).astype(o_ref.dtype)


def merge_call(ya, yb, w_a, w_b, proj, d_model):
    s = ya.shape[0]
    ga0 = (proj.shape[1] - 2 * d_model) // MM_TN
    gb0 = ga0 + d_model // MM_TN
    return pl.pallas_call(
        _merge_kernel,
        grid=(d_model // MM_TN, s // MM_TM),
        in_specs=[pl.BlockSpec((MM_TM, A_Q_W), lambda j, i: (i, 0)),
                  pl.BlockSpec((MM_TM, B_W), lambda j, i: (i, 0)),
                  pl.BlockSpec((A_Q_W, MM_TN), lambda j, i: (0, j)),
                  pl.BlockSpec((B_W, MM_TN), lambda j, i: (0, j)),
                  pl.BlockSpec((MM_TM, MM_TN), lambda j, i: (i, ga0 + j)),
                  pl.BlockSpec((MM_TM, MM_TN), lambda j, i: (i, gb0 + j))],
        out_specs=pl.BlockSpec((MM_TM, MM_TN), lambda j, i: (i, j)),
        out_shape=jax.ShapeDtypeStruct((s, d_model), BF16),
        scratch_shapes=[pltpu.VMEM((A_Q_W, MM_TN), BF16), pltpu.VMEM((B_W, MM_TN), BF16)],
        compiler_params=_params(("parallel", "arbitrary")),
        name="merge",
    )(ya, yb, w_a, w_b, proj, proj)


def _resmm_kernel(a_ref, w_hbm, r_ref, o_ref, wb_ref, wbuf, wsem):
    _stream_weight_chunk(w_hbm, wbuf, wsem, wb_ref)
    o_ref[...] = r_ref[...] + jnp.dot(a_ref[...], wb_ref[...], preferred_element_type=F32)


def resmm_call(a, w, res):
    s, k = a.shape
    n = w.shape[1]
    return pl.pallas_call(
        _resmm_kernel,
        grid=(n // MM_TN, s // MM_TM),
        in_specs=[pl.BlockSpec((MM_TM, k), lambda j, i: (i, 0)),
                  pl.BlockSpec(memory_space=pl.ANY),
                  pl.BlockSpec((MM_TM, MM_TN), lambda j, i: (i, j))],
        out_specs=pl.BlockSpec((MM_TM, MM_TN), lambda j, i: (i, j)),
        out_shape=jax.ShapeDtypeStruct((s, n), F32),
        scratch_shapes=[pltpu.VMEM((k, MM_TN), BF16), pltpu.VMEM((2, k, MM_TN), F32),
                        pltpu.SemaphoreType.DMA((2,))],
        compiler_params=_params(("arbitrary", "arbitrary")),
        name="resmm",
    )(a, w, res)


def _ple_kernel(a_ref, wg_hbm, p_ref, wp_ref, h_ref, o_ref, wgb, wpb, wbuf, wsem):
    _stream_weight_chunk(wg_hbm, wbuf, wsem, wgb)

    @pl.when(pl.program_id(1) == 0)
    def _():
        wpb[...] = wp_ref[...].astype(BF16)

    gate = _sigmoid(jnp.dot(a_ref[...], wgb[...], preferred_element_type=F32))
    emb = jnp.dot(p_ref[...].astype(BF16), wpb[...], preferred_element_type=F32)
    o_ref[...] = h_ref[...] + gate * emb


def ple_call(xn, w_gate, p, w_ple, h):
    s, d = xn.shape
    pd = p.shape[1]
    return pl.pallas_call(
        _ple_kernel,
        grid=(d // MM_TN, s // MM_TM),
        in_specs=[pl.BlockSpec((MM_TM, d), lambda j, i: (i, 0)),
                  pl.BlockSpec(memory_space=pl.ANY),
                  pl.BlockSpec((MM_TM, pd), lambda j, i: (i, 0)),
                  pl.BlockSpec((pd, MM_TN), lambda j, i: (0, j)),
                  pl.BlockSpec((MM_TM, MM_TN), lambda j, i: (i, j))],
        out_specs=pl.BlockSpec((MM_TM, MM_TN), lambda j, i: (i, j)),
        out_shape=jax.ShapeDtypeStruct((s, d), F32),
        scratch_shapes=[pltpu.VMEM((d, MM_TN), BF16), pltpu.VMEM((pd, MM_TN), BF16),
                        pltpu.VMEM((2, d, MM_TN), F32), pltpu.SemaphoreType.DMA((2,))],
        compiler_params=_params(("arbitrary", "arbitrary")),
        name="ple",
    )(xn, w_gate, p, w_ple, h)


def _swiglu_chunk(x, wb, bgu_ref):
    hgu = jnp.dot(x, wb[...], preferred_element_type=F32) + bgu_ref[...]
    gate = jnp.minimum(hgu, SWIGLU_LIMIT)
    up = jnp.clip(hgu, -SWIGLU_LIMIT, SWIGLU_LIMIT)
    up = pltpu.roll(up, 2 * MOE_TF - 1, 1)
    act = gate * _sigmoid(gate * SWIGLU_ALPHA) * (up + 1.0)
    rr = lax.broadcasted_iota(jnp.int32, (2 * MOE_TF, MOE_TF), 0)
    cc = lax.broadcasted_iota(jnp.int32, (2 * MOE_TF, MOE_TF), 1)
    sel = (rr == 2 * cc).astype(BF16)
    return jnp.dot(act.astype(BF16), sel, preferred_element_type=F32)


def _dispatch_gateup_kernel(be_ref, bv_ref, nused_ref, sw_ref, nx_ref, nsw_ref, tok_ref, tokn_ref,
                            x_hbm, w_hbm, bgu_ref, xs_ref, act_ref, gbuf, gsem, wb, wbuf, wsem):
    m = pl.program_id(1)
    nused = nused_ref[0]
    slot = m % 2
    _stream_expert_chunk(be_ref, bv_ref, sw_ref, nx_ref, nsw_ref, w_hbm, wbuf, wsem, wb)

    def row_copy(tok, sl, r):
        return pltpu.make_async_copy(x_hbm.at[tok], gbuf.at[sl, r], gsem.at[sl])

    def issue(t_ref, sl):
        def body(g, c):
            rows = [g * DMA_UNROLL + u for u in range(DMA_UNROLL)]
            toks = [t_ref[0, 0, r] for r in rows]
            for u in range(DMA_UNROLL):
                row_copy(toks[u], sl, rows[u]).start(priority=ROW_DMA_PRIORITY)
            return c

        lax.fori_loop(0, MOE_TM // DMA_UNROLL, body, 0)

    @pl.when(m == 0)
    def _():
        issue(tok_ref, 0)

    @pl.when(m + 1 < nused)
    def _():
        issue(tokn_ref, 1 - slot)

    @pl.when(m < nused)
    def _():
        pltpu.make_async_copy(x_hbm.at[pl.ds(0, MOE_TM)], gbuf.at[slot], gsem.at[slot]).wait()
        x = gbuf[slot].reshape(xs_ref.shape)
        xs_ref[...] = x
        act_ref[...] = _swiglu_chunk(x, wb, bgu_ref).astype(act_ref.dtype)

    @pl.when(m >= nused)
    def _():
        xs_ref[...] = jnp.zeros(xs_ref.shape, xs_ref.dtype)
        act_ref[...] = jnp.zeros(act_ref.shape, act_ref.dtype)


def dispatch_gateup_call(sched, row_tok, xn3d, w_gu, b_gu):
    s, dg, _ = xn3d.shape
    e, d, f2 = w_gu.shape
    n_rows = row_tok.shape[0]
    nblk = n_rows // MOE_TM
    tok3 = row_tok.reshape(nblk, 1, MOE_TM)
    tok_spec = lambda f: pl.BlockSpec((1, 1, MOE_TM), f, memory_space=pltpu.SMEM)
    return pl.pallas_call(
        _dispatch_gateup_kernel,
        grid_spec=pltpu.PrefetchScalarGridSpec(
            num_scalar_prefetch=6,
            grid=(1, nblk),
            in_specs=[tok_spec(lambda j, m, *_: (m, 0, 0)),
                      tok_spec(lambda j, m, *_: (jnp.minimum(m + 1, nblk - 1), 0, 0)),
                      pl.BlockSpec(memory_space=pl.ANY),
                      pl.BlockSpec(memory_space=pl.ANY),
                      pl.BlockSpec((None, 1, 2 * MOE_TF), lambda j, m, be, *_: (be[m], 0, 0))],
            out_specs=[pl.BlockSpec((MOE_TM, d), lambda j, m, *_: (m, 0)),
                       pl.BlockSpec((MOE_TM, MOE_TF), lambda j, m, *_: (m, 0))],
            scratch_shapes=[pltpu.VMEM((2, MOE_TM, dg, LANES), BF16), pltpu.SemaphoreType.DMA((2,)),
                            pltpu.VMEM((d, 2 * MOE_TF), BF16), pltpu.VMEM((2, d, 2 * MOE_TF), F32),
                            pltpu.SemaphoreType.DMA((2,))]),
        out_shape=[jax.ShapeDtypeStruct((n_rows, d), BF16),
                   jax.ShapeDtypeStruct((n_rows, MOE_TF), BF16)],
        compiler_params=_params(("arbitrary", "arbitrary")),
        name="dispatch_gateup0",
    )(*sched, tok3, tok3, xn3d, w_gu, b_gu.reshape(e, 1, f2))


def _expert_changed(be_ref, m):
    return jnp.logical_or(m == 0, be_ref[m] != be_ref[jnp.maximum(m - 1, 0)])


def _for_live_block(bv_ref, m, o_ref, compute):
    live = bv_ref[m] > 0

    @pl.when(live)
    def _():
        compute(slice(None))

    @pl.when(jnp.logical_not(live))
    def _():
        o_ref[...] = jnp.zeros(o_ref.shape, o_ref.dtype)


def _stream_expert_chunk(be_ref, bv_ref, sw_ref, nx_ref, nsw_ref, w_hbm, wbuf, wsem, wb, chunk0=0):
    j = pl.program_id(0)
    m = pl.program_id(1)
    nj = pl.num_programs(0)
    tn = wbuf.shape[2]

    def chunk_copy(e, jj, slot):
        col = pl.multiple_of((jj + chunk0) * tn, tn)
        return pltpu.make_async_copy(w_hbm.at[e, :, pl.ds(col, tn)], wbuf.at[slot], wsem.at[slot])

    @pl.when(jnp.logical_and(bv_ref[m] > 0, _expert_changed(be_ref, m)))
    def _():
        k = j * nsw_ref[0] + sw_ref[m]
        slot = k % 2

        @pl.when(k == 0)
        def _():
            chunk_copy(be_ref[0], 0, 0).start()

        chunk_copy(be_ref[m], j, slot).wait()
        more_experts = nx_ref[m] >= 0

        @pl.when(more_experts)
        def _():
            chunk_copy(nx_ref[m], j, 1 - slot).start()

        @pl.when(jnp.logical_and(jnp.logical_not(more_experts), j + 1 < nj))
        def _():
            chunk_copy(be_ref[0], j + 1, 1 - slot).start()

        wb[...] = wbuf[slot].astype(BF16)


def _gateup_kernel(be_ref, bv_ref, nused_ref, sw_ref, nx_ref, nsw_ref, x_ref, w_hbm, bgu_ref, o_ref,
                   wb, wbuf, wsem):
    m = pl.program_id(1)
    _stream_expert_chunk(be_ref, bv_ref, sw_ref, nx_ref, nsw_ref, w_hbm, wbuf, wsem, wb, chunk0=1)

    def compute_rows(rows):
        o_ref[rows] = _swiglu_chunk(x_ref[rows], wb, bgu_ref).astype(o_ref.dtype)

    _for_live_block(bv_ref, m, o_ref, compute_rows)


def gateup_call(sched, xs, w_gu, b_gu):
    n_rows, dh = xs.shape
    e, d, f2 = w_gu.shape
    f = f2 // 2
    nblk = n_rows // MOE_TM
    npass = f // MOE_TF - 1

    return pl.pallas_call(
        _gateup_kernel,
        grid_spec=pltpu.PrefetchScalarGridSpec(
            num_scalar_prefetch=6,
            grid=(npass, nblk),
            in_specs=[pl.BlockSpec((MOE_TM, dh), lambda j, m, be, bv, nu, *_: (jnp.minimum(m, nu[0] - 1), 0)),
                      pl.BlockSpec(memory_space=pl.ANY),
                      pl.BlockSpec((None, 1, 2 * MOE_TF), lambda j, m, be, *_: (be[m], 0, j + 1))],
            out_specs=pl.BlockSpec((MOE_TM, MOE_TF), lambda j, m, *_: (m, j)),
            scratch_shapes=[pltpu.VMEM((d, 2 * MOE_TF), BF16), pltpu.VMEM((2, d, 2 * MOE_TF), F32),
                            pltpu.SemaphoreType.DMA((2,))]),
        out_shape=jax.ShapeDtypeStruct((n_rows, npass * MOE_TF), BF16),
        compiler_params=_params(("arbitrary", "arbitrary")),
        name="expert_gateup",
    )(*sched, xs, w_gu, b_gu.reshape(e, 1, f2))


def _down_kernel(be_ref, bv_ref, nused_ref, sw_ref, nx_ref, nsw_ref, a0_ref, a1_ref, w_hbm, bdn_ref,
                 o_ref, wb, wbuf, wsem):
    m = pl.program_id(1)
    _stream_expert_chunk(be_ref, bv_ref, sw_ref, nx_ref, nsw_ref, w_hbm, wbuf, wsem, wb)
    f0 = a0_ref.shape[1]

    def compute_rows(rows):
        out = (jnp.dot(a0_ref[rows], wb[:f0], preferred_element_type=F32)
               + jnp.dot(a1_ref[rows], wb[f0:], preferred_element_type=F32)) + bdn_ref[...]
        o_ref[rows] = _to_token_rows(out.astype(o_ref.dtype))

    _for_live_block(bv_ref, m, o_ref, compute_rows)


def down_call(sched, act0, act1, w_dn, b_dn):
    n_rows, f0 = act0.shape
    f1 = act1.shape[1]
    e, f, d = w_dn.shape
    assert f0 + f1 == f
    nblk = n_rows // MOE_TM
    row_blk = lambda j, m, be, bv, nu, *_: (jnp.minimum(m, nu[0] - 1), 0)

    return pl.pallas_call(
        _down_kernel,
        grid_spec=pltpu.PrefetchScalarGridSpec(
            num_scalar_prefetch=6,
            grid=(d // MOE_TN, nblk),
            in_specs=[pl.BlockSpec((MOE_TM, f0), row_blk),
                      pl.BlockSpec((MOE_TM, f1), row_blk),
                      pl.BlockSpec(memory_space=pl.ANY),
                      pl.BlockSpec((None, 1, MOE_TN), lambda j, m, be, *_: (be[m], 0, j))],
            out_specs=pl.BlockSpec((MOE_TM, MOE_TN // LANES, LANES), lambda j, m, *_: (m, j, 0)),
            scratch_shapes=[pltpu.VMEM((f, MOE_TN), BF16), pltpu.VMEM((2, f, MOE_TN), F32),
                            pltpu.SemaphoreType.DMA((2,))]),
        out_shape=jax.ShapeDtypeStruct((n_rows, d // LANES, LANES), BF16),
        compiler_params=_params(("arbitrary", "arbitrary")),
        name="expert_down",
    )(*sched, act0, act1, w_dn, b_dn.reshape(e, 1, d))


def _combine_kernel(pos_ref, posn_ref, ys_hbm, h_ref, gt_ref, g_ref, h_out, xn_out, buf, sem):
    i = pl.program_id(0)
    n = pl.num_programs(0)
    slot = i % 2

    def row_copy(row, sl, k, t):
        return pltpu.make_async_copy(ys_hbm.at[row], buf.at[sl, k, t], sem.at[sl])

    def issue(p_ref, sl):
        def body(g, c):
            rows = [p_ref[0, 0, g * DMA_UNROLL + u] for u in range(DMA_UNROLL)]
            for u in range(DMA_UNROLL):
                t = g * (DMA_UNROLL // TOP_K) + u // TOP_K
                row_copy(rows[u], sl, u % TOP_K, t).start(priority=ROW_DMA_PRIORITY)
            return c

        lax.fori_loop(0, CMB_TB * TOP_K // DMA_UNROLL, body, 0)

    @pl.when(i == 0)
    def _():
        issue(pos_ref, 0)

    @pl.when(i + 1 < n)
    def _():
        issue(posn_ref, 1 - slot)

    for k in range(TOP_K):
        pltpu.make_async_copy(ys_hbm.at[pl.ds(0, CMB_TB)], buf.at[slot, k], sem.at[slot]).wait()

    h = h_ref[...]
    for k in range(TOP_K):
        rows = buf[slot, k].reshape(h.shape).astype(F32)
        h = h + gt_ref[:, k:k + 1] * rows
    h_out[...] = h
    ms = jnp.mean(h * h, axis=-1, keepdims=True)
    xn_out[...] = (h * lax.rsqrt(ms + RMS_EPS) * g_ref[...]).astype(xn_out.dtype)


def combine_call(pos, ys, h, gates, g):
    s, d = h.shape
    nb = s // CMB_TB
    pos3 = pos.reshape(nb, 1, CMB_TB * TOP_K)
    pos_spec = lambda f: pl.BlockSpec((1, 1, CMB_TB * TOP_K), f, memory_space=pltpu.SMEM)
    return pl.pallas_call(
        _combine_kernel,
        grid=(nb,),
        in_specs=[pos_spec(lambda i: (i, 0, 0)),
                  pos_spec(lambda i: (jnp.minimum(i + 1, nb - 1), 0, 0)),
                  pl.BlockSpec(memory_space=pl.ANY),
                  pl.BlockSpec((CMB_TB, d), lambda i: (i, 0)),
                  pl.BlockSpec((CMB_TB, LANES), lambda i: (i, 0)),
                  pl.BlockSpec((1, d), lambda i: (0, 0))],
        out_specs=[pl.BlockSpec((CMB_TB, d), lambda i: (i, 0)),
                   pl.BlockSpec((CMB_TB, d), lambda i: (i, 0))],
        out_shape=[jax.ShapeDtypeStruct((s, d), F32), jax.ShapeDtypeStruct((s, d), BF16)],
        scratch_shapes=[pltpu.VMEM((2, TOP_K, CMB_TB, d // LANES, LANES), BF16),
                        pltpu.SemaphoreType.DMA((2,))],
        compiler_params=_params(("arbitrary",)),
        name="combine",
    )(pos3, pos3, ys, h, gates, g.reshape(1, d))


def moe_routing(top_e, n_tokens):
    e_flat = top_e.reshape(-1)
    onehot = (e_flat[:, None] == jnp.arange(N_EXPERTS, dtype=jnp.int32)[None, :]).astype(jnp.int32)
    csum = jnp.cumsum(onehot, axis=0)
    counts = csum[-1]
    rank = jnp.sum(csum * onehot, axis=1) - 1
    padded = (counts + MOE_TM - 1) // MOE_TM * MOE_TM
    pends = jnp.cumsum(padded).astype(jnp.int32)
    pstarts = pends - padded
    pos = pstarts[e_flat] + rank
    n_blocks = (n_tokens * TOP_K) // MOE_TM + N_EXPERTS
    tok_flat = jnp.arange(n_tokens * TOP_K, dtype=jnp.int32) // TOP_K
    row_tok = jnp.zeros((n_blocks * MOE_TM,), jnp.int32).at[pos].set(
        tok_flat, unique_indices=True, mode="promise_in_bounds")
    n_used = pends[-1] // MOE_TM
    blk_start = jnp.arange(n_blocks, dtype=jnp.int32) * MOE_TM
    block_e = jnp.sum((pends[None, :] <= blk_start[:, None]).astype(jnp.int32), axis=1)
    block_e = jnp.minimum(block_e, N_EXPERTS - 1)
    last_e = block_e[jnp.maximum(n_used - 1, 0)]
    used = jnp.arange(n_blocks) < n_used
    real_end = pstarts + counts
    block_valid = jnp.clip(real_end[block_e] - blk_start, 0, MOE_TM)
    block_valid = jnp.where(used, block_valid, 0).astype(jnp.int32)
    block_e = jnp.where(used, block_e, last_e).astype(jnp.int32)
    has_rows = counts > 0
    eidx = jnp.arange(N_EXPERTS, dtype=jnp.int32)
    later = jnp.logical_and(eidx[None, :] > eidx[:, None], has_rows[None, :])
    next_of = jnp.min(jnp.where(later, eidx[None, :], N_EXPERTS), axis=1)
    next_of = jnp.where(next_of < N_EXPERTS, next_of, -1)
    order_of = jnp.cumsum(has_rows.astype(jnp.int32)) - 1
    routing = dict(block_e=block_e, block_valid=block_valid,
                   n_used=n_used.reshape(1).astype(jnp.int32),
                   switch_idx=order_of[block_e].astype(jnp.int32),
                   next_e=next_of[block_e].astype(jnp.int32),
                   n_switch=jnp.sum(has_rows.astype(jnp.int32)).reshape(1))
    return pos.astype(jnp.int32), row_tok, routing


def kernel(x, p, g_mix, w_in, attn_sink, na_rpb, w_branch_a, w_branch_b, w_out, g_ffn, w_router,
           b_router, w_gate_up, b_gate_up, w_down, b_down, g_ple, w_ple_gate, w_ple, g_final):
    b, s, d = x.shape
    assert b == 1 and w_in.shape[0] == 1
    h = x.reshape(s, d)
    rope_c, rope_s1, rope_s2 = rope_tables(s)

    xn = rmsnorm_call(h, g_mix[0], BF16)
    proj = inproj_call(xn, w_in[0], rope_c, rope_s1, rope_s2, d)
    ya = window_attn_call(proj, attn_sink[0])
    yb = na_attn_call(proj, na_bias_tables(na_rpb[0]))
    merged = merge_call(ya, yb, w_branch_a[0], w_branch_b[0], proj, d)
    h = resmm_call(merged, w_out[0], h)

    xn2, top_e, gates = rms_router_call(h, g_ffn[0], w_router[0], b_router[0])
    pos, row_tok, rt = moe_routing(top_e[:, :TOP_K], s)
    sched = (rt["block_e"], rt["block_valid"], rt["n_used"], rt["switch_idx"], rt["next_e"], rt["n_switch"])
    xs, act0 = dispatch_gateup_call(sched, row_tok, xn2, w_gate_up[0], b_gate_up[0])
    act1 = gateup_call(sched, xs, w_gate_up[0], b_gate_up[0])
    ys = down_call(sched, act0, act1, w_down[0], b_down[0])
    h, xn3 = combine_call(pos, ys, h, gates, g_ple[0])

    h = ple_call(xn3, w_ple_gate[0], p[0].reshape(s, -1), w_ple[0], h)
    out = rmsnorm_call(h, g_final, F32)
    return out.reshape(b, s, d)
```

```python
import functools

import numpy as np
import jax
import jax.numpy as jnp
from jax import lax
from jax.experimental import pallas as pl
from jax.experimental.pallas import tpu as pltpu

F32 = jnp.float32
BF16 = jnp.bfloat16

HEAD_DIM = 64
A_HEADS = 32
A_KV_HEADS = 8
A_GROUP = A_HEADS // A_KV_HEADS
B_HEADS = 32
A_Q_W = A_HEADS * HEAD_DIM
A_KV_W = A_KV_HEADS * HEAD_DIM
B_W = B_HEADS * HEAD_DIM
ROT_DIM = HEAD_DIM // 4
ROPE_THETA = 500000.0
WINDOW = 128
GRID_W = 64
NA_ROWS = 8
NA_COLS = 16
N_EXPERTS = 32
TOP_K = 4
SWIGLU_LIMIT = 7.0
SWIGLU_ALPHA = 1.702
RMS_EPS = 1e-6
NEG_INF = -1e30

LANES = 128
VMEM_LIMIT = 56 * 1024 * 1024

RMS_ROWS = 256
MM_TM = 1024
MM_TN = 512
ATT_BLK = 128
ATT_QB = 2
NA_RB = 8
NA_HPS = 4
MOE_TM = 512
MOE_TF = 256
MOE_TN = 2048
CMB_TB = 128
DMA_UNROLL = 8
ROW_DMA_PRIORITY = 1


def _sigmoid(x):
    return 0.5 * jnp.tanh(0.5 * x) + 0.5


def _params(sem):
    return pltpu.CompilerParams(dimension_semantics=sem, vmem_limit_bytes=VMEM_LIMIT)


def _rms_body(x_ref, g_ref):
    x = x_ref[...].astype(F32)
    ms = jnp.mean(x * x, axis=-1, keepdims=True)
    return x * lax.rsqrt(ms + RMS_EPS) * g_ref[...]


def _rms_kernel(x_ref, g_ref, o_ref):
    o_ref[...] = _rms_body(x_ref, g_ref).astype(o_ref.dtype)


def rmsnorm_call(x, g, out_dtype):
    s, d = x.shape
    return pl.pallas_call(
        _rms_kernel,
        grid=(s // RMS_ROWS,),
        in_specs=[pl.BlockSpec((RMS_ROWS, d), lambda i: (i, 0)),
                  pl.BlockSpec((1, d), lambda i: (0, 0))],
        out_specs=pl.BlockSpec((RMS_ROWS, d), lambda i: (i, 0)),
        out_shape=jax.ShapeDtypeStruct((s, d), out_dtype),
        compiler_params=_params(("parallel",)),
        name="rmsnorm",
    )(x, g.reshape(1, d))


def _to_token_rows(x2d):
    return x2d.reshape(x2d.shape[0], x2d.shape[1] // LANES, LANES)


def _split_bf16(x):
    hi = x.astype(BF16)
    return hi, (x - hi.astype(F32)).astype(BF16)


def _rms_router_kernel(x_ref, g_ref, wh_ref, wl_ref, br_ref, xn_ref, e_ref, gt_ref):
    xn = _rms_body(x_ref, g_ref)
    xh, xl = _split_bf16(xn)
    xn_ref[...] = _to_token_rows(xh)
    logits = (jnp.dot(xh, wh_ref[...], preferred_element_type=F32)
              + jnp.dot(xh, wl_ref[...], preferred_element_type=F32)
              + jnp.dot(xl, wh_ref[...], preferred_element_type=F32)) + br_ref[...]
    lane = lax.broadcasted_iota(jnp.int32, logits.shape, 1)
    logits = jnp.where(lane < N_EXPERTS, logits, -jnp.inf)
    e_out = jnp.zeros(logits.shape, jnp.int32)
    v_out = jnp.zeros(logits.shape, F32)
    v0 = None
    for k in range(TOP_K):
        m = jnp.max(logits, axis=-1, keepdims=True)
        idx = jnp.min(jnp.where(logits == m, lane, LANES), axis=-1, keepdims=True)
        if k == 0:
            v0 = m
        e_out = jnp.where(lane == k, idx, e_out)
        v_out = jnp.where(lane == k, jnp.exp(m - v0), v_out)
        logits = jnp.where(lane == idx, -jnp.inf, logits)
    e_ref[...] = e_out
    gt_ref[...] = v_out / jnp.sum(v_out, axis=-1, keepdims=True)


def rms_router_call(h, g, w_router, b_router):
    s, d = h.shape
    wr = jnp.zeros((d, LANES), F32).at[:, :N_EXPERTS].set(w_router)
    br = jnp.zeros((1, LANES), F32).at[0, :N_EXPERTS].set(b_router)
    wr_hi, wr_lo = _split_bf16(wr)
    return pl.pallas_call(
        _rms_router_kernel,
        grid=(s // RMS_ROWS,),
        in_specs=[pl.BlockSpec((RMS_ROWS, d), lambda i: (i, 0)),
                  pl.BlockSpec((1, d), lambda i: (0, 0)),
                  pl.BlockSpec((d, LANES), lambda i: (0, 0)),
                  pl.BlockSpec((d, LANES), lambda i: (0, 0)),
                  pl.BlockSpec((1, LANES), lambda i: (0, 0))],
        out_specs=[pl.BlockSpec((RMS_ROWS, d // LANES, LANES), lambda i: (i, 0, 0)),
                   pl.BlockSpec((RMS_ROWS, LANES), lambda i: (i, 0)),
                   pl.BlockSpec((RMS_ROWS, LANES), lambda i: (i, 0))],
        out_shape=[jax.ShapeDtypeStruct((s, d // LANES, LANES), BF16),
                   jax.ShapeDtypeStruct((s, LANES), jnp.int32),
                   jax.ShapeDtypeStruct((s, LANES), F32)],
        compiler_params=_params(("parallel",)),
        name="rms_router",
    )(h, g.reshape(1, d), wr_hi, wr_lo, br)


def _rope(acc, c_ref, s1_ref, s2_ref):
    c, s1, s2 = c_ref[...], s1_ref[...], s2_ref[...]
    half = ROT_DIM // 2
    outs = []
    for t in range(acc.shape[1] // LANES):
        x = acc[:, t * LANES:(t + 1) * LANES]
        outs.append(x * c + pltpu.roll(x, LANES - half, 1) * s1 + pltpu.roll(x, half, 1) * s2)
    return jnp.concatenate(outs, axis=1)


def _stream_weight_chunk(w_hbm, wbuf, sem, wb_ref):
    j = pl.program_id(0)
    nj = pl.num_programs(0)
    tn = wbuf.shape[2]

    def chunk_copy(jj, slot):
        return pltpu.make_async_copy(w_hbm.at[:, pl.ds(pl.multiple_of(jj * tn, tn), tn)],
                                     wbuf.at[slot], sem.at[slot])

    @pl.when(pl.program_id(1) == 0)
    def _():
        slot = j % 2

        @pl.when(j == 0)
        def _():
            chunk_copy(0, 0).start()

        chunk_copy(j, slot).wait()

        @pl.when(j + 1 < nj)
        def _():
            chunk_copy(j + 1, 1 - slot).start()

        wb_ref[...] = wbuf[slot].astype(BF16)


def _inproj_kernel(a_ref, w_hbm, c_ref, s1_ref, s2_ref, o_ref, wb_ref, wbuf, wsem, *, n_rope, n_plain):
    j = pl.program_id(0)
    _stream_weight_chunk(w_hbm, wbuf, wsem, wb_ref)
    acc = jnp.dot(a_ref[...], wb_ref[...], preferred_element_type=F32)

    @pl.when(j < n_rope)
    def _():
        o_ref[...] = _rope(acc, c_ref, s1_ref, s2_ref).astype(o_ref.dtype)

    @pl.when(jnp.logical_and(j >= n_rope, j < n_plain))
    def _():
        o_ref[...] = acc.astype(o_ref.dtype)

    @pl.when(j >= n_plain)
    def _():
        o_ref[...] = _sigmoid(acc).astype(o_ref.dtype)


def inproj_call(xn, w_in, rope_c, rope_s1, rope_s2, d_model):
    s, d = xn.shape
    n = w_in.shape[1]
    n_rope = (A_Q_W + A_KV_W) // MM_TN
    n_plain = (n - 2 * d_model) // MM_TN
    tab = pl.BlockSpec((MM_TM, LANES), lambda j, i: (i, 0))
    return pl.pallas_call(
        functools.partial(_inproj_kernel, n_rope=n_rope, n_plain=n_plain),
        grid=(n // MM_TN, s // MM_TM),
        in_specs=[pl.BlockSpec((MM_TM, d), lambda j, i: (i, 0)),
                  pl.BlockSpec(memory_space=pl.ANY),
                  tab, tab, tab],
        out_specs=pl.BlockSpec((MM_TM, MM_TN), lambda j, i: (i, j)),
        out_shape=jax.ShapeDtypeStruct((s, n), BF16),
        scratch_shapes=[pltpu.VMEM((d, MM_TN), BF16), pltpu.VMEM((2, d, MM_TN), F32),
                        pltpu.SemaphoreType.DMA((2,))],
        compiler_params=_params(("arbitrary", "arbitrary")),
        name="inproj",
    )(xn, w_in, rope_c, rope_s1, rope_s2)


def rope_tables(s):
    half = ROT_DIM // 2
    inv = jnp.float32(ROPE_THETA) ** (-jnp.arange(half, dtype=F32) * (2.0 / ROT_DIM))
    ang = jnp.arange(s, dtype=F32)[:, None] * inv[None, :]
    cos, sin = jnp.cos(ang), jnp.sin(ang)
    ones = jnp.ones((s, HEAD_DIM - ROT_DIM), F32)
    zeros = jnp.zeros((s, HEAD_DIM - ROT_DIM), F32)
    zh = jnp.zeros((s, half), F32)
    c = jnp.concatenate([cos, cos, ones], axis=1)
    s1 = jnp.concatenate([-sin, zh, zeros], axis=1)
    s2 = jnp.concatenate([zh, sin, zeros], axis=1)
    rep = LANES // HEAD_DIM
    return jnp.tile(c, (1, rep)), jnp.tile(s1, (1, rep)), jnp.tile(s2, (1, rep))


def _win_kernel(sink_ref, q_ref, kp_ref, kc_ref, kn_ref, vp_ref, vc_ref, vn_ref, o_ref, *, seq):
    n0 = pl.program_id(0) * ATT_QB
    pr = pl.program_id(1)
    kcat = jnp.concatenate([kp_ref[...], kc_ref[...], kn_ref[...]], axis=0)
    vcat = jnp.concatenate([vp_ref[...], vc_ref[...], vn_ref[...]], axis=0)
    rows = A_GROUP * ATT_BLK
    nk = 3 * ATT_BLK
    grp = lax.broadcasted_iota(jnp.int32, (A_GROUP, 1, 1), 0)
    kv_per_blk = LANES // HEAD_DIM
    for b in range(ATT_QB):
        n = n0 + b
        q = q_ref[b * ATT_BLK:(b + 1) * ATT_BLK, :] * jnp.asarray(HEAD_DIM ** -0.5, BF16)
        qpos = n * ATT_BLK + lax.broadcasted_iota(jnp.int32, (ATT_BLK, nk), 0)
        kpos = (n - 1) * ATT_BLK + lax.broadcasted_iota(jnp.int32, (ATT_BLK, nk), 1)
        mask = ((jnp.abs(qpos - kpos) <= WINDOW) & (kpos >= 0) & (kpos < seq))[None]
        for kvh in range(kv_per_blk):
            k_h = kcat[b * ATT_BLK:b * ATT_BLK + nk, kvh * HEAD_DIM:(kvh + 1) * HEAD_DIM]
            v_h = vcat[b * ATT_BLK:b * ATT_BLK + nk, kvh * HEAD_DIM:(kvh + 1) * HEAD_DIM]
            h0 = kvh * A_GROUP
            qs = jnp.concatenate(
                [q[:, (h0 + g) * HEAD_DIM:(h0 + g + 1) * HEAD_DIM] for g in range(A_GROUP)], axis=0)
            sc = lax.dot_general(qs, k_h, (((1,), (1,)), ((), ())), preferred_element_type=F32)
            sc = jnp.where(mask, sc.reshape(A_GROUP, ATT_BLK, nk), NEG_INF)
            snk = jnp.zeros((A_GROUP, 1, 1), F32)
            for g in range(A_GROUP):
                sv = sink_ref[(pr * kv_per_blk + kvh) * A_GROUP + g]
                snk = jnp.where(grp == g, sv, snk)
            m = jnp.maximum(jnp.max(sc, axis=-1, keepdims=True), snk)
            p = jnp.exp(sc - m)
            denom = jnp.sum(p, axis=-1, keepdims=True) + jnp.exp(snk - m)
            attn = (p * (1.0 / denom)).astype(BF16).reshape(rows, nk)
            out = jnp.dot(attn, v_h, preferred_element_type=F32)
            for g in range(A_GROUP):
                o_ref[b * ATT_BLK:(b + 1) * ATT_BLK, (h0 + g) * HEAD_DIM:(h0 + g + 1) * HEAD_DIM] = (
                    out[g * ATT_BLK:(g + 1) * ATT_BLK].astype(o_ref.dtype))


def window_attn_call(proj, sink):
    s = proj.shape[0]
    nb = s // ATT_BLK
    qw = A_GROUP * LANES
    k0 = A_Q_W // LANES
    v0 = (A_Q_W + A_KV_W) // LANES
    edge = (ATT_BLK, LANES)
    own = (ATT_QB * ATT_BLK, LANES)
    prev = lambda n: jnp.maximum(n * ATT_QB - 1, 0)
    nxt = lambda n: jnp.minimum((n + 1) * ATT_QB, nb - 1)
    return pl.pallas_call(
        functools.partial(_win_kernel, seq=s),
        grid=(nb // ATT_QB, A_Q_W // qw),
        in_specs=[pl.BlockSpec(memory_space=pltpu.SMEM),
                  pl.BlockSpec((ATT_QB * ATT_BLK, qw), lambda n, p: (n, p)),
                  pl.BlockSpec(edge, lambda n, p: (prev(n), k0 + p)),
                  pl.BlockSpec(own, lambda n, p: (n, k0 + p)),
                  pl.BlockSpec(edge, lambda n, p: (nxt(n), k0 + p)),
                  pl.BlockSpec(edge, lambda n, p: (prev(n), v0 + p)),
                  pl.BlockSpec(own, lambda n, p: (n, v0 + p)),
                  pl.BlockSpec(edge, lambda n, p: (nxt(n), v0 + p))],
        out_specs=pl.BlockSpec((ATT_QB * ATT_BLK, qw), lambda n, p: (n, p)),
        out_shape=jax.ShapeDtypeStruct((s, A_Q_W), BF16),
        compiler_params=_params(("parallel", "parallel")),
        name="window_attn",
    )(sink, proj, proj, proj, proj, proj, proj, proj)


def na_bias_tables(rpb):
    c = np.arange(GRID_W)
    kc = np.arange(GRID_W)
    dc = np.clip(kc[None, :] - c[:, None] + NA_COLS - 1, 0, 2 * NA_COLS - 2)
    qcs = np.clip(c - NA_COLS // 2, 0, GRID_W - NA_COLS)
    cmask = (kc[None, :] >= qcs[:, None]) & (kc[None, :] < qcs[:, None] + NA_COLS)
    t = jnp.where(jnp.asarray(cmask)[None, None], rpb[:, :, dc].astype(F32), NEG_INF)
    t = jnp.transpose(t, (0, 2, 1, 3)).reshape(rpb.shape[0], GRID_W, (2 * NA_ROWS - 1) * GRID_W)
    return jnp.pad(t, ((0, 0), (0, 0), (GRID_W, 0)))


def _na_kernel(q_ref, kp_ref, kc_ref, kn_ref, vp_ref, vc_ref, vn_ref, b_ref, o_ref,
               kbuf, vbuf, s_scr, p_scr, *, grid_rows):
    rb = pl.program_id(1)
    blk = NA_RB * GRID_W
    kbuf[0:blk] = kp_ref[...]
    kbuf[blk:2 * blk] = kc_ref[...]
    kbuf[2 * blk:3 * blk] = kn_ref[...]
    vbuf[0:blk] = vp_ref[...]
    vbuf[blk:2 * blk] = vc_ref[...]
    vbuf[2 * blk:3 * blk] = vn_ref[...]
    nkeys = NA_ROWS * GRID_W
    width = NA_HPS * HEAD_DIM
    grp = NA_HPS * GRID_W
    row_head = lax.broadcasted_iota(jnp.int32, (grp, width), 0) // GRID_W
    lane_head = lax.broadcasted_iota(jnp.int32, (grp, width), 1) // HEAD_DIM
    own = row_head == lane_head
    out_head = lax.broadcasted_iota(jnp.int32, (GRID_W, width), 1) // HEAD_DIM
    nrb = grid_rows // NA_RB
    half = NA_ROWS // 2

    def rows(first_rb):
        starts = []
        for i in range(NA_RB):
            r = first_rb * NA_RB + i
            rs = min(max(r - half, 0), grid_rows - NA_ROWS)
            typ = rs - (r - half) + (half - 1)
            start = (rs - (first_rb - 1) * NA_RB) * GRID_W
            starts.append(start)
            qi = q_ref[i * GRID_W:(i + 1) * GRID_W, :] * jnp.asarray(HEAD_DIM ** -0.5, BF16)
            qs = jnp.where(own, jnp.concatenate([qi] * NA_HPS, axis=0), jnp.zeros((grp, width), BF16))
            sc = lax.dot_general(qs, kbuf[start:start + nkeys, :], (((1,), (1,)), ((), ())),
                                 preferred_element_type=F32)
            boff = (typ + 1) * GRID_W
            s_scr[i * grp:(i + 1) * grp, :] = sc + b_ref[:, :, boff:boff + nkeys].reshape(grp, nkeys)
        for i in range(NA_RB):
            sc = s_scr[i * grp:(i + 1) * grp, :]
            m = jnp.max(sc, axis=-1, keepdims=True)
            p = jnp.exp(sc - m)
            p_scr[i * grp:(i + 1) * grp, :] = (
                p * (1.0 / jnp.sum(p, axis=-1, keepdims=True))).astype(BF16)
        for i in range(NA_RB):
            start = starts[i]
            out = jnp.dot(p_scr[i * grp:(i + 1) * grp, :], vbuf[start:start + nkeys, :],
                          preferred_element_type=F32)
            res = out[:GRID_W]
            for h in range(1, NA_HPS):
                res = jnp.where(out_head == h, out[h * GRID_W:(h + 1) * GRID_W], res)
            o_ref[i * GRID_W:(i + 1) * GRID_W, :] = res.astype(o_ref.dtype)

    @pl.when(rb == 0)
    def _():
        rows(0)

    @pl.when(rb == nrb - 1)
    def _():
        rows(nrb - 1)

    @pl.when(jnp.logical_and(rb > 0, rb < nrb - 1))
    def _():
        rows(1)


def na_attn_call(proj, bias_tab):
    s = proj.shape[0]
    grid_rows = s // GRID_W
    nrb = grid_rows // NA_RB
    blk_rows = NA_RB * GRID_W
    width = NA_HPS * HEAD_DIM
    q0 = (A_Q_W + 2 * A_KV_W) // width
    k0 = q0 + B_W // width
    v0 = k0 + B_W // width
    blk = (blk_rows, width)
    hpb = NA_HPS
    prev = lambda r: jnp.maximum(r - 1, 0)
    cur = lambda r: r
    nxt = lambda r: jnp.minimum(r + 1, nrb - 1)
    return pl.pallas_call(
        functools.partial(_na_kernel, grid_rows=grid_rows),
        grid=(B_HEADS // hpb, nrb),
        in_specs=[pl.BlockSpec(blk, lambda h, r: (r, q0 + h)),
                  pl.BlockSpec(blk, lambda h, r: (prev(r), k0 + h)),
                  pl.BlockSpec(blk, lambda h, r: (cur(r), k0 + h)),
                  pl.BlockSpec(blk, lambda h, r: (nxt(r), k0 + h)),
                  pl.BlockSpec(blk, lambda h, r: (prev(r), v0 + h)),
                  pl.BlockSpec(blk, lambda h, r: (cur(r), v0 + h)),
                  pl.BlockSpec(blk, lambda h, r: (nxt(r), v0 + h)),
                  pl.BlockSpec((hpb, GRID_W, 2 * NA_ROWS * GRID_W), lambda h, r: (h, 0, 0))],
        out_specs=pl.BlockSpec(blk, lambda h, r: (r, h)),
        out_shape=jax.ShapeDtypeStruct((s, B_W), BF16),
        scratch_shapes=[pltpu.VMEM((3 * blk_rows, width), BF16),
                        pltpu.VMEM((3 * blk_rows, width), BF16),
                        pltpu.VMEM((NA_RB * hpb * GRID_W, NA_ROWS * GRID_W), F32),
                        pltpu.VMEM((NA_RB * hpb * GRID_W, NA_ROWS * GRID_W), BF16)],
        compiler_params=_params(("parallel", "arbitrary")),
        name="na_attn",
    )(proj, proj, proj, proj, proj, proj, proj, bias_tab)


def _merge_kernel(ya_ref, yb_ref, wa_ref, wb_ref, ga_ref, gb_ref, o_ref, wab, wbb):
    @pl.when(pl.program_id(1) == 0)
    def _():
        wab[...] = wa_ref[...].astype(BF16)
        wbb[...] = wb_ref[...].astype(BF16)

    a = jnp.dot(ya_ref[...], wab[...], preferred_element_type=F32)
    b = jnp.dot(yb_ref[...], wbb[...], preferred_element_type=F32)
    o_ref[...] = (ga_ref[...].astype(F32) * a + gb_ref[...].astype(F32) * b).astype(o_ref.dtype)


def merge_call(ya, yb, w_a, w_b, proj, d_model):
    s = ya.shape[0]
    ga0 = (proj.shape[1] - 2 * d_model) // MM_TN
    gb0 = ga0 + d_model // MM_TN
    return pl.pallas_call(
        _merge_kernel,
        grid=(d_model // MM_TN, s // MM_TM),
        in_specs=[pl.BlockSpec((MM_TM, A_Q_W), lambda j, i: (i, 0)),
                  pl.BlockSpec((MM_TM, B_W), lambda j, i: (i, 0)),
                  pl.BlockSpec((A_Q_W, MM_TN), lambda j, i: (0, j)),
                  pl.BlockSpec((B_W, MM_TN), lambda j, i: (0, j)),
                  pl.BlockSpec((MM_TM, MM_TN), lambda j, i: (i, ga0 + j)),
                  pl.BlockSpec((MM_TM, MM_TN), lambda j, i: (i, gb0 + j))],
        out_specs=pl.BlockSpec((MM_TM, MM_TN), lambda j, i: (i, j)),
        out_shape=jax.ShapeDtypeStruct((s, d_model), BF16),
        scratch_shapes=[pltpu.VMEM((A_Q_W, MM_TN), BF16), pltpu.VMEM((B_W, MM_TN), BF16)],
        compiler_params=_params(("parallel", "arbitrary")),
        name="merge",
    )(ya, yb, w_a, w_b, proj, proj)


def _resmm_kernel(a_ref, w_hbm, r_ref, o_ref, wb_ref, wbuf, wsem):
    _stream_weight_chunk(w_hbm, wbuf, wsem, wb_ref)
    o_ref[...] = r_ref[...] + jnp.dot(a_ref[...], wb_ref[...], preferred_element_type=F32)


def resmm_call(a, w, res):
    s, k = a.shape
    n = w.shape[1]
    return pl.pallas_call(
        _resmm_kernel,
        grid=(n // MM_TN, s // MM_TM),
        in_specs=[pl.BlockSpec((MM_TM, k), lambda j, i: (i, 0)),
                  pl.BlockSpec(memory_space=pl.ANY),
                  pl.BlockSpec((MM_TM, MM_TN), lambda j, i: (i, j))],
        out_specs=pl.BlockSpec((MM_TM, MM_TN), lambda j, i: (i, j)),
        out_shape=jax.ShapeDtypeStruct((s, n), F32),
        scratch_shapes=[pltpu.VMEM((k, MM_TN), BF16), pltpu.VMEM((2, k, MM_TN), F32),
                        pltpu.SemaphoreType.DMA((2,))],
        compiler_params=_params(("arbitrary", "arbitrary")),
        name="resmm",
    )(a, w, res)


def _ple_kernel(a_ref, wg_hbm, p_ref, wp_ref, h_ref, o_ref, wgb, wpb, wbuf, wsem):
    _stream_weight_chunk(wg_hbm, wbuf, wsem, wgb)

    @pl.when(pl.program_id(1) == 0)
    def _():
        wpb[...] = wp_ref[...].astype(BF16)

    gate = _sigmoid(jnp.dot(a_ref[...], wgb[...], preferred_element_type=F32))
    emb = jnp.dot(p_ref[...].astype(BF16), wpb[...], preferred_element_type=F32)
    o_ref[...] = h_ref[...] + gate * emb


def ple_call(xn, w_gate, p, w_ple, h):
    s, d = xn.shape
    pd = p.shape[1]
    return pl.pallas_call(
        _ple_kernel,
        grid=(d // MM_TN, s // MM_TM),
        in_specs=[pl.BlockSpec((MM_TM, d), lambda j, i: (i, 0)),
                  pl.BlockSpec(memory_space=pl.ANY),
                  pl.BlockSpec((MM_TM, pd), lambda j, i: (i, 0)),
                  pl.BlockSpec((pd, MM_TN), lambda j, i: (0, j)),
                  pl.BlockSpec((MM_TM, MM_TN), lambda j, i: (i, j))],
        out_specs=pl.BlockSpec((MM_TM, MM_TN), lambda j, i: (i, j)),
        out_shape=jax.ShapeDtypeStruct((s, d), F32),
        scratch_shapes=[pltpu.VMEM((d, MM_TN), BF16), pltpu.VMEM((pd, MM_TN), BF16),
                        pltpu.VMEM((2, d, MM_TN), F32), pltpu.SemaphoreType.DMA((2,))],
        compiler_params=_params(("arbitrary", "arbitrary")),
        name="ple",
    )(xn, w_gate, p, w_ple, h)


def _swiglu_chunk(x, wb, bgu_ref):
    hgu = jnp.dot(x, wb[...], preferred_element_type=F32) + bgu_ref[...]
    gate = jnp.minimum(hgu, SWIGLU_LIMIT)
    up = jnp.clip(hgu, -SWIGLU_LIMIT, SWIGLU_LIMIT)
    up = pltpu.roll(up, 2 * MOE_TF - 1, 1)
    act = gate * _sigmoid(gate * SWIGLU_ALPHA) * (up + 1.0)
    rr = lax.broadcasted_iota(jnp.int32, (2 * MOE_TF, MOE_TF), 0)
    cc = lax.broadcasted_iota(jnp.int32, (2 * MOE_TF, MOE_TF), 1)
    sel = (rr == 2 * cc).astype(BF16)
    return jnp.dot(act.astype(BF16), sel, preferred_element_type=F32)


def _dispatch_gateup_kernel(be_ref, bv_ref, nused_ref, sw_ref, nx_ref, nsw_ref, tok_ref, tokn_ref,
                            x_hbm, w_hbm, bgu_ref, xs_ref, act_ref, gbuf, gsem, wb, wbuf, wsem):
    m = pl.program_id(1)
    nused = nused_ref[0]
    slot = m % 2
    _stream_expert_chunk(be_ref, bv_ref, sw_ref, nx_ref, nsw_ref, w_hbm, wbuf, wsem, wb)

    def row_copy(tok, sl, r):
        return pltpu.make_async_copy(x_hbm.at[tok], gbuf.at[sl, r], gsem.at[sl])

    groups = MOE_TM // DMA_UNROLL

    def issue(t_ref, sl, g_lo, g_hi):
        def body(g, c):
            rows = [g * DMA_UNROLL + u for u in range(DMA_UNROLL)]
            toks = [t_ref[0, 0, r] for r in rows]
            for u in range(DMA_UNROLL):
                row_copy(toks[u], sl, rows[u]).start(priority=ROW_DMA_PRIORITY)
            return c

        lax.fori_loop(g_lo, g_hi, body, 0)

    @pl.when(m == 0)
    def _():
        issue(tok_ref, 0, 0, groups)

    @pl.when(m + 1 < nused)
    def _():
        issue(tokn_ref, 1 - slot, 0, groups // 2)

    @pl.when(m < nused)
    def _():
        pltpu.make_async_copy(x_hbm.at[pl.ds(0, MOE_TM)], gbuf.at[slot], gsem.at[slot]).wait()
        x = gbuf[slot].reshape(xs_ref.shape)
        xs_ref[...] = x
        act_ref[...] = _swiglu_chunk(x, wb, bgu_ref).astype(act_ref.dtype)

    @pl.when(m + 1 < nused)
    def _():
        issue(tokn_ref, 1 - slot, groups // 2, groups)

    @pl.when(m >= nused)
    def _():
        xs_ref[...] = jnp.zeros(xs_ref.shape, xs_ref.dtype)
        act_ref[...] = jnp.zeros(act_ref.shape, act_ref.dtype)


def dispatch_gateup_call(sched, row_tok, xn3d, w_gu, b_gu):
    s, dg, _ = xn3d.shape
    e, d, f2 = w_gu.shape
    n_rows = row_tok.shape[0]
    nblk = n_rows // MOE_TM
    tok3 = row_tok.reshape(nblk, 1, MOE_TM)
    tok_spec = lambda f: pl.BlockSpec((1, 1, MOE_TM), f, memory_space=pltpu.SMEM)
    return pl.pallas_call(
        _dispatch_gateup_kernel,
        grid_spec=pltpu.PrefetchScalarGridSpec(
            num_scalar_prefetch=6,
            grid=(1, nblk),
            in_specs=[tok_spec(lambda j, m, *_: (m, 0, 0)),
                      tok_spec(lambda j, m, *_: (jnp.minimum(m + 1, nblk - 1), 0, 0)),
                      pl.BlockSpec(memory_space=pl.ANY),
                      pl.BlockSpec(memory_space=pl.ANY),
                      pl.BlockSpec((None, 1, 2 * MOE_TF), lambda j, m, be, *_: (be[m], 0, 0))],
            out_specs=[pl.BlockSpec((MOE_TM, d), lambda j, m, *_: (m, 0)),
                       pl.BlockSpec((MOE_TM, MOE_TF), lambda j, m, *_: (m, 0))],
            scratch_shapes=[pltpu.VMEM((2, MOE_TM, dg, LANES), BF16), pltpu.SemaphoreType.DMA((2,)),
                            pltpu.VMEM((d, 2 * MOE_TF), BF16), pltpu.VMEM((2, d, 2 * MOE_TF), F32),
                            pltpu.SemaphoreType.DMA((2,))]),
        out_shape=[jax.ShapeDtypeStruct((n_rows, d), BF16),
                   jax.ShapeDtypeStruct((n_rows, MOE_TF), BF16)],
        compiler_params=_params(("arbitrary", "arbitrary")),
        name="dispatch_gateup0",
    )(*sched, tok3, tok3, xn3d, w_gu, b_gu.reshape(e, 1, f2))


def _expert_changed(be_ref, m):
    return jnp.logical_or(m == 0, be_ref[m] != be_ref[jnp.maximum(m - 1, 0)])


def _for_live_block(bv_ref, m, o_ref, compute):
    live = bv_ref[m] > 0

    @pl.when(live)
    def _():
        compute(slice(None))

    @pl.when(jnp.logical_not(live))
    def _():
        o_ref[...] = jnp.zeros(o_ref.shape, o_ref.dtype)


def _stream_expert_chunk(be_ref, bv_ref, sw_ref, nx_ref, nsw_ref, w_hbm, wbuf, wsem, wb, chunk0=0):
    j = pl.program_id(0)
    m = pl.program_id(1)
    nj = pl.num_programs(0)
    tn = wbuf.shape[2]

    def chunk_copy(e, jj, slot):
        col = pl.multiple_of((jj + chunk0) * tn, tn)
        return pltpu.make_async_copy(w_hbm.at[e, :, pl.ds(col, tn)], wbuf.at[slot], wsem.at[slot])

    @pl.when(jnp.logical_and(bv_ref[m] > 0, _expert_changed(be_ref, m)))
    def _():
        k = j * nsw_ref[0] + sw_ref[m]
        slot = k % 2

        @pl.when(k == 0)
        def _():
            chunk_copy(be_ref[0], 0, 0).start()

        chunk_copy(be_ref[m], j, slot).wait()
        more_experts = nx_ref[m] >= 0

        @pl.when(more_experts)
        def _():
            chunk_copy(nx_ref[m], j, 1 - slot).start()

        @pl.when(jnp.logical_and(jnp.logical_not(more_experts), j + 1 < nj))
        def _():
            chunk_copy(be_ref[0], j + 1, 1 - slot).start()

        wb[...] = wbuf[slot].astype(BF16)


def _gateup_kernel(be_ref, bv_ref, nused_ref, sw_ref, nx_ref, nsw_ref, x_ref, w_hbm, bgu_ref, o_ref,
                   wb, wbuf, wsem):
    m = pl.program_id(1)
    _stream_expert_chunk(be_ref, bv_ref, sw_ref, nx_ref, nsw_ref, w_hbm, wbuf, wsem, wb, chunk0=1)

    def compute_rows(rows):
        o_ref[rows] = _swiglu_chunk(x_ref[rows], wb, bgu_ref).astype(o_ref.dtype)

    _for_live_block(bv_ref, m, o_ref, compute_rows)


def gateup_call(sched, xs, w_gu, b_gu):
    n_rows, dh = xs.shape
    e, d, f2 = w_gu.shape
    f = f2 // 2
    nblk = n_rows // MOE_TM
    npass = f // MOE_TF - 1

    return pl.pallas_call(
        _gateup_kernel,
        grid_spec=pltpu.PrefetchScalarGridSpec(
            num_scalar_prefetch=6,
            grid=(npass, nblk),
            in_specs=[pl.BlockSpec((MOE_TM, dh), lambda j, m, be, bv, nu, *_: (jnp.minimum(m, nu[0] - 1), 0)),
                      pl.BlockSpec(memory_space=pl.ANY),
                      pl.BlockSpec((None, 1, 2 * MOE_TF), lambda j, m, be, *_: (be[m], 0, j + 1))],
            out_specs=pl.BlockSpec((MOE_TM, MOE_TF), lambda j, m, *_: (m, j)),
            scratch_shapes=[pltpu.VMEM((d, 2 * MOE_TF), BF16), pltpu.VMEM((2, d, 2 * MOE_TF), F32),
                            pltpu.SemaphoreType.DMA((2,))]),
        out_shape=jax.ShapeDtypeStruct((n_rows, npass * MOE_TF), BF16),
        compiler_params=_params(("arbitrary", "arbitrary")),
        name="expert_gateup",
    )(*sched, xs, w_gu, b_gu.reshape(e, 1, f2))


def _down_kernel(be_ref, bv_ref, nused_ref, sw_ref, nx_ref, nsw_ref, a0_ref, a1_ref, w_hbm, bdn_ref,
                 o_ref, wb, wbuf, wsem):
    m = pl.program_id(1)
    _stream_expert_chunk(be_ref, bv_ref, sw_ref, nx_ref, nsw_ref, w_hbm, wbuf, wsem, wb)
    f0 = a0_ref.shape[1]

    def compute_rows(rows):
        out = (jnp.dot(a0_ref[rows], wb[:f0], preferred_element_type=F32)
               + jnp.dot(a1_ref[rows], wb[f0:], preferred_element_type=F32)) + bdn_ref[...]
        o_ref[rows] = _to_token_rows(out.astype(o_ref.dtype))

    _for_live_block(bv_ref, m, o_ref, compute_rows)


def down_call(sched, act0, act1, w_dn, b_dn):
    n_rows, f0 = act0.shape
    f1 = act1.shape[1]
    e, f, d = w_dn.shape
    assert f0 + f1 == f
    nblk = n_rows // MOE_TM
    row_blk = lambda j, m, be, bv, nu, *_: (jnp.minimum(m, nu[0] - 1), 0)

    return pl.pallas_call(
        _down_kernel,
        grid_spec=pltpu.PrefetchScalarGridSpec(
            num_scalar_prefetch=6,
            grid=(d // MOE_TN, nblk),
            in_specs=[pl.BlockSpec((MOE_TM, f0), row_blk),
                      pl.BlockSpec((MOE_TM, f1), row_blk),
                      pl.BlockSpec(memory_space=pl.ANY),
                      pl.BlockSpec((None, 1, MOE_TN), lambda j, m, be, *_: (be[m], 0, j))],
            out_specs=pl.BlockSpec((MOE_TM, MOE_TN // LANES, LANES), lambda j, m, *_: (m, j, 0)),
            scratch_shapes=[pltpu.VMEM((f, MOE_TN), BF16), pltpu.VMEM((2, f, MOE_TN), F32),
                            pltpu.SemaphoreType.DMA((2,))]),
        out_shape=jax.ShapeDtypeStruct((n_rows, d // LANES, LANES), BF16),
        compiler_params=_params(("arbitrary", "arbitrary")),
        name="expert_down",
    )(*sched, act0, act1, w_dn, b_dn.reshape(e, 1, d))


def _combine_kernel(pos_ref, posn_ref, ys_hbm, h_ref, gt_ref, g_ref, h_out, xn_out, buf, sem):
    i = pl.program_id(0)
    n = pl.num_programs(0)
    slot = i % 2

    def row_copy(row, sl, k, t):
        return pltpu.make_async_copy(ys_hbm.at[row], buf.at[sl, k, t], sem.at[sl])

    def issue(p_ref, sl):
        def body(g, c):
            rows = [p_ref[0, 0, g * DMA_UNROLL + u] for u in range(DMA_UNROLL)]
            for u in range(DMA_UNROLL):
                t = g * (DMA_UNROLL // TOP_K) + u // TOP_K
                row_copy(rows[u], sl, u % TOP_K, t).start(priority=ROW_DMA_PRIORITY)
            return c

        lax.fori_loop(0, CMB_TB * TOP_K // DMA_UNROLL, body, 0)

    @pl.when(i == 0)
    def _():
        issue(pos_ref, 0)

    @pl.when(i + 1 < n)
    def _():
        issue(posn_ref, 1 - slot)

    for k in range(TOP_K):
        pltpu.make_async_copy(ys_hbm.at[pl.ds(0, CMB_TB)], buf.at[slot, k], sem.at[slot]).wait()

    h = h_ref[...]
    for k in range(TOP_K):
        rows = buf[slot, k].reshape(h.shape).astype(F32)
        h = h + gt_ref[:, k:k + 1] * rows
    h_out[...] = h
    ms = jnp.mean(h * h, axis=-1, keepdims=True)
    xn_out[...] = (h * lax.rsqrt(ms + RMS_EPS) * g_ref[...]).astype(xn_out.dtype)


def combine_call(pos, ys, h, gates, g):
    s, d = h.shape
    nb = s // CMB_TB
    pos3 = pos.reshape(nb, 1, CMB_TB * TOP_K)
    pos_spec = lambda f: pl.BlockSpec((1, 1, CMB_TB * TOP_K), f, memory_space=pltpu.SMEM)
    return pl.pallas_call(
        _combine_kernel,
        grid=(nb,),
        in_specs=[pos_spec(lambda i: (i, 0, 0)),
                  pos_spec(lambda i: (jnp.minimum(i + 1, nb - 1), 0, 0)),
                  pl.BlockSpec(memory_space=pl.ANY),
                  pl.BlockSpec((CMB_TB, d), lambda i: (i, 0)),
                  pl.BlockSpec((CMB_TB, LANES), lambda i: (i, 0)),
                  pl.BlockSpec((1, d), lambda i: (0, 0))],
        out_specs=[pl.BlockSpec((CMB_TB, d), lambda i: (i, 0)),
                   pl.BlockSpec((CMB_TB, d), lambda i: (i, 0))],
        out_shape=[jax.ShapeDtypeStruct((s, d), F32), jax.ShapeDtypeStruct((s, d), BF16)],
        scratch_shapes=[pltpu.VMEM((2, TOP_K, CMB_TB, d // LANES, LANES), BF16),
                        pltpu.SemaphoreType.DMA((2,))],
        compiler_params=_params(("arbitrary",)),
        name="combine",
    )(pos3, pos3, ys, h, gates, g.reshape(1, d))


def moe_routing(top_e, n_tokens):
    e_flat = top_e.reshape(-1)
    onehot = (e_flat[:, None] == jnp.arange(N_EXPERTS, dtype=jnp.int32)[None, :]).astype(jnp.int32)
    csum = jnp.cumsum(onehot, axis=0)
    counts = csum[-1]
    rank = jnp.sum(csum * onehot, axis=1) - 1
    padded = (counts + MOE_TM - 1) // MOE_TM * MOE_TM
    pends = jnp.cumsum(padded).astype(jnp.int32)
    pstarts = pends - padded
    pos = pstarts[e_flat] + rank
    n_blocks = (n_tokens * TOP_K) // MOE_TM + N_EXPERTS
    tok_flat = jnp.arange(n_tokens * TOP_K, dtype=jnp.int32) // TOP_K
    row_tok = jnp.zeros((n_blocks * MOE_TM,), jnp.int32).at[pos].set(
        tok_flat, unique_indices=True, mode="promise_in_bounds")
    n_used = pends[-1] // MOE_TM
    blk_start = jnp.arange(n_blocks, dtype=jnp.int32) * MOE_TM
    block_e = jnp.sum((pends[None, :] <= blk_start[:, None]).astype(jnp.int32), axis=1)
    block_e = jnp.minimum(block_e, N_EXPERTS - 1)
    last_e = block_e[jnp.maximum(n_used - 1, 0)]
    used = jnp.arange(n_blocks) < n_used
    real_end = pstarts + counts
    block_valid = jnp.clip(real_end[block_e] - blk_start, 0, MOE_TM)
    block_valid = jnp.where(used, block_valid, 0).astype(jnp.int32)
    block_e = jnp.where(used, block_e, last_e).astype(jnp.int32)
    has_rows = counts > 0
    eidx = jnp.arange(N_EXPERTS, dtype=jnp.int32)
    later = jnp.logical_and(eidx[None, :] > eidx[:, None], has_rows[None, :])
    next_of = jnp.min(jnp.where(later, eidx[None, :], N_EXPERTS), axis=1)
    next_of = jnp.where(next_of < N_EXPERTS, next_of, -1)
    order_of = jnp.cumsum(has_rows.astype(jnp.int32)) - 1
    routing = dict(block_e=block_e, block_valid=block_valid,
                   n_used=n_used.reshape(1).astype(jnp.int32),
                   switch_idx=order_of[block_e].astype(jnp.int32),
                   next_e=next_of[block_e].astype(jnp.int32),
                   n_switch=jnp.sum(has_rows.astype(jnp.int32)).reshape(1))
    return pos.astype(jnp.int32), row_tok, routing


def kernel(x, p, g_mix, w_in, attn_sink, na_rpb, w_branch_a, w_branch_b, w_out, g_ffn, w_router,
           b_router, w_gate_up, b_gate_up, w_down, b_down, g_ple, w_ple_gate, w_ple, g_final):
    b, s, d = x.shape
    assert b == 1 and w_in.shape[0] == 1
    h = x.reshape(s, d)
    rope_c, rope_s1, rope_s2 = rope_tables(s)

    xn = rmsnorm_call(h, g_mix[0], BF16)
    proj = inproj_call(xn, w_in[0], rope_c, rope_s1, rope_s2, d)
    ya = window_attn_call(proj, attn_sink[0])
    yb = na_attn_call(proj, na_bias_tables(na_rpb[0]))
    merged = merge_call(ya, yb, w_branch_a[0], w_branch_b[0], proj, d)
    h = resmm_call(merged, w_out[0], h)

    xn2, top_e, gates = rms_router_call(h, g_ffn[0], w_router[0], b_router[0])
    pos, row_tok, rt = moe_routing(top_e[:, :TOP_K], s)
    sched = (rt["block_e"], rt["block_valid"], rt["n_used"], rt["switch_idx"], rt["next_e"], rt["n_switch"])
    xs, act0 = dispatch_gateup_call(sched, row_tok, xn2, w_gate_up[0], b_gate_up[0])
    act1 = gateup_call(sched, xs, w_gate_up[0], b_gate_up[0])
    ys = down_call(sched, act0, act1, w_down[0], b_down[0])
    h, xn3 = combine_call(pos, ys, h, gates, g_ple[0])

    h = ple_call(xn3, w_ple_gate[0], p[0].reshape(s, -1), w_ple[0], h)
    out = rmsnorm_call(h, g_final, F32)
    return out.reshape(b, s, d)
```
